```python
import math
import jax, jax.numpy as jnp
from jax import lax
import numpy as np

D_MODEL = 1024
BATCH = 1
SEQ = 16384
DEPTH = 2

SSD_HEAD_DIM = 64
SSD_D_INNER = D_MODEL
SSD_HEADS = SSD_D_INNER // SSD_HEAD_DIM
SSD_GROUPS = 4
SSD_STATE = 128
SSD_CONV = 4
SSD_CHUNK = 256
SSD_XBC = SSD_D_INNER + 2 * SSD_GROUPS * SSD_STATE
DT_MIN = 0.001
DT_MAX = 0.1
S5_WIDTH = D_MODEL
S5_GROUP_CH = 16
S5_GROUPS = S5_WIDTH // S5_GROUP_CH
S5_STATE = 64
S5_CLIP = 1e-4
MOBA_HEAD_DIM = 64
MOBA_WIDTH = D_MODEL
MOBA_HEADS = MOBA_WIDTH // MOBA_HEAD_DIM
MOBA_BLOCK = 256
MOBA_TOPK = 3
MOBA_Q_CHUNK = 64
N_BRANCH = 3
EPS = 1e-6
SPLIT_SIZES = (SSD_D_INNER, SSD_XBC, SSD_HEADS,
               S5_WIDTH, S5_WIDTH,
               MOBA_WIDTH, MOBA_WIDTH, MOBA_WIDTH, MOBA_WIDTH,
               N_BRANCH * D_MODEL)
IN_WIDTH = sum(SPLIT_SIZES)
SPLIT_POINTS = tuple(int(v) for v in np.cumsum(SPLIT_SIZES)[:-1])

kernel_name = "hybrid_ssd_s5_moba_gated"


def rmsnorm(x, w):
    xf = x.astype(jnp.float32)
    y = xf * lax.rsqrt(jnp.mean(xf * xf, axis=-1, keepdims=True) + EPS)
    return (y * w.astype(jnp.float32)).astype(x.dtype)


def pad_seq(t, mult):
    s = t.shape[1]
    sp = -(-s // mult) * mult
    pads = [(0, 0)] * t.ndim
    pads[1] = (0, sp - s)
    return jnp.pad(t, pads)


def causal_depthwise_conv(u, w, bias):
    k, c = w.shape
    out = lax.conv_general_dilated(u, w[:, None, :].astype(u.dtype), window_strides=(1,),
                                   padding=[(k - 1, 0)],
                                   dimension_numbers=("NWC", "WIO", "NWC"),
                                   feature_group_count=c)
    return out + bias.astype(u.dtype)


def segsum(a):
    t = a.shape[-1]
    cs = jnp.cumsum(a, axis=-1)
    diff = cs[..., :, None] - cs[..., None, :]
    return jnp.where(jnp.tril(jnp.ones((t, t), dtype=bool)), diff, -jnp.inf)


def ssd_chunked(xs, da, bm, cm):
    b, s, h, p = xs.shape
    g, n = bm.shape[2], bm.shape[3]
    j = h // g
    xs, da, bm, cm = (pad_seq(t, SSD_CHUNK) for t in (xs, da, bm, cm))
    nc = xs.shape[1] // SSD_CHUNK
    X = xs.reshape(b, nc, SSD_CHUNK, g, j, p)
    A = da.reshape(b, nc, SSD_CHUNK, g, j).transpose(0, 3, 4, 1, 2)
    Bc = bm.reshape(b, nc, SSD_CHUNK, g, n)
    Cc = cm.reshape(b, nc, SSD_CHUNK, g, n)
    a_cum = jnp.cumsum(A, axis=-1)
    L = jnp.exp(segsum(A))
    cb = jnp.einsum("bclgn,bcsgn->bgcls", Cc, Bc)
    y_diag = jnp.einsum("bgcls,bgjcls,bcsgjp->bclgjp", cb, L, X)
    decay_states = jnp.exp(a_cum[..., -1:] - a_cum)
    states = jnp.einsum("bclgn,bgjcl,bclgjp->bcgjpn", Bc, decay_states, X)
    states = jnp.concatenate([jnp.zeros_like(states[:, :1]), states], axis=1)
    chunk_tot = jnp.pad(a_cum[..., -1], ((0, 0), (0, 0), (0, 0), (1, 0)))
    decay_chunk = jnp.exp(segsum(chunk_tot))
    states = jnp.einsum("bgjzc,bcgjpn->bzgjpn", decay_chunk, states)[:, :-1]
    y_off = jnp.einsum("bclgn,bcgjpn,bgjcl->bclgjp", Cc, states, jnp.exp(a_cum))
    y = (y_diag + y_off).reshape(b, nc * SSD_CHUNK, h, p)
    return y[:, :s]


def ssd_branch(z, xbc, dt_raw, conv_w, conv_b, dt_bias, a_log, d_skip, norm_w):
    b, s, _ = z.shape
    f32 = jnp.float32
    xbc = jax.nn.silu(causal_depthwise_conv(xbc, conv_w, conv_b)).astype(f32)
    xs, bm, cm = jnp.split(xbc, [SSD_D_INNER, SSD_D_INNER + SSD_GROUPS * SSD_STATE], axis=-1)
    xs = xs.reshape(b, s, SSD_HEADS, SSD_HEAD_DIM)
    bm = bm.reshape(b, s, SSD_GROUPS, SSD_STATE)
    cm = cm.reshape(b, s, SSD_GROUPS, SSD_STATE)
    dt = jax.nn.softplus(dt_raw.astype(f32) + dt_bias.astype(f32))
    a = -jnp.exp(a_log.astype(f32))
    y = ssd_chunked(xs * dt[..., None], dt * a, bm, cm) + d_skip.astype(f32)[:, None] * xs
    y = y.reshape(b, s, SSD_D_INNER) * jax.nn.silu(z.astype(f32))
    yg = y.reshape(b, s, SSD_GROUPS, SSD_D_INNER // SSD_GROUPS)
    yg = yg * lax.rsqrt(jnp.mean(yg * yg, axis=-1, keepdims=True) + EPS)
    return (yg.reshape(b, s, SSD_D_INNER) * norm_w.astype(f32)).astype(z.dtype)


def s5_ssm(u, lam_re, lam_im, log_step, b_re, b_im, c_re, c_im, d_skip):
    f32 = jnp.float32
    b, s, _ = u.shape
    ug = u.reshape(b, s, S5_GROUPS, S5_GROUP_CH)
    step = jnp.exp(log_step.astype(f32))[:, None]
    lr = jnp.minimum(lam_re.astype(f32), -S5_CLIP)
    li = lam_im.astype(f32)
    mag = jnp.exp(lr * step)
    ab_re, ab_im = mag * jnp.cos(li * step), mag * jnp.sin(li * step)
    nr, ni = ab_re - 1.0, ab_im
    den = lr * lr + li * li
    coef_re = (nr * lr + ni * li) / den
    coef_im = (ni * lr - nr * li) / den
    br, bi = b_re.astype(f32), b_im.astype(f32)
    bb_re = coef_re[..., None] * br - coef_im[..., None] * bi
    bb_im = coef_re[..., None] * bi + coef_im[..., None] * br
    bu_re = jnp.einsum("bsgc,gpc->bsgp", ug, bb_re)
    bu_im = jnp.einsum("bsgc,gpc->bsgp", ug, bb_im)
    a_re = jnp.broadcast_to(ab_re, bu_re.shape)
    a_im = jnp.broadcast_to(ab_im, bu_im.shape)

    def combine(e1, e2):
        a1r, a1i, b1r, b1i = e1
        a2r, a2i, b2r, b2i = e2
        return (a2r * a1r - a2i * a1i,
                a2r * a1i + a2i * a1r,
                a2r * b1r - a2i * b1i + b2r,
                a2r * b1i + a2i * b1r + b2i)

    _, _, s_re, s_im = lax.associative_scan(combine, (a_re, a_im, bu_re, bu_im), axis=1)
    y = (jnp.einsum("bsgp,gcp->bsgc", s_re, c_re.astype(f32))
         - jnp.einsum("bsgp,gcp->bsgc", s_im, c_im.astype(f32)))
    return y.reshape(b, s, S5_WIDTH) + d_skip.astype(f32) * u


def s5_branch(u, gate, lam_re, lam_im, log_step, b_re, b_im, c_re, c_im, d_skip, glu_w, glu_b):
    y = s5_ssm(u.astype(jnp.float32), lam_re, lam_im, log_step, b_re, b_im, c_re, c_im, d_skip)
    y = jax.nn.gelu(y).astype(u.dtype)
    y = y * jax.nn.sigmoid(y @ glu_w + glu_b)
    return y * jax.nn.silu(gate)


def moba_attention(q, k, v):
    f32 = jnp.float32
    b, s, h, hd = q.shape
    q, k, v = (pad_seq(t, MOBA_BLOCK) for t in (q, k, v))
    sp = q.shape[1]
    nb = sp // MOBA_BLOCK
    topk = min(MOBA_TOPK, nb)
    kb = k.reshape(b, nb, MOBA_BLOCK, h, hd).transpose(0, 3, 1, 2, 4)
    vb = v.reshape(b, nb, MOBA_BLOCK, h, hd).transpose(0, 3, 1, 2, 4)
    kmean = jnp.mean(kb.astype(f32), axis=3).astype(k.dtype)
    nq = sp // MOBA_Q_CHUNK
    q_chunks = q.reshape(b, nq, MOBA_Q_CHUNK, h, hd).transpose(1, 0, 3, 2, 4)
    scale = hd ** -0.5
    chunks_per_block = MOBA_BLOCK // MOBA_Q_CHUNK
    bi = jnp.arange(b)[:, None, None, None]
    hi = jnp.arange(h)[None, :, None, None]
    key_pos = jnp.arange(MOBA_BLOCK)
    block_ids = jnp.arange(nb)

    def attend(args):
        qi, ci = args
        blk = ci // chunks_per_block
        s_gate = jnp.einsum("bhqd,bhnd->bhqn", qi, kmean, preferred_element_type=f32)
        s_gate = jnp.where(block_ids < blk, s_gate, -jnp.inf)
        _, sel = lax.top_k(s_gate, topk)
        valid = jnp.arange(topk) < blk
        k_sel = kb[bi, hi, sel]
        v_sel = vb[bi, hi, sel]
        s_sel = jnp.einsum("bhqd,bhqksd->bhqks", qi, k_sel, preferred_element_type=f32) * scale
        s_sel = jnp.where(valid[:, None], s_sel, -jnp.inf)
        k_own = lax.dynamic_index_in_dim(kb, blk, axis=2, keepdims=False)
        v_own = lax.dynamic_index_in_dim(vb, blk, axis=2, keepdims=False)
        q_pos = (ci % chunks_per_block) * MOBA_Q_CHUNK + jnp.arange(MOBA_Q_CHUNK)
        s_own = jnp.einsum("bhqd,bhsd->bhqs", qi, k_own, preferred_element_type=f32) * scale
        s_own = jnp.where(key_pos[None, :] <= q_pos[:, None], s_own, -jnp.inf)
        scores = jnp.concatenate([s_sel.reshape(b, h, MOBA_Q_CHUNK, topk * MOBA_BLOCK), s_own], axis=-1)
        p = jax.nn.softmax(scores, axis=-1).astype(v.dtype)
        p_sel = p[..., :topk * MOBA_BLOCK].reshape(b, h, MOBA_Q_CHUNK, topk, MOBA_BLOCK)
        p_own = p[..., topk * MOBA_BLOCK:]
        return (jnp.einsum("bhqks,bhqksd->bhqd", p_sel, v_sel)
                + jnp.einsum("bhqs,bhsd->bhqd", p_own, v_own))

    out = lax.map(attend, (q_chunks, jnp.arange(nq, dtype=jnp.int32)))
    out = out.transpose(1, 0, 3, 2, 4).reshape(b, sp, h, hd)
    return out[:, :s]


def setup_inputs(seed: int = 0) -> dict:
    key = jax.random.key(seed)
    ks = jax.random.split(key, 32)
    f32 = jnp.float32

    def nrm(k, shape, scale):
        return scale * jax.random.normal(k, shape, f32)

    x = nrm(ks[0], (BATCH, SEQ, D_MODEL), 1.0)
    norm_w = 1.0 + nrm(ks[1], (DEPTH, D_MODEL), 0.02)
    w_in = nrm(ks[2], (DEPTH, D_MODEL, IN_WIDTH), D_MODEL ** -0.5)
    gate_b = nrm(ks[3], (DEPTH, N_BRANCH * D_MODEL), 0.1)
    conv_w = nrm(ks[4], (DEPTH, SSD_CONV, SSD_XBC), SSD_CONV ** -0.5)
    conv_b = nrm(ks[5], (DEPTH, SSD_XBC), 0.02)
    dt0 = jnp.exp(jax.random.uniform(ks[6], (DEPTH, SSD_HEADS), f32, math.log(DT_MIN), math.log(DT_MAX)))
    dt_bias = dt0 + jnp.log(-jnp.expm1(-dt0))
    a_log = jnp.log(jax.random.uniform(ks[7], (DEPTH, SSD_HEADS), f32, 1.0, 16.0))
    ssd_d = 1.0 + nrm(ks[8], (DEPTH, SSD_HEADS), 0.02)
    ssd_norm_w = 1.0 + nrm(ks[9], (DEPTH, SSD_D_INNER), 0.02)
    w_proj_a = nrm(ks[10], (DEPTH, SSD_D_INNER, D_MODEL), SSD_D_INNER ** -0.5)
    lambda_re = -0.5 + nrm(ks[11], (DEPTH, S5_GROUPS, S5_STATE), 0.01)
    lambda_im = math.pi * jnp.arange(S5_STATE, dtype=f32) + nrm(ks[12], (DEPTH, S5_GROUPS, S5_STATE), 0.01)
    log_step = jax.random.uniform(ks[13], (DEPTH, S5_GROUPS), f32, math.log(DT_MIN), math.log(DT_MAX))
    s5_b_re = nrm(ks[14], (DEPTH, S5_GROUPS, S5_STATE, S5_GROUP_CH), (2 * S5_GROUP_CH) ** -0.5)
    s5_b_im = nrm(ks[15], (DEPTH, S5_GROUPS, S5_STATE, S5_GROUP_CH), (2 * S5_GROUP_CH) ** -0.5)
    s5_c_re = nrm(ks[16], (DEPTH, S5_GROUPS, S5_GROUP_CH, S5_STATE), (2 * S5_STATE) ** -0.5)
    s5_c_im = nrm(ks[17], (DEPTH, S5_GROUPS, S5_GROUP_CH, S5_STATE), (2 * S5_STATE) ** -0.5)
    s5_d = nrm(ks[18], (DEPTH, S5_WIDTH), 1.0)
    glu_w = nrm(ks[19], (DEPTH, S5_WIDTH, S5_WIDTH), S5_WIDTH ** -0.5)
    glu_b = nrm(ks[20], (DEPTH, S5_WIDTH), 0.02)
    w_proj_b = nrm(ks[21], (DEPTH, S5_WIDTH, D_MODEL), S5_WIDTH ** -0.5)
    w_proj_c = nrm(ks[22], (DEPTH, MOBA_WIDTH, D_MODEL), MOBA_WIDTH ** -0.5)
    w_out = nrm(ks[23], (DEPTH, D_MODEL, D_MODEL), D_MODEL ** -0.5)
    final_norm_w = 1.0 + nrm(ks[24], (D_MODEL,), 0.02)
    return {"x": x, "norm_w": norm_w, "w_in": w_in, "gate_b": gate_b,
            "conv_w": conv_w, "conv_b": conv_b, "dt_bias": dt_bias, "a_log": a_log,
            "ssd_d": ssd_d, "ssd_norm_w": ssd_norm_w, "w_proj_a": w_proj_a,
            "lambda_re": lambda_re, "lambda_im": lambda_im, "log_step": log_step,
            "s5_b_re": s5_b_re, "s5_b_im": s5_b_im, "s5_c_re": s5_c_re, "s5_c_im": s5_c_im,
            "s5_d": s5_d, "glu_w": glu_w, "glu_b": glu_b, "w_proj_b": w_proj_b,
            "w_proj_c": w_proj_c, "w_out": w_out, "final_norm_w": final_norm_w}


def reference(x, norm_w, w_in, gate_b, conv_w, conv_b, dt_bias, a_log, ssd_d, ssd_norm_w,
              w_proj_a, lambda_re, lambda_im, log_step, s5_b_re, s5_b_im, s5_c_re, s5_c_im,
              s5_d, glu_w, glu_b, w_proj_b, w_proj_c, w_out, final_norm_w):
    b, s, _ = x.shape
    for l in range(DEPTH):
        h = rmsnorm(x, norm_w[l])
        proj = h @ w_in[l]
        (z, xbc, dt_raw, s5_u, s5_gate, q, k, v, moba_gate, gate_logits) = jnp.split(
            proj, SPLIT_POINTS, axis=-1)
        y_a = ssd_branch(z, xbc, dt_raw, conv_w[l], conv_b[l], dt_bias[l], a_log[l],
                         ssd_d[l], ssd_norm_w[l]) @ w_proj_a[l]
        y_b = s5_branch(s5_u, s5_gate, lambda_re[l], lambda_im[l], log_step[l], s5_b_re[l],
                        s5_b_im[l], s5_c_re[l], s5_c_im[l], s5_d[l], glu_w[l], glu_b[l]) @ w_proj_b[l]
        att = moba_attention(q.reshape(b, s, MOBA_HEADS, MOBA_HEAD_DIM),
                             k.reshape(b, s, MOBA_HEADS, MOBA_HEAD_DIM),
                             v.reshape(b, s, MOBA_HEADS, MOBA_HEAD_DIM)).reshape(b, s, MOBA_WIDTH)
        y_c = (att * jax.nn.silu(moba_gate)) @ w_proj_c[l]
        gates = jax.nn.sigmoid(gate_logits + gate_b[l]).reshape(b, s, N_BRANCH, D_MODEL)
        merged = gates[:, :, 0] * y_a + gates[:, :, 1] * y_b + gates[:, :, 2] * y_c
        x = x + merged @ w_out[l]
    return rmsnorm(x, final_norm_w)
```

```python
import functools
import math

import jax
import jax.numpy as jnp
from jax import lax
from jax.experimental import pallas as pl
from jax.experimental.pallas import tpu as pltpu

F32 = jnp.float32
BF16 = jnp.bfloat16
HI = lax.Precision.HIGHEST

D = 1024
EPS = 1e-6
BLK = 256
SSD_HEADS = 16
SSD_HEAD_DIM = 64
SSD_GROUPS = 4
SSD_STATE = 128
SSD_CONV = 4
S5_GROUPS = 64
S5_CH = 16
S5_STATE = 64
S5_CLIP = 1e-4
S5_L = 16
S5_PAIRS = S5_GROUPS // 2
MOBA_HEADS = 16
MOBA_HD = 64
MOBA_TOPK = 3
NEG = -1e30
LANE = 128
VMEM_LIMIT = 56 * 1024 * 1024

C_Z, C_XS, C_BC, C_S5U, C_S5G, C_Q, C_K, C_MG, C_GL = 0, 1, 2, 3, 4, 5, 6, 7, 8
N_NAT = 11 * D

NT = (((1,), (1,)), ((), ()))
TN = (((0,), (0,)), ((), ()))


def _cparams(n_axes):
    return pltpu.CompilerParams(dimension_semantics=("arbitrary",) * n_axes,
                                vmem_limit_bytes=VMEM_LIMIT)


def _const_spec(shape):
    nd = len(shape)
    return pl.BlockSpec(shape, lambda *_: (0,) * nd)


def _proj_kernel(x_ref, nw_ref, w_ref, wvt_ref, wdt_ref, wdtt_ref,
                 proj_ref, vt_ref, dt_ref, dtt_ref, h_ref):
    @pl.when(pl.program_id(1) == 0)
    def _():
        x = x_ref[...]
        h = x * lax.rsqrt(jnp.mean(x * x, axis=-1, keepdims=True) + EPS) * nw_ref[...]
        hb = h.astype(BF16)
        h_ref[...] = hb
        vt = lax.dot_general(wvt_ref[...], hb, NT, preferred_element_type=F32)
        for c in range(vt_ref.shape[0]):
            vt_ref[c] = vt[:, c * BLK:(c + 1) * BLK].astype(BF16)
        dt_ref[...] = jnp.dot(h, wdt_ref[...], precision=HI, preferred_element_type=F32)
        dtt_ref[...] = lax.dot_general(wdtt_ref[...], h, NT, precision=HI,
                                       preferred_element_type=F32)

    proj_ref[...] = jnp.dot(h_ref[...], w_ref[...], preferred_element_type=F32)


def _project(x, norm_w, w_nat, w_vt, w_dt, w_dtt, tm):
    s = x.shape[0]
    nb = s // BLK
    grid = (s // tm, N_NAT // D)
    return pl.pallas_call(
        _proj_kernel,
        grid=grid,
        in_specs=[
            pl.BlockSpec((tm, D), lambda i, j: (i, 0)),
            _const_spec((1, D)),
            pl.BlockSpec((D, D), lambda i, j: (0, j)),
            _const_spec((D, D)),
            _const_spec((D, LANE)),
            _const_spec((SSD_HEADS, D)),
        ],
        out_specs=[
            pl.BlockSpec((tm, D), lambda i, j: (i, j)),
            pl.BlockSpec((tm // BLK, D, BLK), lambda i, j: (i, 0, 0)),
            pl.BlockSpec((tm, LANE), lambda i, j: (i, 0)),
            pl.BlockSpec((SSD_HEADS, tm), lambda i, j: (0, i)),
        ],
        out_shape=[
            jax.ShapeDtypeStruct((s, N_NAT), F32),
            jax.ShapeDtypeStruct((nb, D, BLK), BF16),
            jax.ShapeDtypeStruct((s, LANE), F32),
            jax.ShapeDtypeStruct((SSD_HEADS, s), F32),
        ],
        scratch_shapes=[pltpu.VMEM((tm, D), BF16)],
        compiler_params=_cparams(2),
        name="proj",
    )(x, norm_w, w_nat, w_vt, w_dt, w_dtt)


def _softplus(v):
    return jnp.maximum(v, 0.0) + jnp.log1p(jnp.exp(-jnp.abs(v)))


def _ssd_kernel(z_ref, xs_ref, bc_ref, dt_ref, dtt_ref, cw_ref, cb_ref, dtb_ref, dtbt_ref,
                al_ref, alt_ref, dsk_ref, nw_ref, e_ref, out_ref, ext_ref, st_ref):
    @pl.when(pl.program_id(0) == 0)
    def _():
        ext_ref[0:8, :] = jnp.zeros((8, 2 * D), F32)
        st_ref[...] = jnp.zeros(st_ref.shape, F32)

    ext_ref[8:8 + BLK, 0:D] = xs_ref[...]
    ext_ref[8:8 + BLK, D:2 * D] = bc_ref[...]
    acc = cb_ref[...] + cw_ref[0:1, :] * ext_ref[5:5 + BLK, :]
    for i in range(1, SSD_CONV):
        acc = acc + cw_ref[i:i + 1, :] * ext_ref[5 + i:5 + i + BLK, :]
    ext_ref[0:8, :] = ext_ref[BLK:BLK + 8, :]
    act = acc * jax.nn.sigmoid(acc)
    xs = act[:, 0:D]

    dt = _softplus(dt_ref[...] + dtb_ref[...])
    da = dt * (-jnp.exp(al_ref[...]))
    dtt = _softplus(dtt_ref[...] + dtbt_ref[...])
    dat = dtt * (-jnp.exp(alt_ref[...]))
    row = lax.broadcasted_iota(jnp.int32, (BLK, BLK), 0)
    col = lax.broadcasted_iota(jnp.int32, (BLK, BLK), 1)
    lower = row >= col
    tri = lower.astype(F32)
    a_cum = jnp.dot(tri, da, precision=HI, preferred_element_type=F32)
    a_cumt = jnp.dot(dat, (row <= col).astype(F32), precision=HI,
                     preferred_element_type=F32)
    xdt = xs * jnp.dot(dt, e_ref[...], precision=HI, preferred_element_type=F32)
    xdt_b = xdt.astype(BF16)

    lane = lax.broadcasted_iota(jnp.int32, (BLK, LANE), 1)
    first = lane < SSD_HEAD_DIM
    rowp = lax.broadcasted_iota(jnp.int32, (LANE, 1), 0) < SSD_HEAD_DIM
    y_parts = []
    for g in range(SSD_GROUPS):
        b_g = act[:, D + g * SSD_STATE:D + (g + 1) * SSD_STATE]
        c_g = act[:, D + SSD_GROUPS * SSD_STATE + g * SSD_STATE:
                  D + SSD_GROUPS * SSD_STATE + (g + 1) * SSD_STATE]
        b_gb = b_g.astype(BF16)
        c_gb = c_g.astype(BF16)
        cbm = lax.dot_general(c_gb, b_gb, NT, preferred_element_type=F32)
        for pp in range(2):
            pair = 2 * g + pp
            h0 = 2 * pair
            x_pair = xdt_b[:, pair * LANE:(pair + 1) * LANE]
            y_pair = jnp.zeros((BLK, LANE), F32)
            for hh in range(2):
                h = h0 + hh
                lm = jnp.where(lower, jnp.exp(a_cum[:, h:h + 1] - a_cumt[h:h + 1, :]), 0.0)
                w = (cbm * lm).astype(BF16)
                xm = jnp.where(first if hh == 0 else jnp.logical_not(first), x_pair,
                               jnp.zeros_like(x_pair))
                y_pair = y_pair + jnp.dot(w, xm, preferred_element_type=F32)
            ac_pair = jnp.where(first, a_cum[:, h0:h0 + 1], a_cum[:, h0 + 1:h0 + 2])
            al_pair = ac_pair[BLK - 1:BLK, :]
            r_pair = st_ref[pair]
            y_off = lax.dot_general(c_gb, r_pair.astype(BF16), NT, preferred_element_type=F32)
            y_pair = y_pair + y_off * jnp.exp(ac_pair)
            xdec = (xdt[:, pair * LANE:(pair + 1) * LANE] * jnp.exp(al_pair - ac_pair)).astype(BF16)
            st_new = lax.dot_general(xdec, b_gb, TN, preferred_element_type=F32)
            al_col = jnp.where(rowp, a_cumt[h0:h0 + 1, BLK - 1:BLK],
                               a_cumt[h0 + 1:h0 + 2, BLK - 1:BLK])
            st_ref[pair] = jnp.exp(al_col) * r_pair + st_new
            y_parts.append(y_pair)
    y = jnp.concatenate(y_parts, axis=1) + dsk_ref[...] * xs
    z = z_ref[...]
    y = y * (z * jax.nn.sigmoid(z))
    gw = D // SSD_GROUPS
    for g in range(SSD_GROUPS):
        yg = y[:, g * gw:(g + 1) * gw]
        yg = yg * lax.rsqrt(jnp.mean(yg * yg, axis=-1, keepdims=True) + EPS)
        out_ref[:, g * gw:(g + 1) * gw] = yg * nw_ref[:, g * gw:(g + 1) * gw]


def _ssd(proj, dt, dtt, cw, cb, dtb, dtbt, al, alt, dsk, nw, expand):
    s = proj.shape[0]
    row = lambda c: pl.BlockSpec((BLK, D), lambda i, c=c: (i, c))
    return pl.pallas_call(
        _ssd_kernel,
        grid=(s // BLK,),
        in_specs=[
            row(C_Z), row(C_XS), row(C_BC),
            pl.BlockSpec((BLK, LANE), lambda i: (i, 0)),
            pl.BlockSpec((SSD_HEADS, BLK), lambda i: (0, i)),
            _const_spec((SSD_CONV, 2 * D)), _const_spec((1, 2 * D)),
            _const_spec((1, LANE)), _const_spec((SSD_HEADS, 1)),
            _const_spec((1, LANE)), _const_spec((SSD_HEADS, 1)),
            _const_spec((1, D)), _const_spec((1, D)), _const_spec((LANE, D)),
        ],
        out_specs=pl.BlockSpec((BLK, D), lambda i: (i, 0)),
        out_shape=jax.ShapeDtypeStruct((s, D), F32),
        scratch_shapes=[pltpu.VMEM((BLK + 8, 2 * D), F32),
                        pltpu.VMEM((SSD_HEADS // 2, LANE, SSD_STATE), F32)],
        compiler_params=_cparams(1),
        name="ssd",
    )(proj, proj, proj, dt, dtt, cw, cb, dtb, dtbt, al, alt, dsk, nw, expand)


def _s5_prep_kernel(lrr_ref, lir_ref, lsr_ref, lrc_ref, lic_ref, lsc_ref,
                    bre_ref, bim_ref, cre_ref, cim_ref,
                    m_ref, bxr_ref, bxi_ref, cxr_ref, cxi_ref, alr_ref, ali_ref):
    rows = 2 * S5_L * S5_CH
    cols = 2 * S5_L * S5_CH
    lr = jnp.minimum(lrr_ref[0], -S5_CLIP)
    li = lir_ref[0]
    st = jnp.exp(lsr_ref[0])
    lrs, lis = lr * st, li * st
    mag = jnp.exp(lrs)
    abr, abi = mag * jnp.cos(lis), mag * jnp.sin(lis)
    nr, ni = abr - 1.0, abi
    den = lr * lr + li * li
    cfr = (nr * lr + ni * li) / den
    cfi = (ni * lr - nr * li) / den
    bre, bim = bre_ref[0], bim_ref[0]
    bbr = cfr * bre - cfi * bim
    bbi = cfr * bim + cfi * bre
    rk = lax.broadcasted_iota(jnp.int32, (rows, 1), 0)
    ek = (S5_L - 1 - (rk % (S5_L * S5_CH)) // S5_CH).astype(F32)
    pm = jnp.exp(ek * lrs)
    pbr, pbi = pm * jnp.cos(ek * lis), pm * jnp.sin(ek * lis)
    bxr = pbr * bbr - pbi * bbi
    bxi = pbr * bbi + pbi * bbr
    bxr_ref[0] = bxr.astype(BF16)
    bxi_ref[0] = bxi.astype(BF16)
    alm = jnp.exp(float(S5_L) * lrs)
    alr_ref[0] = alm * jnp.cos(float(S5_L) * lis)
    ali_ref[0] = alm * jnp.sin(float(S5_L) * lis)
    lrc = jnp.minimum(lrc_ref[0], -S5_CLIP)
    stc = jnp.exp(lsc_ref[0])
    lrsc, lisc = lrc * stc, lic_ref[0] * stc
    ct = lax.broadcasted_iota(jnp.int32, (1, cols), 1)
    et = ((ct % (S5_L * S5_CH)) // S5_CH + 1).astype(F32)
    qm = jnp.exp(et * lrsc)
    qr, qi = qm * jnp.cos(et * lisc), qm * jnp.sin(et * lisc)
    cre, cim = cre_ref[0], cim_ref[0]
    cxr_ref[0] = (cre * qr - cim * qi).astype(BF16)
    cxi_ref[0] = (-(cre * qi + cim * qr)).astype(BF16)
    rw = (jnp.dot(bxr, cre, precision=HI, preferred_element_type=F32)
          - jnp.dot(bxi, cim, precision=HI, preferred_element_type=F32))
    n = S5_L * S5_CH
    tcol = lax.broadcasted_iota(jnp.int32, (n, n), 1) // S5_CH
    for gi in range(2):
        rg = rw[gi * n:(gi + 1) * n, gi * n:(gi + 1) * n]
        m = jnp.zeros((n, n), F32)
        for t in range(S5_L):
            sh = (S5_L - 1 - t) * S5_CH
            if sh:
                shifted = jnp.concatenate([rg[sh:, :], jnp.zeros((sh, n), F32)], axis=0)
            else:
                shifted = rg
            m = jnp.where(tcol == t, shifted, m)
        m_ref[gi] = m.astype(BF16)


def _s5_prep(lrr, lir, lsr, lrc, lic, lsc, bre2, bim2, cre2, cim2):
    n = S5_L * S5_CH
    pair3 = lambda a, b: pl.BlockSpec((1, a, b), lambda i: (i, 0, 0))
    return pl.pallas_call(
        _s5_prep_kernel,
        grid=(S5_PAIRS,),
        in_specs=[pair3(1, LANE)] * 3 + [pair3(LANE, 1)] * 3
                 + [pair3(2 * n, LANE)] * 2 + [pair3(LANE, 2 * n)] * 2,
        out_specs=[pl.BlockSpec((2, n, n), lambda i: (i, 0, 0)),
                   pair3(2 * n, LANE), pair3(2 * n, LANE), pair3(LANE, 2 * n), pair3(LANE, 2 * n),
                   pair3(1, LANE), pair3(1, LANE)],
        out_shape=[jax.ShapeDtypeStruct((S5_GROUPS, n, n), BF16),
                   jax.ShapeDtypeStruct((S5_PAIRS, 2 * n, LANE), BF16),
                   jax.ShapeDtypeStruct((S5_PAIRS, 2 * n, LANE), BF16),
                   jax.ShapeDtypeStruct((S5_PAIRS, LANE, 2 * n), BF16),
                   jax.ShapeDtypeStruct((S5_PAIRS, LANE, 2 * n), BF16),
                   jax.ShapeDtypeStruct((S5_PAIRS, 1, LANE), F32),
                   jax.ShapeDtypeStruct((S5_PAIRS, 1, LANE), F32)],
        compiler_params=_cparams(1),
        name="s5_prep",
    )(lrr, lir, lsr, lrc, lic, lsc, bre2, bim2, cre2, cim2)


def _s5_local_kernel(u_ref, bxr_ref, bxi_ref, sr_ref, si_ref):
    up = jnp.concatenate([u_ref[0], u_ref[1]], axis=1).astype(BF16)
    sr_ref[...] = jnp.dot(up, bxr_ref[0], preferred_element_type=F32)
    si_ref[...] = jnp.dot(up, bxi_ref[0], preferred_element_type=F32)


def _s5_local(u3, bxr, bxi):
    nc = u3.shape[1]
    n = S5_L * S5_CH
    return pl.pallas_call(
        _s5_local_kernel,
        grid=(S5_PAIRS,),
        in_specs=[pl.BlockSpec((2, nc, n), lambda i: (i, 0, 0)),
                  pl.BlockSpec((1, 2 * n, LANE), lambda i: (i, 0, 0)),
                  pl.BlockSpec((1, 2 * n, LANE), lambda i: (i, 0, 0))],
        out_specs=[pl.BlockSpec((nc, LANE), lambda i: (0, i))] * 2,
        out_shape=[jax.ShapeDtypeStruct((nc, S5_PAIRS * LANE), F32)] * 2,
        compiler_params=_cparams(1),
        name="s5_local",
    )(u3, bxr, bxi)


def _s5_scan_kernel(xr_ref, xi_ref, ar_ref, ai_ref, pr_ref, pi_ref):
    ar, ai = ar_ref[...], ai_ref[...]
    w = ar.shape[1]

    def body(r, carry):
        sr, si = carry
        pr_ref[pl.ds(r, 1), :] = sr
        pi_ref[pl.ds(r, 1), :] = si
        xr = xr_ref[pl.ds(r, 1), :]
        xi = xi_ref[pl.ds(r, 1), :]
        return ar * sr - ai * si + xr, ar * si + ai * sr + xi

    lax.fori_loop(0, xr_ref.shape[0], body, (jnp.zeros((1, w), F32), jnp.zeros((1, w), F32)))


def _s5_scan(xr, xi, ar, ai, tw):
    nc, width = xr.shape
    col = pl.BlockSpec((nc, tw), lambda i: (0, i))
    vec = pl.BlockSpec((1, tw), lambda i: (0, i))
    return pl.pallas_call(
        _s5_scan_kernel,
        grid=(width // tw,),
        in_specs=[col, col, vec, vec],
        out_specs=[col, col],
        out_shape=[jax.ShapeDtypeStruct((nc, width), F32)] * 2,
        compiler_params=_cparams(1),
        name="s5_scan",
    )(xr, xi, ar, ai)


def _s5_out_kernel(u_ref, m_ref, pr_ref, pi_ref, cxr_ref, cxi_ref, y_ref):
    n = S5_L * S5_CH
    inter = (jnp.dot(pr_ref[...].astype(BF16), cxr_ref[0], preferred_element_type=F32)
             + jnp.dot(pi_ref[...].astype(BF16), cxi_ref[0], preferred_element_type=F32))
    for gi in range(2):
        intra = jnp.dot(u_ref[gi].astype(BF16), m_ref[gi], preferred_element_type=F32)
        y_ref[gi] = intra + inter[:, gi * n:(gi + 1) * n]


def _s5_out(u3, m, pr, pi, cxr, cxi):
    nc = u3.shape[1]
    n = S5_L * S5_CH
    return pl.pallas_call(
        _s5_out_kernel,
        grid=(S5_PAIRS,),
        in_specs=[pl.BlockSpec((2, nc, n), lambda i: (i, 0, 0)),
                  pl.BlockSpec((2, n, n), lambda i: (i, 0, 0)),
                  pl.BlockSpec((nc, LANE), lambda i: (0, i)),
                  pl.BlockSpec((nc, LANE), lambda i: (0, i)),
                  pl.BlockSpec((1, LANE, 2 * n), lambda i: (i, 0, 0)),
                  pl.BlockSpec((1, LANE, 2 * n), lambda i: (i, 0, 0))],
        out_specs=pl.BlockSpec((2, nc, n), lambda i: (i, 0, 0)),
        out_shape=jax.ShapeDtypeStruct((S5_GROUPS, nc, n), F32),
        compiler_params=_cparams(1),
        name="s5_out",
    )(u3, m, pr, pi, cxr, cxi)


def _kmean_kernel(k_ref, o_ref):
    for r in range(o_ref.shape[0]):
        o_ref[r:r + 1, :] = jnp.mean(k_ref[r * BLK:(r + 1) * BLK, :], axis=0, keepdims=True)


def _kmean(proj, rows):
    s = proj.shape[0]
    nb = s // BLK
    return pl.pallas_call(
        _kmean_kernel,
        grid=(nb // rows,),
        in_specs=[pl.BlockSpec((rows * BLK, D), lambda i: (i, C_K))],
        out_specs=pl.BlockSpec((rows, D), lambda i: (i, 0)),
        out_shape=jax.ShapeDtypeStruct((nb, D), F32),
        compiler_params=_cparams(1),
        name="kmean",
    )(proj)


def _moba_kernel(q_ref, k_ref, vt_ref, km_ref, o_ref, bias_ref):
    i = pl.program_id(1)
    nb = km_ref.shape[0]
    q = q_ref[...]
    lane = lax.broadcasted_iota(jnp.int32, (BLK, LANE), 1)
    blk = lax.broadcasted_iota(jnp.int32, (nb, BLK), 0)
    blk_f = blk.astype(F32)
    scale = MOBA_HD ** -0.5 * math.log2(math.e)
    qs = []
    for h in range(2):
        in_head = (lane >= h * MOBA_HD) & (lane < (h + 1) * MOBA_HD)
        qh = jnp.where(in_head, q, 0.0)
        g = lax.dot_general(km_ref[...], qh, NT, precision=HI, preferred_element_type=F32)
        g = jnp.where(blk < i, g, -jnp.inf)
        sel = jnp.zeros((nb, BLK), jnp.bool_)
        for _ in range(MOBA_TOPK):
            mx = jnp.max(g, axis=0, keepdims=True)
            idx = jnp.min(jnp.where(g == mx, blk_f, float(nb)), axis=0, keepdims=True)
            hit = blk_f == idx
            sel = sel | (hit & (mx > -jnp.inf))
            g = jnp.where(hit, -jnp.inf, g)
        bias_ref[h] = jnp.where(sel, 0.0, NEG)
        qs.append((qh * scale).astype(BF16))

    def attend(kb, vt, h, s_mask, carry):
        m, l, acc = carry
        s_t = lax.dot_general(kb, qs[h], NT, preferred_element_type=F32)
        s_t = s_mask(s_t)
        mn = jnp.maximum(m, jnp.max(s_t, axis=0, keepdims=True))
        alpha = jnp.exp2(m - mn)
        p = jnp.exp2(s_t - mn)
        l = alpha * l + jnp.sum(p, axis=0, keepdims=True)
        acc = alpha * acc + jnp.dot(vt[h * MOBA_HD:(h + 1) * MOBA_HD, :], p.astype(BF16),
                                    preferred_element_type=F32)
        return mn, l, acc

    init = (jnp.full((1, BLK), NEG, F32), jnp.zeros((1, BLK), F32), jnp.zeros((MOBA_HD, BLK), F32))
    krow = lax.broadcasted_iota(jnp.int32, (BLK, BLK), 0)
    qcol = lax.broadcasted_iota(jnp.int32, (BLK, BLK), 1)
    causal = krow <= qcol
    kb_own = k_ref[pl.ds(pl.multiple_of(i * BLK, BLK), BLK), :].astype(BF16)
    vt_own = vt_ref[i]
    own_mask = lambda s_t: jnp.where(causal, s_t, NEG)
    carry = tuple(attend(kb_own, vt_own, h, own_mask, init) for h in range(2))

    def body(j, carry):
        kb = k_ref[pl.ds(pl.multiple_of(j * BLK, BLK), BLK), :].astype(BF16)
        vt = vt_ref[j]
        out = []
        for h in range(2):
            bias = bias_ref[h, pl.ds(j, 1), :]
            out.append(attend(kb, vt, h, lambda s_t, bias=bias: s_t + bias, carry[h]))
        return tuple(out)

    carry = lax.fori_loop(0, i, body, carry)
    for h in range(2):
        _, l, acc = carry[h]
        o_ref[h * MOBA_HD:(h + 1) * MOBA_HD, :] = acc / l


def _moba(proj, vt3, kmean):
    s = proj.shape[0]
    nb = s // BLK
    per = D // LANE
    return pl.pallas_call(
        _moba_kernel,
        grid=(MOBA_HEADS // 2, nb),
        in_specs=[pl.BlockSpec((BLK, LANE), lambda hp, i: (i, C_Q * per + hp)),
                  pl.BlockSpec((s, LANE), lambda hp, i: (0, C_K * per + hp)),
                  pl.BlockSpec((nb, LANE, BLK), lambda hp, i: (0, hp, 0)),
                  pl.BlockSpec((nb, LANE), lambda hp, i: (0, hp))],
        out_specs=pl.BlockSpec((LANE, BLK), lambda hp, i: (hp, i)),
        out_shape=jax.ShapeDtypeStruct((D, s), F32),
        scratch_shapes=[pltpu.VMEM((2, nb, BLK), F32)],
        compiler_params=_cparams(2),
        name="moba",
    )(proj, proj, vt3, kmean)


def _merge_kernel(final, x_ref, ya_ref, ys_ref, u_ref, sg_ref, att_ref, mg_ref,
                  g0_ref, g1_ref, g2_ref, gb_ref, s5d_ref, glub_ref, fnw_ref,
                  gluw_ref, wa_ref, wb_ref, wc_ref, wo_ref, out_ref):
    mm = lambda a, w_ref: jnp.dot(a.astype(BF16), w_ref[...], preferred_element_type=F32)
    yb = ys_ref[...] + s5d_ref[...] * u_ref[...]
    yb = jax.nn.gelu(yb)
    yb = yb * jax.nn.sigmoid(mm(yb, gluw_ref) + glub_ref[...])
    sg = sg_ref[...]
    yb = yb * (sg * jax.nn.sigmoid(sg))
    mg = mg_ref[...]
    att = att_ref[...].T * (mg * jax.nn.sigmoid(mg))
    merged = (jax.nn.sigmoid(g0_ref[...] + gb_ref[:, 0:D]) * mm(ya_ref[...], wa_ref)
              + jax.nn.sigmoid(g1_ref[...] + gb_ref[:, D:2 * D]) * mm(yb, wb_ref)
              + jax.nn.sigmoid(g2_ref[...] + gb_ref[:, 2 * D:3 * D]) * mm(att, wc_ref))
    xn = x_ref[...] + mm(merged, wo_ref)
    if final:
        xn = xn * lax.rsqrt(jnp.mean(xn * xn, axis=-1, keepdims=True) + EPS) * fnw_ref[...]
    out_ref[...] = xn


def _merge(final, x, ya, ys, proj, att_t, gate_b, s5d, glub, fnw, gluw, wa, wb, wc, wo, tm):
    s = x.shape[0]
    rowb = pl.BlockSpec((tm, D), lambda i: (i, 0))
    pcol = lambda c: pl.BlockSpec((tm, D), lambda i, c=c: (i, c))
    wspec = pl.BlockSpec((D, D), lambda i: (0, 0), pipeline_mode=pl.Buffered(1))
    return pl.pallas_call(
        functools.partial(_merge_kernel, final),
        grid=(s // tm,),
        in_specs=[rowb, rowb, rowb, pcol(C_S5U), pcol(C_S5G),
                  pl.BlockSpec((D, tm), lambda i: (0, i)), pcol(C_MG),
                  pcol(C_GL), pcol(C_GL + 1), pcol(C_GL + 2),
                  _const_spec((1, 3 * D)), _const_spec((1, D)), _const_spec((1, D)),
                  _const_spec((1, D)), wspec, wspec, wspec, wspec, wspec],
        out_specs=rowb,
        out_shape=jax.ShapeDtypeStruct((s, D), F32),
        compiler_params=_cparams(1),
        name="merge",
    )(x, ya, ys, proj, proj, att_t, proj, proj, proj, proj,
      gate_b, s5d, glub, fnw, gluw, wa, wb, wc, wo)


def _block_diag_pairs(a):
    g, r, c = a.shape
    a2 = a.reshape(g // 2, 2, r, c)
    out = jnp.einsum("pgrc,gh->pgrhc", a2, jnp.eye(2, dtype=a.dtype))
    return out.reshape(g // 2, 2 * r, 2 * c)


def _row_tile(s, want):
    t = min(want, s)
    while s % t:
        t //= 2
    return t


def kernel(x, norm_w, w_in, gate_b, conv_w, conv_b, dt_bias, a_log, ssd_d, ssd_norm_w, w_proj_a,
           lambda_re, lambda_im, log_step, s5_b_re, s5_b_im, s5_c_re, s5_c_im, s5_d, glu_w, glu_b,
           w_proj_b, w_proj_c, w_out, final_norm_w):
    b, s, _ = x.shape
    assert b == 1 and s % BLK == 0 and x.shape[2] == D
    depth = norm_w.shape[0]
    nb = s // BLK
    nc = s // S5_L
    xc = x.reshape(s, D)
    o_xbc, o_dt = D, 3 * D
    o_s5u = o_dt + SSD_HEADS
    o_q = o_s5u + 2 * D
    o_v = o_q + 2 * D
    o_mg = o_v + D
    expand = (jnp.arange(LANE)[:, None] == (jnp.arange(D)[None, :] // SSD_HEAD_DIM)).astype(F32)
    pad16 = lambda a: jnp.pad(a.astype(F32), (0, LANE - SSD_HEADS)).reshape(1, LANE)
    tile_b = lambda a: jnp.tile(a.transpose(0, 2, 1), (1, S5_L, 1))
    tile_c = lambda a: jnp.tile(a.transpose(0, 2, 1), (1, 1, S5_L))

    for l in range(depth):
        w = w_in[l]
        w_nat = jnp.concatenate([w[:, 0:o_dt], w[:, o_s5u:o_v], w[:, o_mg:]], axis=1).astype(BF16)
        w_vt = w[:, o_v:o_mg].T.astype(BF16)
        w_dt = jnp.pad(w[:, o_dt:o_s5u], ((0, 0), (0, LANE - SSD_HEADS)))
        w_dtt = w[:, o_dt:o_s5u].T
        proj, vt3, dt, dtt = _project(xc, norm_w[l].reshape(1, D), w_nat, w_vt, w_dt, w_dtt,
                                      _row_tile(s, 1024))

        ya = _ssd(proj, dt, dtt, conv_w[l], conv_b[l].reshape(1, 2 * D),
                  pad16(dt_bias[l]), dt_bias[l].reshape(SSD_HEADS, 1),
                  pad16(a_log[l]), a_log[l].reshape(SSD_HEADS, 1),
                  jnp.repeat(ssd_d[l], SSD_HEAD_DIM).reshape(1, D),
                  ssd_norm_w[l].reshape(1, D), expand)

        ls_full = jnp.repeat(log_step[l], S5_STATE)
        m, bxr, bxi, cxr, cxi, alr, ali = _s5_prep(
            lambda_re[l].reshape(S5_PAIRS, 1, LANE), lambda_im[l].reshape(S5_PAIRS, 1, LANE),
            ls_full.reshape(S5_PAIRS, 1, LANE),
            lambda_re[l].reshape(S5_PAIRS, LANE, 1), lambda_im[l].reshape(S5_PAIRS, LANE, 1),
            ls_full.reshape(S5_PAIRS, LANE, 1),
            _block_diag_pairs(tile_b(s5_b_re[l])), _block_diag_pairs(tile_b(s5_b_im[l])),
            _block_diag_pairs(tile_c(s5_c_re[l])), _block_diag_pairs(tile_c(s5_c_im[l])))
        u3 = (proj[:, C_S5U * D:(C_S5U + 1) * D].reshape(nc, S5_L, S5_GROUPS, S5_CH)
              .transpose(2, 0, 1, 3).reshape(S5_GROUPS, nc, S5_L * S5_CH))
        sr, si = _s5_local(u3, bxr, bxi)
        pr, pi = _s5_scan(sr, si, alr.reshape(1, -1), ali.reshape(1, -1), 512)
        y3 = _s5_out(u3, m, pr, pi, cxr, cxi)
        ys = (y3.reshape(S5_GROUPS, nc, S5_L, S5_CH).transpose(1, 2, 0, 3).reshape(s, D))

        km = _kmean(proj, math.gcd(nb, 8))
        att_t = _moba(proj, vt3, km)

        xc = _merge(l == depth - 1, xc, ya, ys, proj, att_t, gate_b[l].reshape(1, 3 * D),
                    s5_d[l].reshape(1, D), glu_b[l].reshape(1, D), final_norm_w.reshape(1, D),
                    glu_w[l].astype(BF16), w_proj_a[l].astype(BF16), w_proj_b[l].astype(BF16),
                    w_proj_c[l].astype(BF16), w_out[l].astype(BF16), _row_tile(s, 256))
    return xc.reshape(b, s, D)
```

```python
import functools
import math

import jax
import jax.numpy as jnp
from jax import lax
from jax.experimental import pallas as pl
from jax.experimental.pallas import tpu as pltpu

F32 = jnp.float32
BF16 = jnp.bfloat16
HI = lax.Precision.HIGHEST

D = 1024
EPS = 1e-6
BLK = 256
SSD_HEADS = 16
SSD_HEAD_DIM = 64
SSD_GROUPS = 4
SSD_STATE = 128
SSD_CONV = 4
S5_GROUPS = 64
S5_CH = 16
S5_STATE = 64
S5_CLIP = 1e-4
S5_L = 16
S5_PAIRS = S5_GROUPS // 2
MOBA_HEADS = 16
MOBA_HD = 64
MOBA_TOPK = 3
MOBA_KG = 4
NEG = -1e30
LANE = 128
SUBLANE = 8
VMEM_LIMIT = 56 * 1024 * 1024

C_Z, C_XS, C_BC, C_S5U, C_S5G, C_Q, C_K, C_MG, C_GL = 0, 1, 2, 3, 4, 5, 6, 7, 8
N_NAT = 11 * D

NT = (((1,), (1,)), ((), ()))
TN = (((0,), (0,)), ((), ()))


def _cparams(n_axes):
    return pltpu.CompilerParams(dimension_semantics=("arbitrary",) * n_axes,
                                vmem_limit_bytes=VMEM_LIMIT)


def _const_spec(shape):
    nd = len(shape)
    return pl.BlockSpec(shape, lambda *_: (0,) * nd)


def _proj_kernel(x_ref, nw_ref, w_ref, wvt_ref, wdt_ref, wdtt_ref,
                 proj_ref, vt_ref, dt_ref, dtt_ref, h_ref):
    @pl.when(pl.program_id(1) == 0)
    def _():
        x = x_ref[...]
        h = x * lax.rsqrt(jnp.mean(x * x, axis=-1, keepdims=True) + EPS) * nw_ref[...]
        hb = h.astype(BF16)
        h_ref[...] = hb
        vt = lax.dot_general(wvt_ref[...], hb, NT, preferred_element_type=F32)
        vt_ref[...] = vt.astype(BF16)
        dt_ref[...] = jnp.dot(h, wdt_ref[...], precision=HI, preferred_element_type=F32)
        dtt_ref[...] = lax.dot_general(wdtt_ref[...], h, NT, precision=HI,
                                       preferred_element_type=F32)

    proj_ref[...] = jnp.dot(h_ref[...], w_ref[...], preferred_element_type=F32).astype(BF16)


def _project(x, norm_w, w_nat, w_vt, w_dt, w_dtt, tm):
    s = x.shape[0]
    nb = s // BLK
    grid = (s // tm, N_NAT // D)
    return pl.pallas_call(
        _proj_kernel,
        grid=grid,
        in_specs=[
            pl.BlockSpec((tm, D), lambda i, j: (i, 0)),
            _const_spec((1, D)),
            pl.BlockSpec((D, D), lambda i, j: (0, j)),
            _const_spec((D, D)),
            _const_spec((D, LANE)),
            _const_spec((SSD_HEADS, D)),
        ],
        out_specs=[
            pl.BlockSpec((tm, D), lambda i, j: (i, j)),
            pl.BlockSpec((D, tm), lambda i, j: (0, i)),
            pl.BlockSpec((tm, LANE), lambda i, j: (i, 0)),
            pl.BlockSpec((SSD_HEADS, tm), lambda i, j: (0, i)),
        ],
        out_shape=[
            jax.ShapeDtypeStruct((s, N_NAT), BF16),
            jax.ShapeDtypeStruct((D, s), BF16),
            jax.ShapeDtypeStruct((s, LANE), F32),
            jax.ShapeDtypeStruct((SSD_HEADS, s), F32),
        ],
        scratch_shapes=[pltpu.VMEM((tm, D), BF16)],
        compiler_params=_cparams(2),
        name="proj",
    )(x, norm_w, w_nat, w_vt, w_dt, w_dtt)


def _softplus(v):
    return jnp.maximum(v, 0.0) + jnp.log1p(jnp.exp(-jnp.abs(v)))


def _ssd_kernel(z_ref, xs_ref, bc_ref, dt_ref, dtt_ref, cw_ref, cb_ref, dtb_ref, dtbt_ref,
                al_ref, alt_ref, dsk_ref, nw_ref, e_ref, out_ref, ext_ref, st_ref):
    @pl.when(pl.program_id(0) == 0)
    def _():
        ext_ref[0:8, :] = jnp.zeros((8, 2 * D), F32)
        st_ref[...] = jnp.zeros(st_ref.shape, F32)

    ext_ref[8:8 + BLK, 0:D] = xs_ref[...].astype(F32)
    ext_ref[8:8 + BLK, D:2 * D] = bc_ref[...].astype(F32)
    acc = cb_ref[...] + cw_ref[0:1, :] * ext_ref[5:5 + BLK, :]
    for i in range(1, SSD_CONV):
        acc = acc + cw_ref[i:i + 1, :] * ext_ref[5 + i:5 + i + BLK, :]
    ext_ref[0:8, :] = ext_ref[BLK:BLK + 8, :]
    act = acc * jax.nn.sigmoid(acc)
    xs = act[:, 0:D]

    dt = _softplus(dt_ref[...] + dtb_ref[...])
    da = dt * (-jnp.exp(al_ref[...]))
    dtt = _softplus(dtt_ref[...] + dtbt_ref[...])
    dat = dtt * (-jnp.exp(alt_ref[...]))
    row = lax.broadcasted_iota(jnp.int32, (BLK, BLK), 0)
    col = lax.broadcasted_iota(jnp.int32, (BLK, BLK), 1)
    lower = row >= col
    tri = lower.astype(F32)
    a_cum = jnp.dot(tri, da, precision=HI, preferred_element_type=F32)
    a_cumt = jnp.dot(dat, (row <= col).astype(F32), precision=HI,
                     preferred_element_type=F32)
    xdt = xs * jnp.dot(dt, e_ref[...], precision=HI, preferred_element_type=F32)
    xdt_b = xdt.astype(BF16)

    lane = lax.broadcasted_iota(jnp.int32, (BLK, LANE), 1)
    first = lane < SSD_HEAD_DIM
    rowp = lax.broadcasted_iota(jnp.int32, (LANE, 1), 0) < SSD_HEAD_DIM
    y_parts = []
    for g in range(SSD_GROUPS):
        b_g = act[:, D + g * SSD_STATE:D + (g + 1) * SSD_STATE]
        c_g = act[:, D + SSD_GROUPS * SSD_STATE + g * SSD_STATE:
                  D + SSD_GROUPS * SSD_STATE + (g + 1) * SSD_STATE]
        b_gb = b_g.astype(BF16)
        c_gb = c_g.astype(BF16)
        cbm = lax.dot_general(c_gb, b_gb, NT, preferred_element_type=F32)
        for pp in range(2):
            pair = 2 * g + pp
            h0 = 2 * pair
            x_pair = xdt_b[:, pair * LANE:(pair + 1) * LANE]
            y_pair = jnp.zeros((BLK, LANE), F32)
            for hh in range(2):
                h = h0 + hh
                lm = jnp.where(lower, jnp.exp(a_cum[:, h:h + 1] - a_cumt[h:h + 1, :]), 0.0)
                w = (cbm * lm).astype(BF16)
                xm = jnp.where(first if hh == 0 else jnp.logical_not(first), x_pair,
                               jnp.zeros_like(x_pair))
                y_pair = y_pair + jnp.dot(w, xm, preferred_element_type=F32)
            ac_pair = jnp.where(first, a_cum[:, h0:h0 + 1], a_cum[:, h0 + 1:h0 + 2])
            al_pair = ac_pair[BLK - 1:BLK, :]
            r_pair = st_ref[pair]
            y_off = lax.dot_general(c_gb, r_pair.astype(BF16), NT, preferred_element_type=F32)
            y_pair = y_pair + y_off * jnp.exp(ac_pair)
            xdec = (xdt[:, pair * LANE:(pair + 1) * LANE] * jnp.exp(al_pair - ac_pair)).astype(BF16)
            st_new = lax.dot_general(xdec, b_gb, TN, preferred_element_type=F32)
            al_col = jnp.where(rowp, a_cumt[h0:h0 + 1, BLK - 1:BLK],
                               a_cumt[h0 + 1:h0 + 2, BLK - 1:BLK])
            st_ref[pair] = jnp.exp(al_col) * r_pair + st_new
            y_parts.append(y_pair)
    y = jnp.concatenate(y_parts, axis=1) + dsk_ref[...] * xs
    z = z_ref[...].astype(F32)
    y = y * (z * jax.nn.sigmoid(z))
    gw = D // SSD_GROUPS
    for g in range(SSD_GROUPS):
        yg = y[:, g * gw:(g + 1) * gw]
        yg = yg * lax.rsqrt(jnp.mean(yg * yg, axis=-1, keepdims=True) + EPS)
        out_ref[:, g * gw:(g + 1) * gw] = (yg * nw_ref[:, g * gw:(g + 1) * gw]).astype(BF16)


def _ssd(proj, dt, dtt, cw, cb, dtb, dtbt, al, alt, dsk, nw, expand):
    s = proj.shape[0]
    row = lambda c: pl.BlockSpec((BLK, D), lambda i, c=c: (i, c))
    return pl.pallas_call(
        _ssd_kernel,
        grid=(s // BLK,),
        in_specs=[
            row(C_Z), row(C_XS), row(C_BC),
            pl.BlockSpec((BLK, LANE), lambda i: (i, 0)),
            pl.BlockSpec((SSD_HEADS, BLK), lambda i: (0, i)),
            _const_spec((SSD_CONV, 2 * D)), _const_spec((1, 2 * D)),
            _const_spec((1, LANE)), _const_spec((SSD_HEADS, 1)),
            _const_spec((1, LANE)), _const_spec((SSD_HEADS, 1)),
            _const_spec((1, D)), _const_spec((1, D)), _const_spec((LANE, D)),
        ],
        out_specs=pl.BlockSpec((BLK, D), lambda i: (i, 0)),
        out_shape=jax.ShapeDtypeStruct((s, D), BF16),
        scratch_shapes=[pltpu.VMEM((BLK + 8, 2 * D), F32),
                        pltpu.VMEM((SSD_HEADS // 2, LANE, SSD_STATE), F32)],
        compiler_params=_cparams(1),
        name="ssd",
    )(proj, proj, proj, dt, dtt, cw, cb, dtb, dtbt, al, alt, dsk, nw, expand)


def _s5_prep_kernel(lrr_ref, lir_ref, lsr_ref, lrc_ref, lic_ref, lsc_ref,
                    bre_ref, bim_ref, cre_ref, cim_ref,
                    m_ref, bxr_ref, bxi_ref, cxr_ref, cxi_ref, alr_ref, ali_ref):
    rows = 2 * S5_L * S5_CH
    cols = 2 * S5_L * S5_CH
    lr = jnp.minimum(lrr_ref[0], -S5_CLIP)
    li = lir_ref[0]
    st = jnp.exp(lsr_ref[0])
    lrs, lis = lr * st, li * st
    mag = jnp.exp(lrs)
    abr, abi = mag * jnp.cos(lis), mag * jnp.sin(lis)
    nr, ni = abr - 1.0, abi
    den = lr * lr + li * li
    cfr = (nr * lr + ni * li) / den
    cfi = (ni * lr - nr * li) / den
    bre, bim = bre_ref[0], bim_ref[0]
    bbr = cfr * bre - cfi * bim
    bbi = cfr * bim + cfi * bre
    rk = lax.broadcasted_iota(jnp.int32, (rows, 1), 0)
    ek = (S5_L - 1 - (rk % (S5_L * S5_CH)) // S5_CH).astype(F32)
    pm = jnp.exp(ek * lrs)
    pbr, pbi = pm * jnp.cos(ek * lis), pm * jnp.sin(ek * lis)
    bxr = pbr * bbr - pbi * bbi
    bxi = pbr * bbi + pbi * bbr
    bxr_ref[0] = bxr.astype(BF16)
    bxi_ref[0] = bxi.astype(BF16)
    alm = jnp.exp(float(S5_L) * lrs)
    alr_ref[0] = alm * jnp.cos(float(S5_L) * lis)
    ali_ref[0] = alm * jnp.sin(float(S5_L) * lis)
    lrc = jnp.minimum(lrc_ref[0], -S5_CLIP)
    stc = jnp.exp(lsc_ref[0])
    lrsc, lisc = lrc * stc, lic_ref[0] * stc
    ct = lax.broadcasted_iota(jnp.int32, (1, cols), 1)
    et = ((ct % (S5_L * S5_CH)) // S5_CH + 1).astype(F32)
    qm = jnp.exp(et * lrsc)
    qr, qi = qm * jnp.cos(et * lisc), qm * jnp.sin(et * lisc)
    cre, cim = cre_ref[0], cim_ref[0]
    cxr_ref[0] = (cre * qr - cim * qi).astype(BF16)
    cxi_ref[0] = (-(cre * qi + cim * qr)).astype(BF16)
    rw = (jnp.dot(bxr, cre, precision=HI, preferred_element_type=F32)
          - jnp.dot(bxi, cim, precision=HI, preferred_element_type=F32))
    n = S5_L * S5_CH
    tcol = lax.broadcasted_iota(jnp.int32, (n, n), 1) // S5_CH
    for gi in range(2):
        rg = rw[gi * n:(gi + 1) * n, gi * n:(gi + 1) * n]
        m = jnp.zeros((n, n), F32)
        for t in range(S5_L):
            sh = (S5_L - 1 - t) * S5_CH
            if sh:
                shifted = jnp.concatenate([rg[sh:, :], jnp.zeros((sh, n), F32)], axis=0)
            else:
                shifted = rg
            m = jnp.where(tcol == t, shifted, m)
        m_ref[gi] = m.astype(BF16)


def _s5_prep(lrr, lir, lsr, lrc, lic, lsc, bre2, bim2, cre2, cim2):
    n = S5_L * S5_CH
    pair3 = lambda a, b: pl.BlockSpec((1, a, b), lambda i: (i, 0, 0))
    return pl.pallas_call(
        _s5_prep_kernel,
        grid=(S5_PAIRS,),
        in_specs=[pair3(1, LANE)] * 3 + [pair3(LANE, 1)] * 3
                 + [pair3(2 * n, LANE)] * 2 + [pair3(LANE, 2 * n)] * 2,
        out_specs=[pl.BlockSpec((2, n, n), lambda i: (i, 0, 0)),
                   pair3(2 * n, LANE), pair3(2 * n, LANE), pair3(LANE, 2 * n), pair3(LANE, 2 * n),
                   pair3(1, LANE), pair3(1, LANE)],
        out_shape=[jax.ShapeDtypeStruct((S5_GROUPS, n, n), BF16),
                   jax.ShapeDtypeStruct((S5_PAIRS, 2 * n, LANE), BF16),
                   jax.ShapeDtypeStruct((S5_PAIRS, 2 * n, LANE), BF16),
                   jax.ShapeDtypeStruct((S5_PAIRS, LANE, 2 * n), BF16),
                   jax.ShapeDtypeStruct((S5_PAIRS, LANE, 2 * n), BF16),
                   jax.ShapeDtypeStruct((S5_PAIRS, 1, LANE), F32),
                   jax.ShapeDtypeStruct((S5_PAIRS, 1, LANE), F32)],
        compiler_params=_cparams(1),
        name="s5_prep",
    )(lrr, lir, lsr, lrc, lic, lsc, bre2, bim2, cre2, cim2)


def _s5_local_kernel(u_ref, bxr_ref, bxi_ref, sr_ref, si_ref):
    up = jnp.concatenate([u_ref[0], u_ref[1]], axis=1).astype(BF16)
    sr_ref[...] = jnp.dot(up, bxr_ref[0], preferred_element_type=F32)
    si_ref[...] = jnp.dot(up, bxi_ref[0], preferred_element_type=F32)


def _s5_local(u3, bxr, bxi):
    nc = u3.shape[1]
    n = S5_L * S5_CH
    return pl.pallas_call(
        _s5_local_kernel,
        grid=(S5_PAIRS,),
        in_specs=[pl.BlockSpec((2, nc, n), lambda i: (i, 0, 0)),
                  pl.BlockSpec((1, 2 * n, LANE), lambda i: (i, 0, 0)),
                  pl.BlockSpec((1, 2 * n, LANE), lambda i: (i, 0, 0))],
        out_specs=[pl.BlockSpec((nc, LANE), lambda i: (0, i))] * 2,
        out_shape=[jax.ShapeDtypeStruct((nc, S5_PAIRS * LANE), F32)] * 2,
        compiler_params=_cparams(1),
        name="s5_local",
    )(u3, bxr, bxi)


def _s5_scan_kernel(xr_ref, xi_ref, ar_ref, ai_ref, pr_ref, pi_ref):
    ar, ai = ar_ref[...], ai_ref[...]
    w = ar.shape[1]

    def body(r, carry):
        sr, si = carry
        pr_ref[pl.ds(r, 1), :] = sr
        pi_ref[pl.ds(r, 1), :] = si
        xr = xr_ref[pl.ds(r, 1), :]
        xi = xi_ref[pl.ds(r, 1), :]
        return ar * sr - ai * si + xr, ar * si + ai * sr + xi

    lax.fori_loop(0, xr_ref.shape[0], body, (jnp.zeros((1, w), F32), jnp.zeros((1, w), F32)))


def _s5_scan(xr, xi, ar, ai, tw):
    nc, width = xr.shape
    col = pl.BlockSpec((nc, tw), lambda i: (0, i))
    vec = pl.BlockSpec((1, tw), lambda i: (0, i))
    return pl.pallas_call(
        _s5_scan_kernel,
        grid=(width // tw,),
        in_specs=[col, col, vec, vec],
        out_specs=[col, col],
        out_shape=[jax.ShapeDtypeStruct((nc, width), F32)] * 2,
        compiler_params=_cparams(1),
        name="s5_scan",
    )(xr, xi, ar, ai)


def _s5_out_kernel(u_ref, m_ref, pr_ref, pi_ref, cxr_ref, cxi_ref, y_ref):
    n = S5_L * S5_CH
    inter = (jnp.dot(pr_ref[...].astype(BF16), cxr_ref[0], preferred_element_type=F32)
             + jnp.dot(pi_ref[...].astype(BF16), cxi_ref[0], preferred_element_type=F32))
    for gi in range(2):
        intra = jnp.dot(u_ref[gi].astype(BF16), m_ref[gi], preferred_element_type=F32)
        y_ref[gi] = (intra + inter[:, gi * n:(gi + 1) * n]).astype(BF16)


def _s5_out(u3, m, pr, pi, cxr, cxi):
    nc = u3.shape[1]
    n = S5_L * S5_CH
    return pl.pallas_call(
        _s5_out_kernel,
        grid=(S5_PAIRS,),
        in_specs=[pl.BlockSpec((2, nc, n), lambda i: (i, 0, 0)),
                  pl.BlockSpec((2, n, n), lambda i: (i, 0, 0)),
                  pl.BlockSpec((nc, LANE), lambda i: (0, i)),
                  pl.BlockSpec((nc, LANE), lambda i: (0, i)),
                  pl.BlockSpec((1, LANE, 2 * n), lambda i: (i, 0, 0)),
                  pl.BlockSpec((1, LANE, 2 * n), lambda i: (i, 0, 0))],
        out_specs=pl.BlockSpec((2, nc, n), lambda i: (i, 0, 0)),
        out_shape=jax.ShapeDtypeStruct((S5_GROUPS, nc, n), BF16),
        compiler_params=_cparams(1),
        name="s5_out",
    )(u3, m, pr, pi, cxr, cxi)


def _kmean_kernel(k_ref, o_ref):
    for r in range(o_ref.shape[0]):
        o_ref[r:r + 1, :] = jnp.mean(k_ref[r * BLK:(r + 1) * BLK, :].astype(F32), axis=0, keepdims=True)


def _kmean(proj, rows):
    s = proj.shape[0]
    nb = s // BLK
    return pl.pallas_call(
        _kmean_kernel,
        grid=(nb // rows,),
        in_specs=[pl.BlockSpec((rows * BLK, D), lambda i: (i, C_K))],
        out_specs=pl.BlockSpec((rows, D), lambda i: (i, 0)),
        out_shape=jax.ShapeDtypeStruct((nb, D), F32),
        compiler_params=_cparams(1),
        name="kmean",
    )(proj)


def _moba_kernel(q_ref, k_ref, vt_ref, km_ref, o_ref, bias_ref, sa_ref, sb_ref, mxa_ref, mxb_ref):
    i = pl.program_id(1)
    nb = km_ref.shape[0]
    kg = min(MOBA_KG, nb)
    kt = kg * BLK
    q = q_ref[...].astype(F32)
    lane = lax.broadcasted_iota(jnp.int32, (BLK, LANE), 1)
    blk = lax.broadcasted_iota(jnp.int32, (nb, BLK), 0)
    blk_f = blk.astype(F32)
    scale = MOBA_HD ** -0.5 * math.log2(math.e)
    qs = []
    for h in range(2):
        in_head = (lane >= h * MOBA_HD) & (lane < (h + 1) * MOBA_HD)
        qh = jnp.where(in_head, q, 0.0)
        g = lax.dot_general(km_ref[...], qh, NT, precision=HI, preferred_element_type=F32)
        g = jnp.where(blk < i, g, -jnp.inf)
        sel = jnp.zeros((nb, BLK), jnp.bool_)
        for _ in range(MOBA_TOPK):
            mx = jnp.max(g, axis=0, keepdims=True)
            idx = jnp.min(jnp.where(g == mx, blk_f, float(nb)), axis=0, keepdims=True)
            hit = blk_f == idx
            sel = sel | (hit & (mx > -jnp.inf))
            g = jnp.where(hit, -jnp.inf, g)
        bias = jnp.where(sel, 0.0, NEG)
        for t in range(nb // kg):
            bias_ref[h, t, 0:kg, :] = bias[t * kg:(t + 1) * kg, :]
        qs.append((qh * scale).astype(BF16))

    krow = lax.broadcasted_iota(jnp.int32, (BLK, BLK), 0)
    qcol = lax.broadcasted_iota(jnp.int32, (BLK, BLK), 1)
    causal = krow <= qcol
    last_g = nb // kg - 1
    ones = lambda n: jnp.ones((16, n), BF16)

    def produce(t, s_ref, mx_ref):
        tc = jnp.minimum(t, last_g)
        kb = k_ref[pl.ds(pl.multiple_of(tc * kt, kt), kt), :]
        off = jnp.where(t > last_g, NEG, 0.0)
        for h in range(2):
            s_t = lax.dot_general(kb, qs[h], NT, preferred_element_type=F32)
            bias = bias_ref[h, tc, 0:kg, :] + off
            mx = None
            for u in range(kg):
                su = s_t[u * BLK:(u + 1) * BLK, :] + bias[u:u + 1, :]
                s_ref[h, u * BLK:(u + 1) * BLK, :] = su
                mu = jnp.max(su, axis=0, keepdims=True)
                mx = mu if mx is None else jnp.maximum(mx, mu)
            mx_ref[h, 0:1, :] = mx

    def fold(h, carry, mx, s, va):
        m, acc = carry
        mn = jnp.maximum(m, mx)
        p = jnp.exp2(s - mn).astype(BF16)
        return mn, jnp.exp2(m - mn) * acc + jnp.dot(va, p, preferred_element_type=F32)

    def consume(t, s_ref, mx_ref, carry):
        tc = jnp.minimum(t, last_g)
        vt = vt_ref[:, pl.ds(pl.multiple_of(tc * kt, kt), kt)]
        return tuple(
            fold(h, carry[h], mx_ref[h, 0:1, :], s_ref[h],
                 jnp.concatenate([vt[h * MOBA_HD:(h + 1) * MOBA_HD, :], ones(kt)], axis=0))
            for h in range(2))

    produce(0, sa_ref, mxa_ref)
    kb_own = k_ref[pl.ds(pl.multiple_of(i * BLK, BLK), BLK), :]
    vt_own = vt_ref[:, pl.ds(pl.multiple_of(i * BLK, BLK), BLK)]
    init = (jnp.full((1, BLK), NEG, F32), jnp.zeros((MOBA_HD + 16, BLK), F32))
    carry = []
    for h in range(2):
        s_own = jnp.where(causal, lax.dot_general(kb_own, qs[h], NT, preferred_element_type=F32), NEG)
        va = jnp.concatenate([vt_own[h * MOBA_HD:(h + 1) * MOBA_HD, :], ones(BLK)], axis=0)
        carry.append(fold(h, init, jnp.max(s_own, axis=0, keepdims=True), s_own, va))

    def body(r, carry):
        produce(2 * r + 1, sb_ref, mxb_ref)
        carry = consume(2 * r, sa_ref, mxa_ref, carry)
        produce(2 * r + 2, sa_ref, mxa_ref)
        return consume(2 * r + 1, sb_ref, mxb_ref, carry)

    n_groups = (i + kg - 1) // kg
    carry = lax.fori_loop(0, (n_groups + 1) // 2, body, tuple(carry))
    for h in range(2):
        _, acc = carry[h]
        o_ref[h * MOBA_HD:(h + 1) * MOBA_HD, :] = (
            acc[0:MOBA_HD, :] / acc[MOBA_HD:MOBA_HD + 1, :]).astype(BF16)


def _moba(proj, vt, kmean):
    s = proj.shape[0]
    nb = s // BLK
    assert nb % min(MOBA_KG, nb) == 0
    per = D // LANE
    return pl.pallas_call(
        _moba_kernel,
        grid=(MOBA_HEADS // 2, nb),
        in_specs=[pl.BlockSpec((BLK, LANE), lambda hp, i: (i, C_Q * per + hp)),
                  pl.BlockSpec((s, LANE), lambda hp, i: (0, C_K * per + hp)),
                  pl.BlockSpec((LANE, s), lambda hp, i: (hp, 0)),
                  pl.BlockSpec((nb, LANE), lambda hp, i: (0, hp))],
        out_specs=pl.BlockSpec((LANE, BLK), lambda hp, i: (hp, i)),
        out_shape=jax.ShapeDtypeStruct((D, s), BF16),
        scratch_shapes=[pltpu.VMEM((2, nb // min(MOBA_KG, nb), SUBLANE, BLK), F32)]
                       + [pltpu.VMEM((2, min(MOBA_KG, nb) * BLK, BLK), F32)] * 2
                       + [pltpu.VMEM((2, SUBLANE, BLK), F32)] * 2,
        compiler_params=_cparams(2),
        name="moba",
    )(proj, proj, vt, kmean)


def _merge_kernel(final, x_ref, ya_ref, ys_ref, u_ref, sg_ref, att_ref, mg_ref,
                  g0_ref, g1_ref, g2_ref, gb_ref, s5d_ref, glub_ref, fnw_ref,
                  gluw_ref, wa_ref, wb_ref, wc_ref, wo_ref, out_ref):
    mm = lambda a, w_ref: jnp.dot(a.astype(BF16), w_ref[...], preferred_element_type=F32)
    f32 = lambda r: r[...].astype(F32)
    yb = f32(ys_ref) + s5d_ref[...] * f32(u_ref)
    yb = jax.nn.gelu(yb)
    yb = yb * jax.nn.sigmoid(mm(yb, gluw_ref) + glub_ref[...])
    sg = f32(sg_ref)
    yb = yb * (sg * jax.nn.sigmoid(sg))
    mg = f32(mg_ref)
    att = f32(att_ref).T * (mg * jax.nn.sigmoid(mg))
    merged = (jax.nn.sigmoid(f32(g0_ref) + gb_ref[:, 0:D]) * mm(ya_ref[...], wa_ref)
              + jax.nn.sigmoid(f32(g1_ref) + gb_ref[:, D:2 * D]) * mm(yb, wb_ref)
              + jax.nn.sigmoid(f32(g2_ref) + gb_ref[:, 2 * D:3 * D]) * mm(att, wc_ref))
    xn = x_ref[...] + mm(merged, wo_ref)
    if final:
        xn = xn * lax.rsqrt(jnp.mean(xn * xn, axis=-1, keepdims=True) + EPS) * fnw_ref[...]
    out_ref[...] = xn


def _merge(final, x, ya, ys, proj, att_t, gate_b, s5d, glub, fnw, gluw, wa, wb, wc, wo, tm):
    s = x.shape[0]
    rowb = pl.BlockSpec((tm, D), lambda i: (i, 0))
    pcol = lambda c: pl.BlockSpec((tm, D), lambda i, c=c: (i, c))
    wspec = pl.BlockSpec((D, D), lambda i: (0, 0), pipeline_mode=pl.Buffered(1))
    return pl.pallas_call(
        functools.partial(_merge_kernel, final),
        grid=(s // tm,),
        in_specs=[rowb, rowb, rowb, pcol(C_S5U), pcol(C_S5G),
                  pl.BlockSpec((D, tm), lambda i: (0, i)), pcol(C_MG),
                  pcol(C_GL), pcol(C_GL + 1), pcol(C_GL + 2),
                  _const_spec((1, 3 * D)), _const_spec((1, D)), _const_spec((1, D)),
                  _const_spec((1, D)), wspec, wspec, wspec, wspec, wspec],
        out_specs=rowb,
        out_shape=jax.ShapeDtypeStruct((s, D), F32),
        compiler_params=_cparams(1),
        name="merge",
    )(x, ya, ys, proj, proj, att_t, proj, proj, proj, proj,
      gate_b, s5d, glub, fnw, gluw, wa, wb, wc, wo)


def _block_diag_pairs(a):
    g, r, c = a.shape
    a2 = a.reshape(g // 2, 2, r, c)
    out = jnp.einsum("pgrc,gh->pgrhc", a2, jnp.eye(2, dtype=a.dtype))
    return out.reshape(g // 2, 2 * r, 2 * c)


def _row_tile(s, want):
    t = min(want, s)
    while s % t:
        t //= 2
    return t


def kernel(x, norm_w, w_in, gate_b, conv_w, conv_b, dt_bias, a_log, ssd_d, ssd_norm_w, w_proj_a,
           lambda_re, lambda_im, log_step, s5_b_re, s5_b_im, s5_c_re, s5_c_im, s5_d, glu_w, glu_b,
           w_proj_b, w_proj_c, w_out, final_norm_w):
    b, s, _ = x.shape
    assert b == 1 and s % BLK == 0 and x.shape[2] == D
    depth = norm_w.shape[0]
    nb = s // BLK
    nc = s // S5_L
    xc = x.reshape(s, D)
    o_xbc, o_dt = D, 3 * D
    o_s5u = o_dt + SSD_HEADS
    o_q = o_s5u + 2 * D
    o_v = o_q + 2 * D
    o_mg = o_v + D
    expand = (jnp.arange(LANE)[:, None] == (jnp.arange(D)[None, :] // SSD_HEAD_DIM)).astype(F32)
    pad16 = lambda a: jnp.pad(a.astype(F32), (0, LANE - SSD_HEADS)).reshape(1, LANE)
    tile_b = lambda a: jnp.tile(a.transpose(0, 2, 1), (1, S5_L, 1))
    tile_c = lambda a: jnp.tile(a.transpose(0, 2, 1), (1, 1, S5_L))

    for l in range(depth):
        w = w_in[l]
        w_nat = jnp.concatenate([w[:, 0:o_dt], w[:, o_s5u:o_v], w[:, o_mg:]], axis=1).astype(BF16)
        w_vt = w[:, o_v:o_mg].T.astype(BF16)
        w_dt = jnp.pad(w[:, o_dt:o_s5u], ((0, 0), (0, LANE - SSD_HEADS)))
        w_dtt = w[:, o_dt:o_s5u].T
        proj, vt3, dt, dtt = _project(xc, norm_w[l].reshape(1, D), w_nat, w_vt, w_dt, w_dtt,
                                      _row_tile(s, 1024))

        ya = _ssd(proj, dt, dtt, conv_w[l], conv_b[l].reshape(1, 2 * D),
                  pad16(dt_bias[l]), dt_bias[l].reshape(SSD_HEADS, 1),
                  pad16(a_log[l]), a_log[l].reshape(SSD_HEADS, 1),
                  jnp.repeat(ssd_d[l], SSD_HEAD_DIM).reshape(1, D),
                  ssd_norm_w[l].reshape(1, D), expand)

        ls_full = jnp.repeat(log_step[l], S5_STATE)
        m, bxr, bxi, cxr, cxi, alr, ali = _s5_prep(
            lambda_re[l].reshape(S5_PAIRS, 1, LANE), lambda_im[l].reshape(S5_PAIRS, 1, LANE),
            ls_full.reshape(S5_PAIRS, 1, LANE),
            lambda_re[l].reshape(S5_PAIRS, LANE, 1), lambda_im[l].reshape(S5_PAIRS, LANE, 1),
            ls_full.reshape(S5_PAIRS, LANE, 1),
            _block_diag_pairs(tile_b(s5_b_re[l])), _block_diag_pairs(tile_b(s5_b_im[l])),
            _block_diag_pairs(tile_c(s5_c_re[l])), _block_diag_pairs(tile_c(s5_c_im[l])))
        u3 = (proj[:, C_S5U * D:(C_S5U + 1) * D].reshape(nc, S5_L, S5_GROUPS, S5_CH)
              .transpose(2, 0, 1, 3).reshape(S5_GROUPS, nc, S5_L * S5_CH))
        sr, si = _s5_local(u3, bxr, bxi)
        pr, pi = _s5_scan(sr, si, alr.reshape(1, -1), ali.reshape(1, -1), 512)
        y3 = _s5_out(u3, m, pr, pi, cxr, cxi)
        ys = (y3.reshape(S5_GROUPS, nc, S5_L, S5_CH).transpose(1, 2, 0, 3).reshape(s, D))

        km = _kmean(proj, math.gcd(nb, 8))
        att_t = _moba(proj, vt3, km)

        xc = _merge(l == depth - 1, xc, ya, ys, proj, att_t, gate_b[l].reshape(1, 3 * D),
                    s5_d[l].reshape(1, D), glu_b[l].reshape(1, D), final_norm_w.reshape(1, D),
                    glu_w[l].astype(BF16), w_proj_a[l].astype(BF16), w_proj_b[l].astype(BF16),
                    w_proj_c[l].astype(BF16), w_out[l].astype(BF16), _row_tile(s, 256))
    return xc.reshape(b, s, D)
```

```python
import functools
import math

import jax
import jax.numpy as jnp
from jax import lax
from jax.experimental import pallas as pl
from jax.experimental.pallas import tpu as pltpu

F32 = jnp.float32
BF16 = jnp.bfloat16
HI = lax.Precision.HIGHEST

D = 1024
EPS = 1e-6
BLK = 256
SSD_HEADS = 16
SSD_HEAD_DIM = 64
SSD_GROUPS = 4
SSD_STATE = 128
SSD_CONV = 4
S5_GROUPS = 64
S5_CH = 16
S5_STATE = 64
S5_CLIP = 1e-4
S5_L = 16
S5_PAIRS = S5_GROUPS // 2
S5_OCT = 8
MOBA_HEADS = 16
MOBA_HD = 64
MOBA_TOPK = 3
MOBA_KG = 4
NEG = -1e30
LANE = 128
SUBLANE = 8
VMEM_LIMIT = 56 * 1024 * 1024

C_Z, C_XS, C_BC, C_S5U, C_S5G, C_Q, C_K, C_MG, C_GL = 0, 1, 2, 3, 4, 5, 6, 7, 8
N_NAT = 11 * D

NT = (((1,), (1,)), ((), ()))
TN = (((0,), (0,)), ((), ()))


def _cparams(n_axes):
    return pltpu.CompilerParams(dimension_semantics=("arbitrary",) * n_axes,
                                vmem_limit_bytes=VMEM_LIMIT)


def _const_spec(shape):
    nd = len(shape)
    return pl.BlockSpec(shape, lambda *_: (0,) * nd)


def _proj_kernel(x_ref, nw_ref, w_ref, wvt_ref, wdt_ref,
                 proj_ref, vt_ref, dt_ref, dtt_ref, h_ref):
    @pl.when(pl.program_id(1) == 0)
    def _():
        x = x_ref[...]
        h = x * lax.rsqrt(jnp.mean(x * x, axis=-1, keepdims=True) + EPS) * nw_ref[...]
        hb = h.astype(BF16)
        h_ref[...] = hb
        vt = lax.dot_general(wvt_ref[...], hb, NT, preferred_element_type=F32)
        vt_ref[...] = vt.astype(BF16)
        dt_ref[...] = jnp.dot(h, wdt_ref[...], precision=HI, preferred_element_type=F32)
        dtt_ref[...] = dt_ref[...].T[0:SSD_HEADS, :]

    proj_ref[...] = jnp.dot(h_ref[...], w_ref[...], preferred_element_type=F32).astype(BF16)


def _project(x, norm_w, w_nat, w_vt, w_dt, tm):
    s = x.shape[0]
    grid = (s // tm, N_NAT // D)
    return pl.pallas_call(
        _proj_kernel,
        grid=grid,
        in_specs=[
            pl.BlockSpec((tm, D), lambda i, j: (i, 0)),
            _const_spec((1, D)),
            pl.BlockSpec((D, D), lambda i, j: (0, j)),
            _const_spec((D, D)),
            _const_spec((D, LANE)),
        ],
        out_specs=[
            pl.BlockSpec((tm, D), lambda i, j: (i, j)),
            pl.BlockSpec((D, tm), lambda i, j: (0, i)),
            pl.BlockSpec((tm, LANE), lambda i, j: (i, 0)),
            pl.BlockSpec((SSD_HEADS, tm), lambda i, j: (0, i)),
        ],
        out_shape=[
            jax.ShapeDtypeStruct((s, N_NAT), BF16),
            jax.ShapeDtypeStruct((D, s), BF16),
            jax.ShapeDtypeStruct((s, LANE), F32),
            jax.ShapeDtypeStruct((SSD_HEADS, s), F32),
        ],
        scratch_shapes=[pltpu.VMEM((tm, D), BF16)],
        compiler_params=_cparams(2),
        name="proj",
    )(x, norm_w, w_nat, w_vt, w_dt)


def _sigmoid(v):
    return 0.5 * jnp.tanh(0.5 * v) + 0.5


def _split3(v):
    hi = v.astype(BF16)
    r1 = v - hi.astype(F32)
    mid = r1.astype(BF16)
    lo = (r1 - mid.astype(F32)).astype(BF16)
    return hi, mid, lo


def _dot_exact_lhs(a_exact, v):
    ab = a_exact.astype(BF16)
    return sum(jnp.dot(ab, t, preferred_element_type=F32) for t in _split3(v))


def _dot_exact_rhs(v, b_exact):
    bb = b_exact.astype(BF16)
    return sum(jnp.dot(t, bb, preferred_element_type=F32) for t in _split3(v))


def _softplus(v):
    return jnp.maximum(v, 0.0) + jnp.log1p(jnp.exp(-jnp.abs(v)))


def _ssd_kernel(z_ref, xs_ref, bc_ref, dt_ref, dtt_ref, cw_ref, cb_ref, dtb_ref, dtbt_ref,
                al_ref, alt_ref, dsk_ref, nw_ref, e_ref, out_ref, ext_ref, st_ref):
    @pl.when(pl.program_id(0) == 0)
    def _():
        ext_ref[0:8, :] = jnp.zeros((8, 2 * D), F32)
        st_ref[...] = jnp.zeros(st_ref.shape, F32)

    ext_ref[8:8 + BLK, 0:D] = xs_ref[...].astype(F32)
    ext_ref[8:8 + BLK, D:2 * D] = bc_ref[...].astype(F32)
    acc = cb_ref[...] + cw_ref[0:1, :] * ext_ref[5:5 + BLK, :]
    for i in range(1, SSD_CONV):
        acc = acc + cw_ref[i:i + 1, :] * ext_ref[5 + i:5 + i + BLK, :]
    ext_ref[0:8, :] = ext_ref[BLK:BLK + 8, :]
    act = acc * _sigmoid(acc)
    xs = act[:, 0:D]

    dt = _softplus(dt_ref[...] + dtb_ref[...])
    da = dt * (-jnp.exp(al_ref[...]))
    dtt = _softplus(dtt_ref[...] + dtbt_ref[...])
    dat = dtt * (-jnp.exp(alt_ref[...]))
    row = lax.broadcasted_iota(jnp.int32, (BLK, BLK), 0)
    col = lax.broadcasted_iota(jnp.int32, (BLK, BLK), 1)
    lower = row >= col
    tri = lower.astype(F32)
    a_cum = _dot_exact_lhs(tri, da)
    a_cumt = _dot_exact_rhs(dat, (row <= col).astype(F32))
    xdt = xs * _dot_exact_rhs(dt, e_ref[...])
    xdt_b = xdt.astype(BF16)

    lane = lax.broadcasted_iota(jnp.int32, (BLK, LANE), 1)
    first = lane < SSD_HEAD_DIM
    rowp = lax.broadcasted_iota(jnp.int32, (LANE, 1), 0) < SSD_HEAD_DIM
    y_parts = []
    for g in range(SSD_GROUPS):
        b_g = act[:, D + g * SSD_STATE:D + (g + 1) * SSD_STATE]
        c_g = act[:, D + SSD_GROUPS * SSD_STATE + g * SSD_STATE:
                  D + SSD_GROUPS * SSD_STATE + (g + 1) * SSD_STATE]
        b_gb = b_g.astype(BF16)
        c_gb = c_g.astype(BF16)
        cbm = lax.dot_general(c_gb, b_gb, NT, preferred_element_type=F32)
        for pp in range(2):
            pair = 2 * g + pp
            h0 = 2 * pair
            x_pair = xdt_b[:, pair * LANE:(pair + 1) * LANE]
            y_pair = jnp.zeros((BLK, LANE), F32)
            for hh in range(2):
                h = h0 + hh
                lm = jnp.where(lower, jnp.exp(a_cum[:, h:h + 1] - a_cumt[h:h + 1, :]), 0.0)
                w = (cbm * lm).astype(BF16)
                xm = jnp.where(first if hh == 0 else jnp.logical_not(first), x_pair,
                               jnp.zeros_like(x_pair))
                y_pair = y_pair + jnp.dot(w, xm, preferred_element_type=F32)
            ac_pair = jnp.where(first, a_cum[:, h0:h0 + 1], a_cum[:, h0 + 1:h0 + 2])
            al_pair = ac_pair[BLK - 1:BLK, :]
            r_pair = st_ref[pair]
            y_off = lax.dot_general(c_gb, r_pair.astype(BF16), NT, preferred_element_type=F32)
            y_pair = y_pair + y_off * jnp.exp(ac_pair)
            xdec = (xdt[:, pair * LANE:(pair + 1) * LANE] * jnp.exp(al_pair - ac_pair)).astype(BF16)
            st_new = lax.dot_general(xdec, b_gb, TN, preferred_element_type=F32)
            al_col = jnp.where(rowp, a_cumt[h0:h0 + 1, BLK - 1:BLK],
                               a_cumt[h0 + 1:h0 + 2, BLK - 1:BLK])
            st_ref[pair] = jnp.exp(al_col) * r_pair + st_new
            y_parts.append(y_pair)
    y = jnp.concatenate(y_parts, axis=1) + dsk_ref[...] * xs
    z = z_ref[...].astype(F32)
    y = y * (z * _sigmoid(z))
    gw = D // SSD_GROUPS
    for g in range(SSD_GROUPS):
        yg = y[:, g * gw:(g + 1) * gw]
        yg = yg * lax.rsqrt(jnp.mean(yg * yg, axis=-1, keepdims=True) + EPS)
        out_ref[:, g * gw:(g + 1) * gw] = (yg * nw_ref[:, g * gw:(g + 1) * gw]).astype(BF16)


def _ssd(proj, dt, dtt, cw, cb, dtb, dtbt, al, alt, dsk, nw, expand):
    s = proj.shape[0]
    row = lambda c: pl.BlockSpec((BLK, D), lambda i, c=c: (i, c))
    return pl.pallas_call(
        _ssd_kernel,
        grid=(s // BLK,),
        in_specs=[
            row(C_Z), row(C_XS), row(C_BC),
            pl.BlockSpec((BLK, LANE), lambda i: (i, 0)),
            pl.BlockSpec((SSD_HEADS, BLK), lambda i: (0, i)),
            _const_spec((SSD_CONV, 2 * D)), _const_spec((1, 2 * D)),
            _const_spec((1, LANE)), _const_spec((SSD_HEADS, 1)),
            _const_spec((1, LANE)), _const_spec((SSD_HEADS, 1)),
            _const_spec((1, D)), _const_spec((1, D)), _const_spec((LANE, D)),
        ],
        out_specs=pl.BlockSpec((BLK, D), lambda i: (i, 0)),
        out_shape=jax.ShapeDtypeStruct((s, D), BF16),
        scratch_shapes=[pltpu.VMEM((BLK + 8, 2 * D), F32),
                        pltpu.VMEM((SSD_HEADS // 2, LANE, SSD_STATE), F32)],
        compiler_params=_cparams(1),
        name="ssd",
    )(proj, proj, proj, dt, dtt, cw, cb, dtb, dtbt, al, alt, dsk, nw, expand)


def _s5_prep_kernel(lrr_ref, lir_ref, lsr_ref, lrc_ref, lic_ref, lsc_ref,
                    bre_ref, bim_ref, cre_ref, cim_ref,
                    m_ref, bxr_ref, bxi_ref, cxr_ref, cxi_ref, alr_ref, ali_ref):
    rows = 2 * S5_L * S5_CH
    cols = 2 * S5_L * S5_CH
    lr = jnp.minimum(lrr_ref[0], -S5_CLIP)
    li = lir_ref[0]
    st = jnp.exp(lsr_ref[0])
    lrs, lis = lr * st, li * st
    mag = jnp.exp(lrs)
    abr, abi = mag * jnp.cos(lis), mag * jnp.sin(lis)
    nr, ni = abr - 1.0, abi
    den = lr * lr + li * li
    cfr = (nr * lr + ni * li) / den
    cfi = (ni * lr - nr * li) / den
    bre, bim = bre_ref[0], bim_ref[0]
    bbr = cfr * bre - cfi * bim
    bbi = cfr * bim + cfi * bre
    rk = lax.broadcasted_iota(jnp.int32, (rows, 1), 0)
    ek = (S5_L - 1 - (rk % (S5_L * S5_CH)) // S5_CH).astype(F32)
    pm = jnp.exp(ek * lrs)
    pbr, pbi = pm * jnp.cos(ek * lis), pm * jnp.sin(ek * lis)
    bxr = pbr * bbr - pbi * bbi
    bxi = pbr * bbi + pbi * bbr
    bxr_ref[0] = bxr.astype(BF16)
    bxi_ref[0] = bxi.astype(BF16)
    alm = jnp.exp(float(S5_L) * lrs)
    alr_ref[0] = alm * jnp.cos(float(S5_L) * lis)
    ali_ref[0] = alm * jnp.sin(float(S5_L) * lis)
    lrc = jnp.minimum(lrc_ref[0], -S5_CLIP)
    stc = jnp.exp(lsc_ref[0])
    lrsc, lisc = lrc * stc, lic_ref[0] * stc
    ct = lax.broadcasted_iota(jnp.int32, (1, cols), 1)
    et = ((ct % (S5_L * S5_CH)) // S5_CH + 1).astype(F32)
    qm = jnp.exp(et * lrsc)
    qr, qi = qm * jnp.cos(et * lisc), qm * jnp.sin(et * lisc)
    cre, cim = cre_ref[0], cim_ref[0]
    cxr_ref[0] = (cre * qr - cim * qi).astype(BF16)
    cxi_ref[0] = (-(cre * qi + cim * qr)).astype(BF16)
    rw = (jnp.dot(bxr, cre, precision=HI, preferred_element_type=F32)
          - jnp.dot(bxi, cim, precision=HI, preferred_element_type=F32))
    n = S5_L * S5_CH
    tcol = lax.broadcasted_iota(jnp.int32, (n, n), 1) // S5_CH
    for gi in range(2):
        rg = rw[gi * n:(gi + 1) * n, gi * n:(gi + 1) * n]
        m = jnp.zeros((n, n), F32)
        for t in range(S5_L):
            sh = (S5_L - 1 - t) * S5_CH
            if sh:
                shifted = jnp.concatenate([rg[sh:, :], jnp.zeros((sh, n), F32)], axis=0)
            else:
                shifted = rg
            m = jnp.where(tcol == t, shifted, m)
        m_ref[gi] = m.astype(BF16)


def _s5_prep(lrr, lir, lsr, lrc, lic, lsc, bre2, bim2, cre2, cim2):
    n = S5_L * S5_CH
    pair3 = lambda a, b: pl.BlockSpec((1, a, b), lambda i: (i, 0, 0))
    return pl.pallas_call(
        _s5_prep_kernel,
        grid=(S5_PAIRS,),
        in_specs=[pair3(1, LANE)] * 3 + [pair3(LANE, 1)] * 3
                 + [pair3(2 * n, LANE)] * 2 + [pair3(LANE, 2 * n)] * 2,
        out_specs=[pl.BlockSpec((2, n, n), lambda i: (i, 0, 0)),
                   pair3(2 * n, LANE), pair3(2 * n, LANE), pair3(LANE, 2 * n), pair3(LANE, 2 * n),
                   pair3(1, LANE), pair3(1, LANE)],
        out_shape=[jax.ShapeDtypeStruct((S5_GROUPS, n, n), BF16),
                   jax.ShapeDtypeStruct((S5_PAIRS, 2 * n, LANE), BF16),
                   jax.ShapeDtypeStruct((S5_PAIRS, 2 * n, LANE), BF16),
                   jax.ShapeDtypeStruct((S5_PAIRS, LANE, 2 * n), BF16),
                   jax.ShapeDtypeStruct((S5_PAIRS, LANE, 2 * n), BF16),
                   jax.ShapeDtypeStruct((S5_PAIRS, 1, LANE), F32),
                   jax.ShapeDtypeStruct((S5_PAIRS, 1, LANE), F32)],
        compiler_params=_cparams(1),
        name="s5_prep",
    )(lrr, lir, lsr, lrc, lic, lsc, bre2, bim2, cre2, cim2)


def _s5_local_kernel(u_ref, bxr_ref, bxi_ref, sr_ref, si_ref):
    sr_ref[...] = jnp.dot(u_ref[0], bxr_ref[0], preferred_element_type=F32)
    si_ref[...] = jnp.dot(u_ref[0], bxi_ref[0], preferred_element_type=F32)


def _s5_local(u8, bxr8, bxi8):
    nc, w = u8.shape[1], u8.shape[2]
    ow = S5_OCT * S5_STATE
    return pl.pallas_call(
        _s5_local_kernel,
        grid=(S5_GROUPS // S5_OCT,),
        in_specs=[pl.BlockSpec((1, nc, w), lambda i: (i, 0, 0)),
                  pl.BlockSpec((1, w, ow), lambda i: (i, 0, 0)),
                  pl.BlockSpec((1, w, ow), lambda i: (i, 0, 0))],
        out_specs=[pl.BlockSpec((nc, ow), lambda i: (0, i))] * 2,
        out_shape=[jax.ShapeDtypeStruct((nc, S5_GROUPS * S5_STATE), F32)] * 2,
        compiler_params=_cparams(1),
        name="s5_local",
    )(u8, bxr8, bxi8)


def _s5_scan_kernel(xr_ref, xi_ref, ar_ref, ai_ref, pr_ref, pi_ref):
    ar, ai = ar_ref[...], ai_ref[...]
    w = ar.shape[1]

    def body(r, carry):
        sr, si = carry
        pr_ref[pl.ds(r, 1), :] = sr
        pi_ref[pl.ds(r, 1), :] = si
        xr = xr_ref[pl.ds(r, 1), :]
        xi = xi_ref[pl.ds(r, 1), :]
        return ar * sr - ai * si + xr, ar * si + ai * sr + xi

    lax.fori_loop(0, xr_ref.shape[0], body, (jnp.zeros((1, w), F32), jnp.zeros((1, w), F32)))


def _s5_scan(xr, xi, ar, ai, tw):
    nc, width = xr.shape
    col = pl.BlockSpec((nc, tw), lambda i: (0, i))
    vec = pl.BlockSpec((1, tw), lambda i: (0, i))
    return pl.pallas_call(
        _s5_scan_kernel,
        grid=(width // tw,),
        in_specs=[col, col, vec, vec],
        out_specs=[col, col],
        out_shape=[jax.ShapeDtypeStruct((nc, width), F32)] * 2,
        compiler_params=_cparams(1),
        name="s5_scan",
    )(xr, xi, ar, ai)


def _s5_out_kernel(u_ref, m_ref, pr_ref, pi_ref, cxr_ref, cxi_ref, y_ref):
    y = (jnp.dot(u_ref[0], m_ref[0], preferred_element_type=F32)
         + jnp.dot(pr_ref[...].astype(BF16), cxr_ref[0], preferred_element_type=F32)
         + jnp.dot(pi_ref[...].astype(BF16), cxi_ref[0], preferred_element_type=F32))
    y_ref[0] = y.astype(BF16)


def _s5_out(u8, m8, pr, pi, cxr8, cxi8, tn):
    n_oct, nc, w = u8.shape
    ow = S5_OCT * S5_STATE
    return pl.pallas_call(
        _s5_out_kernel,
        grid=(n_oct, w // tn),
        in_specs=[pl.BlockSpec((1, nc, w), lambda i, j: (i, 0, 0)),
                  pl.BlockSpec((1, w, tn), lambda i, j: (i, 0, j)),
                  pl.BlockSpec((nc, ow), lambda i, j: (0, i)),
                  pl.BlockSpec((nc, ow), lambda i, j: (0, i)),
                  pl.BlockSpec((1, ow, tn), lambda i, j: (i, 0, j)),
                  pl.BlockSpec((1, ow, tn), lambda i, j: (i, 0, j))],
        out_specs=pl.BlockSpec((1, nc, tn), lambda i, j: (i, 0, j)),
        out_shape=jax.ShapeDtypeStruct((n_oct, nc, w), BF16),
        compiler_params=_cparams(2),
        name="s5_out",
    )(u8, m8, pr, pi, cxr8, cxi8)


def _kmean_kernel(k_ref, o_ref):
    for r in range(o_ref.shape[0]):
        o_ref[r:r + 1, :] = jnp.mean(k_ref[r * BLK:(r + 1) * BLK, :].astype(F32), axis=0, keepdims=True)


def _kmean(proj, rows):
    s = proj.shape[0]
    nb = s // BLK
    return pl.pallas_call(
        _kmean_kernel,
        grid=(nb // rows,),
        in_specs=[pl.BlockSpec((rows * BLK, D), lambda i: (i, C_K))],
        out_specs=pl.BlockSpec((rows, D), lambda i: (i, 0)),
        out_shape=jax.ShapeDtypeStruct((nb, D), F32),
        compiler_params=_cparams(1),
        name="kmean",
    )(proj)


def _moba_kernel(q_ref, k_ref, vt_ref, km_ref, o_ref, bias_ref, s_ref):
    i = pl.program_id(1)
    nb = km_ref.shape[0]
    kg = min(MOBA_KG, nb)
    kt = kg * BLK
    q = q_ref[...].astype(F32)
    lane = lax.broadcasted_iota(jnp.int32, (BLK, LANE), 1)
    blk = lax.broadcasted_iota(jnp.int32, (nb, BLK), 0)
    blk_f = blk.astype(F32)
    scale = MOBA_HD ** -0.5 * math.log2(math.e)
    qs = []
    for h in range(2):
        in_head = (lane >= h * MOBA_HD) & (lane < (h + 1) * MOBA_HD)
        qh = jnp.where(in_head, q, 0.0)
        g = lax.dot_general(km_ref[...], qh, NT, precision=HI, preferred_element_type=F32)
        g = jnp.where(blk < i, g, -jnp.inf)
        sel = jnp.zeros((nb, BLK), jnp.bool_)
        for _ in range(MOBA_TOPK):
            mx = jnp.max(g, axis=0, keepdims=True)
            idx = jnp.min(jnp.where(g == mx, blk_f, float(nb)), axis=0, keepdims=True)
            hit = blk_f == idx
            sel = sel | (hit & (mx > -jnp.inf))
            g = jnp.where(hit, -jnp.inf, g)
        bias = jnp.where(sel, 0.0, NEG)
        for t in range(nb // kg):
            bias_ref[h, t, 0:kg, :] = bias[t * kg:(t + 1) * kg, :]
        qs.append((qh * scale).astype(BF16))

    krow = lax.broadcasted_iota(jnp.int32, (BLK, BLK), 0)
    qcol = lax.broadcasted_iota(jnp.int32, (BLK, BLK), 1)
    causal = krow <= qcol
    last_g = nb // kg - 1
    ones = lambda n: jnp.ones((16, n), BF16)

    def produce(t, h):
        tc = jnp.minimum(t, last_g)
        kb = k_ref[pl.ds(pl.multiple_of(tc * kt, kt), kt), :]
        s_t = lax.dot_general(kb, qs[h], NT, preferred_element_type=F32)
        bias = bias_ref[h, tc, 0:kg, :]
        mx = None
        for u in range(kg):
            su = s_t[u * BLK:(u + 1) * BLK, :] + bias[u:u + 1, :]
            s_ref[h, u * BLK:(u + 1) * BLK, :] = su
            mu = jnp.max(su, axis=0, keepdims=True)
            mx = mu if mx is None else jnp.maximum(mx, mu)
        return mx

    def fold(carry, mx, s, va):
        m, acc = carry
        mn = jnp.maximum(m, mx)
        p = jnp.exp2(s - mn).astype(BF16)
        return mn, jnp.exp2(m - mn) * acc + jnp.dot(va, p, preferred_element_type=F32)

    maxima = tuple(produce(0, h) for h in range(2))
    kb_own = k_ref[pl.ds(pl.multiple_of(i * BLK, BLK), BLK), :]
    vt_own = vt_ref[:, pl.ds(pl.multiple_of(i * BLK, BLK), BLK)]
    init = (jnp.full((1, BLK), NEG, F32), jnp.zeros((MOBA_HD + 16, BLK), F32))
    state = []
    for h in range(2):
        s_own = jnp.where(causal, lax.dot_general(kb_own, qs[h], NT, preferred_element_type=F32), NEG)
        va = jnp.concatenate([vt_own[h * MOBA_HD:(h + 1) * MOBA_HD, :], ones(BLK)], axis=0)
        state.append(fold(init, jnp.max(s_own, axis=0, keepdims=True), s_own, va))

    def body(t, carry):
        state, maxima = carry
        vt = vt_ref[:, pl.ds(pl.multiple_of(t * kt, kt), kt)]
        new_state, new_maxima = [], []
        for h in range(2):
            m, acc = state[h]
            mn = jnp.maximum(m, maxima[h])
            p = jnp.exp2(s_ref[h] - mn).astype(BF16)
            new_maxima.append(produce(t + 1, h))
            va = jnp.concatenate([vt[h * MOBA_HD:(h + 1) * MOBA_HD, :], ones(kt)], axis=0)
            new_state.append((mn, jnp.exp2(m - mn) * acc + jnp.dot(va, p, preferred_element_type=F32)))
        return tuple(new_state), tuple(new_maxima)

    n_groups = (i + kg - 1) // kg
    carry, _ = lax.fori_loop(0, n_groups, body, (tuple(state), maxima))
    for h in range(2):
        _, acc = carry[h]
        o_ref[h * MOBA_HD:(h + 1) * MOBA_HD, :] = (
            acc[0:MOBA_HD, :] / acc[MOBA_HD:MOBA_HD + 1, :]).astype(BF16)


def _moba(proj, vt, kmean):
    s = proj.shape[0]
    nb = s // BLK
    assert nb % min(MOBA_KG, nb) == 0
    per = D // LANE
    return pl.pallas_call(
        _moba_kernel,
        grid=(MOBA_HEADS // 2, nb),
        in_specs=[pl.BlockSpec((BLK, LANE), lambda hp, i: (i, C_Q * per + hp)),
                  pl.BlockSpec((s, LANE), lambda hp, i: (0, C_K * per + hp)),
                  pl.BlockSpec((LANE, s), lambda hp, i: (hp, 0)),
                  pl.BlockSpec((nb, LANE), lambda hp, i: (0, hp))],
        out_specs=pl.BlockSpec((LANE, BLK), lambda hp, i: (hp, i)),
        out_shape=jax.ShapeDtypeStruct((D, s), BF16),
        scratch_shapes=[pltpu.VMEM((2, nb // min(MOBA_KG, nb), SUBLANE, BLK), F32),
                        pltpu.VMEM((2, min(MOBA_KG, nb) * BLK, BLK), F32)],
        compiler_params=_cparams(2),
        name="moba",
    )(proj, proj, vt, kmean)


def _merge_kernel(final, x_ref, ya_ref, ys_ref, u_ref, sg_ref, att_ref, mg_ref,
                  g0_ref, g1_ref, g2_ref, gb_ref, s5d_ref, glub_ref, fnw_ref,
                  gluw_ref, wa_ref, wb_ref, wc_ref, wo_ref, out_ref):
    mm = lambda a, w_ref: jnp.dot(a.astype(BF16), w_ref[...], preferred_element_type=F32)
    f32 = lambda r: r[...].astype(F32)
    yb = f32(ys_ref) + s5d_ref[...] * f32(u_ref)
    yb = jax.nn.gelu(yb)
    yb = yb * _sigmoid(mm(yb, gluw_ref) + glub_ref[...])
    sg = f32(sg_ref)
    yb = yb * (sg * _sigmoid(sg))
    mg = f32(mg_ref)
    att = f32(att_ref).T * (mg * _sigmoid(mg))
    merged = (_sigmoid(f32(g0_ref) + gb_ref[:, 0:D]) * mm(ya_ref[...], wa_ref)
              + _sigmoid(f32(g1_ref) + gb_ref[:, D:2 * D]) * mm(yb, wb_ref)
              + _sigmoid(f32(g2_ref) + gb_ref[:, 2 * D:3 * D]) * mm(att, wc_ref))
    xn = x_ref[...] + mm(merged, wo_ref)
    if final:
        xn = xn * lax.rsqrt(jnp.mean(xn * xn, axis=-1, keepdims=True) + EPS) * fnw_ref[...]
    out_ref[...] = xn


def _merge(final, x, ya, ys, proj, att_t, gate_b, s5d, glub, fnw, gluw, wa, wb, wc, wo, tm):
    s = x.shape[0]
    rowb = pl.BlockSpec((tm, D), lambda i: (i, 0))
    pcol = lambda c: pl.BlockSpec((tm, D), lambda i, c=c: (i, c))
    wspec = pl.BlockSpec((D, D), lambda i: (0, 0), pipeline_mode=pl.Buffered(1))
    return pl.pallas_call(
        functools.partial(_merge_kernel, final),
        grid=(s // tm,),
        in_specs=[rowb, rowb, rowb, pcol(C_S5U), pcol(C_S5G),
                  pl.BlockSpec((D, tm), lambda i: (0, i)), pcol(C_MG),
                  pcol(C_GL), pcol(C_GL + 1), pcol(C_GL + 2),
                  _const_spec((1, 3 * D)), _const_spec((1, D)), _const_spec((1, D)),
                  _const_spec((1, D)), wspec, wspec, wspec, wspec, wspec],
        out_specs=rowb,
        out_shape=jax.ShapeDtypeStruct((s, D), F32),
        compiler_params=_cparams(1),
        name="merge",
    )(x, ya, ys, proj, proj, att_t, proj, proj, proj, proj,
      gate_b, s5d, glub, fnw, gluw, wa, wb, wc, wo)


def _block_diag_pairs(a):
    g, r, c = a.shape
    a2 = a.reshape(g // 2, 2, r, c)
    out = jnp.einsum("pgrc,gh->pgrhc", a2, jnp.eye(2, dtype=a.dtype))
    return out.reshape(g // 2, 2 * r, 2 * c)


def _pair_diag_blocks(a, axis2):
    idx0 = [slice(None)] * a.ndim
    idx1 = [slice(None)] * a.ndim
    idx0[1], idx0[axis2] = 0, 0
    idx1[1], idx1[axis2] = 1, 1
    return jnp.stack([a[tuple(idx0)], a[tuple(idx1)]], axis=1)


def _octet_diag(a, spec):
    return jnp.einsum(spec, a, jnp.eye(S5_OCT, dtype=a.dtype))


def _row_tile(s, want):
    t = min(want, s)
    while s % t:
        t //= 2
    return t


def kernel(x, norm_w, w_in, gate_b, conv_w, conv_b, dt_bias, a_log, ssd_d, ssd_norm_w, w_proj_a,
           lambda_re, lambda_im, log_step, s5_b_re, s5_b_im, s5_c_re, s5_c_im, s5_d, glu_w, glu_b,
           w_proj_b, w_proj_c, w_out, final_norm_w):
    b, s, _ = x.shape
    assert b == 1 and s % BLK == 0 and x.shape[2] == D
    depth = norm_w.shape[0]
    nb = s // BLK
    nc = s // S5_L
    xc = x.reshape(s, D)
    o_dt = 3 * D
    o_s5u = o_dt + SSD_HEADS
    o_q = o_s5u + 2 * D
    o_v = o_q + 2 * D
    o_mg = o_v + D
    expand = (jnp.arange(LANE)[:, None] == (jnp.arange(D)[None, :] // SSD_HEAD_DIM)).astype(F32)
    pad16 = lambda a: jnp.pad(a.astype(F32), (0, LANE - SSD_HEADS)).reshape(1, LANE)
    tile_b = lambda a: jnp.tile(a.transpose(0, 2, 1), (1, S5_L, 1))
    tile_c = lambda a: jnp.tile(a.transpose(0, 2, 1), (1, 1, S5_L))

    for l in range(depth):
        w = w_in[l]
        w_nat = jnp.concatenate([w[:, 0:o_dt], w[:, o_s5u:o_v], w[:, o_mg:]], axis=1).astype(BF16)
        w_vt = w[:, o_v:o_mg].T.astype(BF16)
        w_dt = jnp.pad(w[:, o_dt:o_s5u], ((0, 0), (0, LANE - SSD_HEADS)))
        proj, vt, dt, dtt = _project(xc, norm_w[l].reshape(1, D), w_nat, w_vt, w_dt,
                                     _row_tile(s, 1024))

        ya = _ssd(proj, dt, dtt, conv_w[l], conv_b[l].reshape(1, 2 * D),
                  pad16(dt_bias[l]), dt_bias[l].reshape(SSD_HEADS, 1),
                  pad16(a_log[l]), a_log[l].reshape(SSD_HEADS, 1),
                  jnp.repeat(ssd_d[l], SSD_HEAD_DIM).reshape(1, D),
                  ssd_norm_w[l].reshape(1, D), expand)

        ls_full = jnp.repeat(log_step[l], S5_STATE)
        m, bxr, bxi, cxr, cxi, alr, ali = _s5_prep(
            lambda_re[l].reshape(S5_PAIRS, 1, LANE), lambda_im[l].reshape(S5_PAIRS, 1, LANE),
            ls_full.reshape(S5_PAIRS, 1, LANE),
            lambda_re[l].reshape(S5_PAIRS, LANE, 1), lambda_im[l].reshape(S5_PAIRS, LANE, 1),
            ls_full.reshape(S5_PAIRS, LANE, 1),
            _block_diag_pairs(tile_b(s5_b_re[l])), _block_diag_pairs(tile_b(s5_b_im[l])),
            _block_diag_pairs(tile_c(s5_c_re[l])), _block_diag_pairs(tile_c(s5_c_im[l])))
        n_oct = S5_GROUPS // S5_OCT
        u8 = (proj[:, C_S5U * D:(C_S5U + 1) * D].reshape(nc, S5_L, n_oct, LANE)
              .transpose(2, 0, 1, 3).reshape(n_oct, nc, S5_L * LANE))
        m8 = _octet_diag(m.reshape(n_oct, S5_OCT, S5_L, S5_CH, S5_L, S5_CH), "ogkdtc,gh->okgdthc")
        bxr8, bxi8 = (_octet_diag(_pair_diag_blocks(a.reshape(S5_PAIRS, 2, S5_L, S5_CH, 2, S5_STATE), 4)
                                  .reshape(n_oct, S5_OCT, S5_L, S5_CH, S5_STATE), "ogkdp,gh->okgdhp")
                      for a in (bxr, bxi))
        cxr8, cxi8 = (_octet_diag(_pair_diag_blocks(a.reshape(S5_PAIRS, 2, S5_STATE, 2, S5_L, S5_CH), 3)
                                  .reshape(n_oct, S5_OCT, S5_STATE, S5_L, S5_CH), "ogptc,gh->ogpthc")
                      for a in (cxr, cxi))
        w8 = S5_L * LANE
        m8 = m8.reshape(n_oct, w8, w8)
        bxr8, bxi8 = (a.reshape(n_oct, w8, S5_OCT * S5_STATE) for a in (bxr8, bxi8))
        cxr8, cxi8 = (a.reshape(n_oct, S5_OCT * S5_STATE, w8) for a in (cxr8, cxi8))
        sr, si = _s5_local(u8, bxr8, bxi8)
        pr, pi = _s5_scan(sr, si, alr.reshape(1, -1), ali.reshape(1, -1), 512)
        y8 = _s5_out(u8, m8, pr, pi, cxr8, cxi8, 512)
        ys = y8.reshape(n_oct, nc, S5_L, LANE).transpose(1, 2, 0, 3).reshape(s, D)

        km = _kmean(proj, math.gcd(nb, 8))
        att_t = _moba(proj, vt, km)

        xc = _merge(l == depth - 1, xc, ya, ys, proj, att_t, gate_b[l].reshape(1, 3 * D),
                    s5_d[l].reshape(1, D), glu_b[l].reshape(1, D), final_norm_w.reshape(1, D),
                    glu_w[l].astype(BF16), w_proj_a[l].astype(BF16), w_proj_b[l].astype(BF16),
                    w_proj_c[l].astype(BF16), w_out[l].astype(BF16), _row_tile(s, 512))
    return xc.reshape(b, s, D)
```

```python
import functools
import math

import jax
import jax.numpy as jnp
import numpy as np
from jax import lax
from jax.experimental import pallas as pl
from jax.experimental.pallas import tpu as pltpu

F32 = jnp.float32
BF16 = jnp.bfloat16
HI = lax.Precision.HIGHEST

D = 1024
EPS = 1e-6
BLK = 256
SSD_HEADS = 16
SSD_HEAD_DIM = 64
SSD_GROUPS = 4
SSD_STATE = 128
SSD_CONV = 4
S5_GROUPS = 64
S5_CH = 16
S5_STATE = 64
S5_CLIP = 1e-4
S5_L = 16
S5_PAIRS = S5_GROUPS // 2
S5_OCT = 8
MOBA_HEADS = 16
MOBA_HD = 64
MOBA_TOPK = 3
MOBA_KG = 4
NEG = -1e30
LANE = 128
SUBLANE = 8
VMEM_LIMIT = 56 * 1024 * 1024

C_Z, C_XS, C_BC, C_S5U, C_S5G, C_Q, C_K, C_MG, C_GL = 0, 1, 2, 3, 4, 5, 6, 7, 8
N_NAT = 11 * D

NT = (((1,), (1,)), ((), ()))
TN = (((0,), (0,)), ((), ()))


def _cparams(n_axes):
    return pltpu.CompilerParams(dimension_semantics=("arbitrary",) * n_axes,
                                vmem_limit_bytes=VMEM_LIMIT)


def _const_spec(shape):
    nd = len(shape)
    return pl.BlockSpec(shape, lambda *_: (0,) * nd)


def _proj_kernel(x_ref, nw_ref, w_ref, wvt_ref, wdt_ref,
                 proj_ref, vt_ref, dt_ref, dtt_ref, h_ref):
    @pl.when(pl.program_id(1) == 0)
    def _():
        x = x_ref[...]
        h = x * lax.rsqrt(jnp.mean(x * x, axis=-1, keepdims=True) + EPS) * nw_ref[...]
        hb = h.astype(BF16)
        h_ref[...] = hb
        vt = lax.dot_general(wvt_ref[...], hb, NT, preferred_element_type=F32)
        vt_ref[...] = vt.astype(BF16)
        dt_ref[...] = jnp.dot(h, wdt_ref[...], precision=HI, preferred_element_type=F32)
        dtt_ref[...] = dt_ref[...].T[0:SSD_HEADS, :]

    proj_ref[...] = jnp.dot(h_ref[...], w_ref[...], preferred_element_type=F32).astype(BF16)


def _project(x, norm_w, w_nat, w_vt, w_dt, tm):
    s = x.shape[0]
    grid = (s // tm, N_NAT // D)
    return pl.pallas_call(
        _proj_kernel,
        grid=grid,
        in_specs=[
            pl.BlockSpec((tm, D), lambda i, j: (i, 0)),
            _const_spec((1, D)),
            pl.BlockSpec((D, D), lambda i, j: (0, j)),
            _const_spec((D, D)),
            _const_spec((D, LANE)),
        ],
        out_specs=[
            pl.BlockSpec((tm, D), lambda i, j: (i, j)),
            pl.BlockSpec((D, tm), lambda i, j: (0, i)),
            pl.BlockSpec((tm, LANE), lambda i, j: (i, 0)),
            pl.BlockSpec((SSD_HEADS, tm), lambda i, j: (0, i)),
        ],
        out_shape=[
            jax.ShapeDtypeStruct((s, N_NAT), BF16),
            jax.ShapeDtypeStruct((D, s), BF16),
            jax.ShapeDtypeStruct((s, LANE), F32),
            jax.ShapeDtypeStruct((SSD_HEADS, s), F32),
        ],
        scratch_shapes=[pltpu.VMEM((tm, D), BF16)],
        compiler_params=_cparams(2),
        name="proj",
    )(x, norm_w, w_nat, w_vt, w_dt)


def _sigmoid(v):
    return 0.5 * jnp.tanh(0.5 * v) + 0.5


def _split3(v):
    hi = v.astype(BF16)
    r1 = v - hi.astype(F32)
    mid = r1.astype(BF16)
    lo = (r1 - mid.astype(F32)).astype(BF16)
    return hi, mid, lo


def _dot_exact_lhs(a_exact, v):
    ab = a_exact.astype(BF16)
    return sum(jnp.dot(ab, t, preferred_element_type=F32) for t in _split3(v))


def _dot_exact_rhs(v, b_exact):
    bb = b_exact.astype(BF16)
    return sum(jnp.dot(t, bb, preferred_element_type=F32) for t in _split3(v))


def _softplus(v):
    return jnp.maximum(v, 0.0) + jnp.log1p(jnp.exp(-jnp.abs(v)))


def _ssd_kernel(z_ref, xs_ref, bc_ref, dt_ref, dtt_ref, cw_ref, cb_ref, dtb_ref, dtbt_ref,
                al_ref, alt_ref, dsk_ref, nw_ref, e_ref, out_ref, ext_ref, st_ref):
    @pl.when(pl.program_id(0) == 0)
    def _():
        ext_ref[0:8, :] = jnp.zeros((8, 2 * D), F32)
        st_ref[...] = jnp.zeros(st_ref.shape, F32)

    ext_ref[8:8 + BLK, 0:D] = xs_ref[...].astype(F32)
    ext_ref[8:8 + BLK, D:2 * D] = bc_ref[...].astype(F32)
    acc = cb_ref[...] + cw_ref[0:1, :] * ext_ref[5:5 + BLK, :]
    for i in range(1, SSD_CONV):
        acc = acc + cw_ref[i:i + 1, :] * ext_ref[5 + i:5 + i + BLK, :]
    ext_ref[0:8, :] = ext_ref[BLK:BLK + 8, :]
    act = acc * _sigmoid(acc)
    xs = act[:, 0:D]

    dt = _softplus(dt_ref[...] + dtb_ref[...])
    da = dt * (-jnp.exp(al_ref[...]))
    dtt = _softplus(dtt_ref[...] + dtbt_ref[...])
    dat = dtt * (-jnp.exp(alt_ref[...]))
    row = lax.broadcasted_iota(jnp.int32, (BLK, BLK), 0)
    col = lax.broadcasted_iota(jnp.int32, (BLK, BLK), 1)
    lower = row >= col
    tri = lower.astype(F32)
    a_cum = _dot_exact_lhs(tri, da)
    a_cumt = _dot_exact_rhs(dat, (row <= col).astype(F32))
    xdt = xs * _dot_exact_rhs(dt, e_ref[...])
    xdt_b = xdt.astype(BF16)

    lane = lax.broadcasted_iota(jnp.int32, (BLK, LANE), 1)
    first = lane < SSD_HEAD_DIM
    rowp = lax.broadcasted_iota(jnp.int32, (LANE, 1), 0) < SSD_HEAD_DIM
    y_parts = []
    for g in range(SSD_GROUPS):
        b_g = act[:, D + g * SSD_STATE:D + (g + 1) * SSD_STATE]
        c_g = act[:, D + SSD_GROUPS * SSD_STATE + g * SSD_STATE:
                  D + SSD_GROUPS * SSD_STATE + (g + 1) * SSD_STATE]
        b_gb = b_g.astype(BF16)
        c_gb = c_g.astype(BF16)
        cbm = lax.dot_general(c_gb, b_gb, NT, preferred_element_type=F32)
        for pp in range(2):
            pair = 2 * g + pp
            h0 = 2 * pair
            x_pair = xdt_b[:, pair * LANE:(pair + 1) * LANE]
            y_pair = jnp.zeros((BLK, LANE), F32)
            for hh in range(2):
                h = h0 + hh
                lm = jnp.where(lower, jnp.exp(a_cum[:, h:h + 1] - a_cumt[h:h + 1, :]), 0.0)
                w = (cbm * lm).astype(BF16)
                xm = jnp.where(first if hh == 0 else jnp.logical_not(first), x_pair,
                               jnp.zeros_like(x_pair))
                y_pair = y_pair + jnp.dot(w, xm, preferred_element_type=F32)
            ac_pair = jnp.where(first, a_cum[:, h0:h0 + 1], a_cum[:, h0 + 1:h0 + 2])
            al_pair = ac_pair[BLK - 1:BLK, :]
            r_pair = st_ref[pair]
            y_off = lax.dot_general(c_gb, r_pair.astype(BF16), NT, preferred_element_type=F32)
            y_pair = y_pair + y_off * jnp.exp(ac_pair)
            xdec = (xdt[:, pair * LANE:(pair + 1) * LANE] * jnp.exp(al_pair - ac_pair)).astype(BF16)
            st_new = lax.dot_general(xdec, b_gb, TN, preferred_element_type=F32)
            al_col = jnp.where(rowp, a_cumt[h0:h0 + 1, BLK - 1:BLK],
                               a_cumt[h0 + 1:h0 + 2, BLK - 1:BLK])
            st_ref[pair] = jnp.exp(al_col) * r_pair + st_new
            y_parts.append(y_pair)
    y = jnp.concatenate(y_parts, axis=1) + dsk_ref[...] * xs
    z = z_ref[...].astype(F32)
    y = y * (z * _sigmoid(z))
    gw = D // SSD_GROUPS
    for g in range(SSD_GROUPS):
        yg = y[:, g * gw:(g + 1) * gw]
        yg = yg * lax.rsqrt(jnp.mean(yg * yg, axis=-1, keepdims=True) + EPS)
        out_ref[:, g * gw:(g + 1) * gw] = (yg * nw_ref[:, g * gw:(g + 1) * gw]).astype(BF16)


def _ssd(proj, dt, dtt, cw, cb, dtb, dtbt, al, alt, dsk, nw, expand):
    s = proj.shape[0]
    row = lambda c: pl.BlockSpec((BLK, D), lambda i, c=c: (i, c))
    return pl.pallas_call(
        _ssd_kernel,
        grid=(s // BLK,),
        in_specs=[
            row(C_Z), row(C_XS), row(C_BC),
            pl.BlockSpec((BLK, LANE), lambda i: (i, 0)),
            pl.BlockSpec((SSD_HEADS, BLK), lambda i: (0, i)),
            _const_spec((SSD_CONV, 2 * D)), _const_spec((1, 2 * D)),
            _const_spec((1, LANE)), _const_spec((SSD_HEADS, 1)),
            _const_spec((1, LANE)), _const_spec((SSD_HEADS, 1)),
            _const_spec((1, D)), _const_spec((1, D)), _const_spec((LANE, D)),
        ],
        out_specs=pl.BlockSpec((BLK, D), lambda i: (i, 0)),
        out_shape=jax.ShapeDtypeStruct((s, D), BF16),
        scratch_shapes=[pltpu.VMEM((BLK + 8, 2 * D), F32),
                        pltpu.VMEM((SSD_HEADS // 2, LANE, SSD_STATE), F32)],
        compiler_params=_cparams(1),
        name="ssd",
    )(proj, proj, proj, dt, dtt, cw, cb, dtb, dtbt, al, alt, dsk, nw, expand)


S5_LAGPAD = 3


def _s5_prep_kernel(lrr_ref, lir_ref, lsr_ref, lrc_ref, lic_ref, lsc_ref,
                    bre_ref, bim_ref, cre_ref, cim_ref, scat_ref,
                    bd_ref, bxr_ref, bxi_ref, cxr_ref, cxi_ref, alr_ref, ali_ref):
    n = S5_L * S5_CH
    rows = 2 * n
    cols = 2 * n
    bd_ref[0, :, 0:S5_LAGPAD * LANE] = jnp.zeros((LANE, S5_LAGPAD * LANE), BF16)
    bxr_ref[...] = jnp.zeros(bxr_ref.shape, BF16)
    bxi_ref[...] = jnp.zeros(bxi_ref.shape, BF16)
    lane128 = lax.broadcasted_iota(jnp.int32, (S5_CH, LANE), 1) // S5_CH
    rk = lax.broadcasted_iota(jnp.int32, (rows, 1), 0)
    ek = (S5_L - 1 - (rk % n) // S5_CH).astype(F32)
    ct = lax.broadcasted_iota(jnp.int32, (1, cols), 1)
    et = ((ct % n) // S5_CH + 1).astype(F32)
    for pr in range(S5_OCT // 2):
        lr = jnp.minimum(lrr_ref[pr], -S5_CLIP)
        li = lir_ref[pr]
        st = jnp.exp(lsr_ref[pr])
        lrs, lis = lr * st, li * st
        mag = jnp.exp(lrs)
        abr, abi = mag * jnp.cos(lis), mag * jnp.sin(lis)
        nr, ni = abr - 1.0, abi
        den = lr * lr + li * li
        cfr = (nr * lr + ni * li) / den
        cfi = (ni * lr - nr * li) / den
        bre, bim = bre_ref[pr], bim_ref[pr]
        bbr = cfr * bre - cfi * bim
        bbi = cfr * bim + cfi * bre
        pm = jnp.exp(ek * lrs)
        pbr, pbi = pm * jnp.cos(ek * lis), pm * jnp.sin(ek * lis)
        bxr = pbr * bbr - pbi * bbi
        bxi = pbr * bbi + pbi * bbr
        alm = jnp.exp(float(S5_L) * lrs)
        alr_ref[pr] = alm * jnp.cos(float(S5_L) * lis)
        ali_ref[pr] = alm * jnp.sin(float(S5_L) * lis)
        lrc = jnp.minimum(lrc_ref[pr], -S5_CLIP)
        stc = jnp.exp(lsc_ref[pr])
        lrsc, lisc = lrc * stc, lic_ref[pr] * stc
        qm = jnp.exp(et * lrsc)
        qr, qi = qm * jnp.cos(et * lisc), qm * jnp.sin(et * lisc)
        cre, cim = cre_ref[pr], cim_ref[pr]
        rsl = slice(pr * LANE, (pr + 1) * LANE)
        cxr_ref[0, rsl, :] = jnp.dot((cre * qr - cim * qi).astype(BF16), scat_ref[pr],
                                     preferred_element_type=F32).astype(BF16)
        cxi_ref[0, rsl, :] = jnp.dot((-(cre * qi + cim * qr)).astype(BF16), scat_ref[pr],
                                     preferred_element_type=F32).astype(BF16)
        rw = (jnp.dot(bxr, cre, precision=HI, preferred_element_type=F32)
              - jnp.dot(bxi, cim, precision=HI, preferred_element_type=F32))
        bxr_b, bxi_b = bxr.astype(BF16), bxi.astype(BF16)
        for gi in range(2):
            g8 = 2 * pr + gi
            for k in range(S5_L):
                src = slice(gi * n + k * S5_CH, gi * n + (k + 1) * S5_CH)
                dst = slice(k * LANE + g8 * S5_CH, k * LANE + (g8 + 1) * S5_CH)
                bxr_ref[0, dst, rsl] = bxr_b[src, :]
                bxi_ref[0, dst, rsl] = bxi_b[src, :]
            for j in range(S5_L):
                m = S5_L - 1 - j
                blk = rw[gi * n + m * S5_CH:gi * n + (m + 1) * S5_CH, gi * n:gi * n + LANE]
                bd_ref[0, g8 * S5_CH:(g8 + 1) * S5_CH,
                       (S5_LAGPAD + j) * LANE:(S5_LAGPAD + j + 1) * LANE] = jnp.where(
                           lane128 == g8, blk, 0.0).astype(BF16)


def _s5_prep(lrr, lir, lsr, lrc, lic, lsc, bre2, bim2, cre2, cim2, scat):
    n = S5_L * S5_CH
    half = S5_OCT // 2
    n_oct = S5_GROUPS // S5_OCT
    w8 = S5_L * LANE
    ow = S5_OCT * S5_STATE
    blk4 = lambda a, b: pl.BlockSpec((half, a, b), lambda i: (i, 0, 0))
    oct3 = lambda a, b: pl.BlockSpec((1, a, b), lambda i: (i, 0, 0))
    return pl.pallas_call(
        _s5_prep_kernel,
        grid=(n_oct,),
        in_specs=[blk4(1, LANE)] * 3 + [blk4(LANE, 1)] * 3
                 + [blk4(2 * n, LANE)] * 2 + [blk4(LANE, 2 * n)] * 2
                 + [_const_spec((half, 2 * n, w8))],
        out_specs=[oct3(LANE, (S5_LAGPAD + S5_L) * LANE),
                   oct3(w8, ow), oct3(w8, ow), oct3(ow, w8), oct3(ow, w8),
                   blk4(1, LANE), blk4(1, LANE)],
        out_shape=[jax.ShapeDtypeStruct((n_oct, LANE, (S5_LAGPAD + S5_L) * LANE), BF16),
                   jax.ShapeDtypeStruct((n_oct, w8, ow), BF16),
                   jax.ShapeDtypeStruct((n_oct, w8, ow), BF16),
                   jax.ShapeDtypeStruct((n_oct, ow, w8), BF16),
                   jax.ShapeDtypeStruct((n_oct, ow, w8), BF16),
                   jax.ShapeDtypeStruct((S5_PAIRS, 1, LANE), F32),
                   jax.ShapeDtypeStruct((S5_PAIRS, 1, LANE), F32)],
        compiler_params=_cparams(1),
        name="s5_prep",
    )(lrr, lir, lsr, lrc, lic, lsc, bre2, bim2, cre2, cim2, scat)


def _s5_local_kernel(u_ref, bxr_ref, bxi_ref, sr_ref, si_ref):
    sr_ref[...] = jnp.dot(u_ref[0], bxr_ref[0], preferred_element_type=F32)
    si_ref[...] = jnp.dot(u_ref[0], bxi_ref[0], preferred_element_type=F32)


def _s5_local(u8, bxr8, bxi8):
    nc, w = u8.shape[1], u8.shape[2]
    ow = S5_OCT * S5_STATE
    return pl.pallas_call(
        _s5_local_kernel,
        grid=(S5_GROUPS // S5_OCT,),
        in_specs=[pl.BlockSpec((1, nc, w), lambda i: (i, 0, 0)),
                  pl.BlockSpec((1, w, ow), lambda i: (i, 0, 0)),
                  pl.BlockSpec((1, w, ow), lambda i: (i, 0, 0))],
        out_specs=[pl.BlockSpec((nc, ow), lambda i: (0, i))] * 2,
        out_shape=[jax.ShapeDtypeStruct((nc, S5_GROUPS * S5_STATE), F32)] * 2,
        compiler_params=_cparams(1),
        name="s5_local",
    )(u8, bxr8, bxi8)


def _s5_scan_kernel(xr_ref, xi_ref, ar_ref, ai_ref, pr_ref, pi_ref):
    ar, ai = ar_ref[...], ai_ref[...]
    w = ar.shape[1]

    def body(r, carry):
        sr, si = carry
        pr_ref[pl.ds(r, 1), :] = sr
        pi_ref[pl.ds(r, 1), :] = si
        xr = xr_ref[pl.ds(r, 1), :]
        xi = xi_ref[pl.ds(r, 1), :]
        return ar * sr - ai * si + xr, ar * si + ai * sr + xi

    lax.fori_loop(0, xr_ref.shape[0], body, (jnp.zeros((1, w), F32), jnp.zeros((1, w), F32)))


def _s5_scan(xr, xi, ar, ai, tw):
    nc, width = xr.shape
    col = pl.BlockSpec((nc, tw), lambda i: (0, i))
    vec = pl.BlockSpec((1, tw), lambda i: (0, i))
    return pl.pallas_call(
        _s5_scan_kernel,
        grid=(width // tw,),
        in_specs=[col, col, vec, vec],
        out_specs=[col, col],
        out_shape=[jax.ShapeDtypeStruct((nc, width), F32)] * 2,
        compiler_params=_cparams(1),
        name="s5_scan",
    )(xr, xi, ar, ai)


def _s5_out_kernel(u_ref, bd_ref, pr_ref, pi_ref, cxr_ref, cxi_ref, y_ref):
    tt = S5_LAGPAD + 1
    prb, pib = pr_ref[...].astype(BF16), pi_ref[...].astype(BF16)
    for jt in range(S5_L // tt):
        osl = slice(jt * tt * LANE, (jt + 1) * tt * LANE)
        acc = (jnp.dot(prb, cxr_ref[0, :, osl], preferred_element_type=F32)
               + jnp.dot(pib, cxi_ref[0, :, osl], preferred_element_type=F32))
        for k in range((jt + 1) * tt):
            lag0 = jt * tt - k + S5_LAGPAD
            acc = acc + jnp.dot(u_ref[0, :, k * LANE:(k + 1) * LANE],
                                bd_ref[0, :, lag0 * LANE:(lag0 + tt) * LANE],
                                preferred_element_type=F32)
        y_ref[0, :, osl] = acc.astype(BF16)


def _s5_out(u8, bd, pr, pi, cxr8, cxi8):
    n_oct, nc, w = u8.shape
    ow = S5_OCT * S5_STATE
    return pl.pallas_call(
        _s5_out_kernel,
        grid=(n_oct,),
        in_specs=[pl.BlockSpec((1, nc, w), lambda i: (i, 0, 0)),
                  pl.BlockSpec((1, LANE, bd.shape[2]), lambda i: (i, 0, 0)),
                  pl.BlockSpec((nc, ow), lambda i: (0, i)),
                  pl.BlockSpec((nc, ow), lambda i: (0, i)),
                  pl.BlockSpec((1, ow, w), lambda i: (i, 0, 0)),
                  pl.BlockSpec((1, ow, w), lambda i: (i, 0, 0))],
        out_specs=pl.BlockSpec((1, nc, w), lambda i: (i, 0, 0)),
        out_shape=jax.ShapeDtypeStruct((n_oct, nc, w), BF16),
        compiler_params=_cparams(1),
        name="s5_out",
    )(u8, bd, pr, pi, cxr8, cxi8)


def _kmean_kernel(k_ref, o_ref):
    for r in range(o_ref.shape[0]):
        o_ref[r:r + 1, :] = jnp.mean(k_ref[r * BLK:(r + 1) * BLK, :].astype(F32), axis=0, keepdims=True)


def _kmean(proj, rows):
    s = proj.shape[0]
    nb = s // BLK
    return pl.pallas_call(
        _kmean_kernel,
        grid=(nb // rows,),
        in_specs=[pl.BlockSpec((rows * BLK, D), lambda i: (i, C_K))],
        out_specs=pl.BlockSpec((rows, D), lambda i: (i, 0)),
        out_shape=jax.ShapeDtypeStruct((nb, D), F32),
        compiler_params=_cparams(1),
        name="kmean",
    )(proj)


def _moba_kernel(q_ref, k_ref, vt_ref, km_ref, o_ref, bias_ref, s_ref):
    i = pl.program_id(1)
    nb = km_ref.shape[0]
    kg = min(MOBA_KG, nb)
    kt = kg * BLK
    q = q_ref[...].astype(F32)
    lane = lax.broadcasted_iota(jnp.int32, (BLK, LANE), 1)
    blk = lax.broadcasted_iota(jnp.int32, (nb, BLK), 0)
    blk_f = blk.astype(F32)
    scale = MOBA_HD ** -0.5 * math.log2(math.e)
    qs = []
    for h in range(2):
        in_head = (lane >= h * MOBA_HD) & (lane < (h + 1) * MOBA_HD)
        qh = jnp.where(in_head, q, 0.0)
        g = lax.dot_general(km_ref[...], qh, NT, precision=HI, preferred_element_type=F32)
        g = jnp.where(blk < i, g, -jnp.inf)
        sel = jnp.zeros((nb, BLK), jnp.bool_)
        for _ in range(MOBA_TOPK):
            mx = jnp.max(g, axis=0, keepdims=True)
            idx = jnp.min(jnp.where(g == mx, blk_f, float(nb)), axis=0, keepdims=True)
            hit = blk_f == idx
            sel = sel | (hit & (mx > -jnp.inf))
            g = jnp.where(hit, -jnp.inf, g)
        bias = jnp.where(sel, 0.0, NEG)
        for t in range(nb // kg):
            bias_ref[h, t, 0:kg, :] = bias[t * kg:(t + 1) * kg, :]
        qs.append((qh * scale).astype(BF16))

    krow = lax.broadcasted_iota(jnp.int32, (BLK, BLK), 0)
    qcol = lax.broadcasted_iota(jnp.int32, (BLK, BLK), 1)
    causal = krow <= qcol
    last_g = nb // kg - 1
    ones = lambda n: jnp.ones((16, n), BF16)

    def produce(t, h):
        tc = jnp.minimum(t, last_g)
        kb = k_ref[pl.ds(pl.multiple_of(tc * kt, kt), kt), :]
        s_t = lax.dot_general(kb, qs[h], NT, preferred_element_type=F32)
        bias = bias_ref[h, tc, 0:kg, :]
        mx = None
        for u in range(kg):
            su = s_t[u * BLK:(u + 1) * BLK, :] + bias[u:u + 1, :]
            s_ref[h, u * BLK:(u + 1) * BLK, :] = su
            mu = jnp.max(su, axis=0, keepdims=True)
            mx = mu if mx is None else jnp.maximum(mx, mu)
        return mx

    def fold(carry, mx, s, va):
        m, acc = carry
        mn = jnp.maximum(m, mx)
        p = jnp.exp2(s - mn).astype(BF16)
        return mn, jnp.exp2(m - mn) * acc + jnp.dot(va, p, preferred_element_type=F32)

    maxima = tuple(produce(0, h) for h in range(2))
    kb_own = k_ref[pl.ds(pl.multiple_of(i * BLK, BLK), BLK), :]
    vt_own = vt_ref[:, pl.ds(pl.multiple_of(i * BLK, BLK), BLK)]
    init = (jnp.full((1, BLK), NEG, F32), jnp.zeros((MOBA_HD + 16, BLK), F32))
    state = []
    for h in range(2):
        s_own = jnp.where(causal, lax.dot_general(kb_own, qs[h], NT, preferred_element_type=F32), NEG)
        va = jnp.concatenate([vt_own[h * MOBA_HD:(h + 1) * MOBA_HD, :], ones(BLK)], axis=0)
        state.append(fold(init, jnp.max(s_own, axis=0, keepdims=True), s_own, va))

    def body(t, carry):
        state, maxima = carry
        vt = vt_ref[:, pl.ds(pl.multiple_of(t * kt, kt), kt)]
        new_state, new_maxima = [], []
        for h in range(2):
            m, acc = state[h]
            mn = jnp.maximum(m, maxima[h])
            p = jnp.exp2(s_ref[h] - mn).astype(BF16)
            new_maxima.append(produce(t + 1, h))
            va = jnp.concatenate([vt[h * MOBA_HD:(h + 1) * MOBA_HD, :], ones(kt)], axis=0)
            new_state.append((mn, jnp.exp2(m - mn) * acc + jnp.dot(va, p, preferred_element_type=F32)))
        return tuple(new_state), tuple(new_maxima)

    n_groups = (i + kg - 1) // kg
    carry, _ = lax.fori_loop(0, n_groups, body, (tuple(state), maxima))
    for h in range(2):
        _, acc = carry[h]
        o_ref[h * MOBA_HD:(h + 1) * MOBA_HD, :] = (
            acc[0:MOBA_HD, :] / acc[MOBA_HD:MOBA_HD + 1, :]).astype(BF16)


def _moba(proj, vt, kmean):
    s = proj.shape[0]
    nb = s // BLK
    assert nb % min(MOBA_KG, nb) == 0
    per = D // LANE
    return pl.pallas_call(
        _moba_kernel,
        grid=(MOBA_HEADS // 2, nb),
        in_specs=[pl.BlockSpec((BLK, LANE), lambda hp, i: (i, C_Q * per + hp)),
                  pl.BlockSpec((s, LANE), lambda hp, i: (0, C_K * per + hp)),
                  pl.BlockSpec((LANE, s), lambda hp, i: (hp, 0)),
                  pl.BlockSpec((nb, LANE), lambda hp, i: (0, hp))],
        out_specs=pl.BlockSpec((LANE, BLK), lambda hp, i: (hp, i)),
        out_shape=jax.ShapeDtypeStruct((D, s), BF16),
        scratch_shapes=[pltpu.VMEM((2, nb // min(MOBA_KG, nb), SUBLANE, BLK), F32),
                        pltpu.VMEM((2, min(MOBA_KG, nb) * BLK, BLK), F32)],
        compiler_params=_cparams(2),
        name="moba",
    )(proj, proj, vt, kmean)


def _merge_kernel(final, x_ref, ya_ref, ys_ref, u_ref, sg_ref, att_ref, mg_ref,
                  g0_ref, g1_ref, g2_ref, gb_ref, s5d_ref, glub_ref, fnw_ref,
                  gluw_ref, wa_ref, wb_ref, wc_ref, wo_ref, out_ref):
    mm = lambda a, w_ref: jnp.dot(a.astype(BF16), w_ref[...], preferred_element_type=F32)
    f32 = lambda r: r[...].astype(F32)
    yb = f32(ys_ref) + s5d_ref[...] * f32(u_ref)
    yb = jax.nn.gelu(yb)
    yb = yb * _sigmoid(mm(yb, gluw_ref) + glub_ref[...])
    sg = f32(sg_ref)
    yb = yb * (sg * _sigmoid(sg))
    mg = f32(mg_ref)
    att = f32(att_ref).T * (mg * _sigmoid(mg))
    merged = (_sigmoid(f32(g0_ref) + gb_ref[:, 0:D]) * mm(ya_ref[...], wa_ref)
              + _sigmoid(f32(g1_ref) + gb_ref[:, D:2 * D]) * mm(yb, wb_ref)
              + _sigmoid(f32(g2_ref) + gb_ref[:, 2 * D:3 * D]) * mm(att, wc_ref))
    xn = x_ref[...] + mm(merged, wo_ref)
    if final:
        xn = xn * lax.rsqrt(jnp.mean(xn * xn, axis=-1, keepdims=True) + EPS) * fnw_ref[...]
    out_ref[...] = xn


def _merge(final, x, ya, ys, proj, att_t, gate_b, s5d, glub, fnw, gluw, wa, wb, wc, wo, tm):
    s = x.shape[0]
    rowb = pl.BlockSpec((tm, D), lambda i: (i, 0))
    pcol = lambda c: pl.BlockSpec((tm, D), lambda i, c=c: (i, c))
    wspec = pl.BlockSpec((D, D), lambda i: (0, 0), pipeline_mode=pl.Buffered(1))
    return pl.pallas_call(
        functools.partial(_merge_kernel, final),
        grid=(s // tm,),
        in_specs=[rowb, rowb, rowb, pcol(C_S5U), pcol(C_S5G),
                  pl.BlockSpec((D, tm), lambda i: (0, i)), pcol(C_MG),
                  pcol(C_GL), pcol(C_GL + 1), pcol(C_GL + 2),
                  _const_spec((1, 3 * D)), _const_spec((1, D)), _const_spec((1, D)),
                  _const_spec((1, D)), wspec, wspec, wspec, wspec, wspec],
        out_specs=rowb,
        out_shape=jax.ShapeDtypeStruct((s, D), F32),
        compiler_params=_cparams(1),
        name="merge",
    )(x, ya, ys, proj, proj, att_t, proj, proj, proj, proj,
      gate_b, s5d, glub, fnw, gluw, wa, wb, wc, wo)


def _block_diag_pairs(a):
    g, r, c = a.shape
    a2 = a.reshape(g // 2, 2, r, c)
    out = jnp.einsum("pgrc,gh->pgrhc", a2, jnp.eye(2, dtype=a.dtype))
    return out.reshape(g // 2, 2 * r, 2 * c)


def _lane_scatter():
    half = S5_OCT // 2
    sel = np.zeros((half, 2, S5_OCT), np.float32)
    for pr in range(half):
        for gi in range(2):
            sel[pr, gi, 2 * pr + gi] = 1.0
    eye_t, eye_c = np.eye(S5_L, dtype=np.float32), np.eye(S5_CH, dtype=np.float32)
    sc = np.einsum("pgh,tu,cd->pgtcuhd", sel, eye_t, eye_c)
    return jnp.asarray(sc.reshape(half, 2 * S5_L * S5_CH, S5_L * LANE), dtype=BF16)


def _row_tile(s, want):
    t = min(want, s)
    while s % t:
        t //= 2
    return t


def kernel(x, norm_w, w_in, gate_b, conv_w, conv_b, dt_bias, a_log, ssd_d, ssd_norm_w, w_proj_a,
           lambda_re, lambda_im, log_step, s5_b_re, s5_b_im, s5_c_re, s5_c_im, s5_d, glu_w, glu_b,
           w_proj_b, w_proj_c, w_out, final_norm_w):
    b, s, _ = x.shape
    assert b == 1 and s % BLK == 0 and x.shape[2] == D
    depth = norm_w.shape[0]
    nb = s // BLK
    nc = s // S5_L
    xc = x.reshape(s, D)
    o_dt = 3 * D
    o_s5u = o_dt + SSD_HEADS
    o_q = o_s5u + 2 * D
    o_v = o_q + 2 * D
    o_mg = o_v + D
    expand = (jnp.arange(LANE)[:, None] == (jnp.arange(D)[None, :] // SSD_HEAD_DIM)).astype(F32)
    pad16 = lambda a: jnp.pad(a.astype(F32), (0, LANE - SSD_HEADS)).reshape(1, LANE)
    tile_b = lambda a: jnp.tile(a.transpose(0, 2, 1), (1, S5_L, 1))
    tile_c = lambda a: jnp.tile(a.transpose(0, 2, 1), (1, 1, S5_L))
    scat = _lane_scatter()

    for l in range(depth):
        w = w_in[l]
        w_nat = jnp.concatenate([w[:, 0:o_dt], w[:, o_s5u:o_v], w[:, o_mg:]], axis=1).astype(BF16)
        w_vt = w[:, o_v:o_mg].T.astype(BF16)
        w_dt = jnp.pad(w[:, o_dt:o_s5u], ((0, 0), (0, LANE - SSD_HEADS)))
        proj, vt, dt, dtt = _project(xc, norm_w[l].reshape(1, D), w_nat, w_vt, w_dt,
                                     _row_tile(s, 1024))

        ya = _ssd(proj, dt, dtt, conv_w[l], conv_b[l].reshape(1, 2 * D),
                  pad16(dt_bias[l]), dt_bias[l].reshape(SSD_HEADS, 1),
                  pad16(a_log[l]), a_log[l].reshape(SSD_HEADS, 1),
                  jnp.repeat(ssd_d[l], SSD_HEAD_DIM).reshape(1, D),
                  ssd_norm_w[l].reshape(1, D), expand)

        ls_full = jnp.repeat(log_step[l], S5_STATE)
        bd, bxr8, bxi8, cxr8, cxi8, alr, ali = _s5_prep(
            lambda_re[l].reshape(S5_PAIRS, 1, LANE), lambda_im[l].reshape(S5_PAIRS, 1, LANE),
            ls_full.reshape(S5_PAIRS, 1, LANE),
            lambda_re[l].reshape(S5_PAIRS, LANE, 1), lambda_im[l].reshape(S5_PAIRS, LANE, 1),
            ls_full.reshape(S5_PAIRS, LANE, 1),
            _block_diag_pairs(tile_b(s5_b_re[l])), _block_diag_pairs(tile_b(s5_b_im[l])),
            _block_diag_pairs(tile_c(s5_c_re[l])), _block_diag_pairs(tile_c(s5_c_im[l])), scat)
        n_oct = S5_GROUPS // S5_OCT
        u8 = (proj[:, C_S5U * D:(C_S5U + 1) * D].reshape(nc, S5_L, n_oct, LANE)
              .transpose(2, 0, 1, 3).reshape(n_oct, nc, S5_L * LANE))
        sr, si = _s5_local(u8, bxr8, bxi8)
        pr, pi = _s5_scan(sr, si, alr.reshape(1, -1), ali.reshape(1, -1), 512)
        y8 = _s5_out(u8, bd, pr, pi, cxr8, cxi8)
        ys = y8.reshape(n_oct, nc, S5_L, LANE).transpose(1, 2, 0, 3).reshape(s, D)

        km = _kmean(proj, math.gcd(nb, 8))
        att_t = _moba(proj, vt, km)

        xc = _merge(l == depth - 1, xc, ya, ys, proj, att_t, gate_b[l].reshape(1, 3 * D),
                    s5_d[l].reshape(1, D), glu_b[l].reshape(1, D), final_norm_w.reshape(1, D),
                    glu_w[l].astype(BF16), w_proj_a[l].astype(BF16), w_proj_b[l].astype(BF16),
                    w_proj_c[l].astype(BF16), w_out[l].astype(BF16), _row_tile(s, 512))
    return xc.reshape(b, s, D)
```

```python
import functools
import math

import jax
import jax.numpy as jnp
import numpy as np
from jax import lax
from jax.experimental import pallas as pl
from jax.experimental.pallas import tpu as pltpu

F32 = jnp.float32
BF16 = jnp.bfloat16
HI = lax.Precision.HIGHEST

D = 1024
EPS = 1e-6
BLK = 256
SSD_HEADS = 16
SSD_HEAD_DIM = 64
SSD_GROUPS = 4
SSD_STATE = 128
SSD_CONV = 4
S5_GROUPS = 64
S5_CH = 16
S5_STATE = 64
S5_CLIP = 1e-4
S5_L = 16
S5_PAIRS = S5_GROUPS // 2
S5_OCT = 8
MOBA_HEADS = 16
MOBA_HD = 64
MOBA_TOPK = 3
MOBA_KG = 4
NEG = -1e30
LANE = 128
SUBLANE = 8
VMEM_LIMIT = 56 * 1024 * 1024

C_Z, C_XS, C_BC, C_S5U, C_S5G, C_Q, C_K, C_MG, C_GL = 0, 1, 2, 3, 4, 5, 6, 7, 8
N_NAT = 11 * D

NT = (((1,), (1,)), ((), ()))
TN = (((0,), (0,)), ((), ()))


def _cparams(n_axes):
    return pltpu.CompilerParams(dimension_semantics=("arbitrary",) * n_axes,
                                vmem_limit_bytes=VMEM_LIMIT)


def _const_spec(shape):
    nd = len(shape)
    return pl.BlockSpec(shape, lambda *_: (0,) * nd)


def _proj_kernel(x_ref, nw_ref, w_ref, wvt_ref, wdt_ref,
                 proj_ref, vt_ref, dt_ref, dtt_ref, h_ref):
    @pl.when(pl.program_id(1) == 0)
    def _():
        x = x_ref[...]
        h = x * lax.rsqrt(jnp.mean(x * x, axis=-1, keepdims=True) + EPS) * nw_ref[...]
        hb = h.astype(BF16)
        h_ref[...] = hb
        vt = lax.dot_general(wvt_ref[...], hb, NT, preferred_element_type=F32)
        vt_ref[...] = vt.astype(BF16)
        dt_ref[...] = jnp.dot(h, wdt_ref[...], precision=HI, preferred_element_type=F32)
        dtt_ref[...] = dt_ref[...].T[0:SSD_HEADS, :]

    proj_ref[...] = jnp.dot(h_ref[...], w_ref[...], preferred_element_type=F32).astype(BF16)


def _project(x, norm_w, w_nat, w_vt, w_dt, layer, tm):
    s = x.shape[0]
    grid = (s // tm, N_NAT // D)
    return pl.pallas_call(
        _proj_kernel,
        grid=grid,
        in_specs=[
            pl.BlockSpec((tm, D), lambda i, j: (i, 0)),
            _const_spec((1, D)),
            pl.BlockSpec((None, D, D), lambda i, j: (layer, 0, j)),
            pl.BlockSpec((None, D, D), lambda i, j: (layer, 0, 0)),
            pl.BlockSpec((None, D, LANE), lambda i, j: (layer, 0, 0)),
        ],
        out_specs=[
            pl.BlockSpec((tm, D), lambda i, j: (i, j)),
            pl.BlockSpec((D, tm), lambda i, j: (0, i)),
            pl.BlockSpec((tm, LANE), lambda i, j: (i, 0)),
            pl.BlockSpec((SSD_HEADS, tm), lambda i, j: (0, i)),
        ],
        out_shape=[
            jax.ShapeDtypeStruct((s, N_NAT), BF16),
            jax.ShapeDtypeStruct((D, s), BF16),
            jax.ShapeDtypeStruct((s, LANE), F32),
            jax.ShapeDtypeStruct((SSD_HEADS, s), F32),
        ],
        scratch_shapes=[pltpu.VMEM((tm, D), BF16)],
        compiler_params=_cparams(2),
        name="proj",
    )(x, norm_w, w_nat, w_vt, w_dt)


def _sigmoid(v):
    return 0.5 * jnp.tanh(0.5 * v) + 0.5


def _split3(v):
    hi = v.astype(BF16)
    r1 = v - hi.astype(F32)
    mid = r1.astype(BF16)
    lo = (r1 - mid.astype(F32)).astype(BF16)
    return hi, mid, lo


def _dot_exact_lhs(a_exact, v):
    ab = a_exact.astype(BF16)
    return sum(jnp.dot(ab, t, preferred_element_type=F32) for t in _split3(v))


def _dot_exact_rhs(v, b_exact):
    bb = b_exact.astype(BF16)
    return sum(jnp.dot(t, bb, preferred_element_type=F32) for t in _split3(v))


def _softplus(v):
    return jnp.maximum(v, 0.0) + jnp.log1p(jnp.exp(-jnp.abs(v)))


def _ssd_kernel(z_ref, xs_ref, bc_ref, dt_ref, dtt_ref, cw_ref, cb_ref, dtb_ref, dtbt_ref,
                al_ref, alt_ref, dsk_ref, nw_ref, e_ref, out_ref, ext_ref, st_ref):
    @pl.when(pl.program_id(0) == 0)
    def _():
        ext_ref[0:8, :] = jnp.zeros((8, 2 * D), F32)
        st_ref[...] = jnp.zeros(st_ref.shape, F32)

    ext_ref[8:8 + BLK, 0:D] = xs_ref[...].astype(F32)
    ext_ref[8:8 + BLK, D:2 * D] = bc_ref[...].astype(F32)
    acc = cb_ref[...] + cw_ref[0:1, :] * ext_ref[5:5 + BLK, :]
    for i in range(1, SSD_CONV):
        acc = acc + cw_ref[i:i + 1, :] * ext_ref[5 + i:5 + i + BLK, :]
    ext_ref[0:8, :] = ext_ref[BLK:BLK + 8, :]
    act = acc * _sigmoid(acc)
    xs = act[:, 0:D]

    dt = _softplus(dt_ref[...] + dtb_ref[...])
    da = dt * (-jnp.exp(al_ref[...]))
    dtt = _softplus(dtt_ref[...] + dtbt_ref[...])
    dat = dtt * (-jnp.exp(alt_ref[...]))
    row = lax.broadcasted_iota(jnp.int32, (BLK, BLK), 0)
    col = lax.broadcasted_iota(jnp.int32, (BLK, BLK), 1)
    lower = row >= col
    tri = lower.astype(F32)
    a_cum = _dot_exact_lhs(tri, da)
    a_cumt = _dot_exact_rhs(dat, (row <= col).astype(F32))
    xdt = xs * _dot_exact_rhs(dt, e_ref[...])
    xdt_b = xdt.astype(BF16)

    lane = lax.broadcasted_iota(jnp.int32, (BLK, LANE), 1)
    first = lane < SSD_HEAD_DIM
    rowp = lax.broadcasted_iota(jnp.int32, (LANE, 1), 0) < SSD_HEAD_DIM
    y_parts = []
    for g in range(SSD_GROUPS):
        b_g = act[:, D + g * SSD_STATE:D + (g + 1) * SSD_STATE]
        c_g = act[:, D + SSD_GROUPS * SSD_STATE + g * SSD_STATE:
                  D + SSD_GROUPS * SSD_STATE + (g + 1) * SSD_STATE]
        b_gb = b_g.astype(BF16)
        c_gb = c_g.astype(BF16)
        cbm = lax.dot_general(c_gb, b_gb, NT, preferred_element_type=F32)
        for pp in range(2):
            pair = 2 * g + pp
            h0 = 2 * pair
            x_pair = xdt_b[:, pair * LANE:(pair + 1) * LANE]
            y_pair = jnp.zeros((BLK, LANE), F32)
            for hh in range(2):
                h = h0 + hh
                lm = jnp.where(lower, jnp.exp(a_cum[:, h:h + 1] - a_cumt[h:h + 1, :]), 0.0)
                w = (cbm * lm).astype(BF16)
                xm = jnp.where(first if hh == 0 else jnp.logical_not(first), x_pair,
                               jnp.zeros_like(x_pair))
                y_pair = y_pair + jnp.dot(w, xm, preferred_element_type=F32)
            ac_pair = jnp.where(first, a_cum[:, h0:h0 + 1], a_cum[:, h0 + 1:h0 + 2])
            al_pair = ac_pair[BLK - 1:BLK, :]
            r_pair = st_ref[pair]
            y_off = lax.dot_general(c_gb, r_pair.astype(BF16), NT, preferred_element_type=F32)
            y_pair = y_pair + y_off * jnp.exp(ac_pair)
            xdec = (xdt[:, pair * LANE:(pair + 1) * LANE] * jnp.exp(al_pair - ac_pair)).astype(BF16)
            st_new = lax.dot_general(xdec, b_gb, TN, preferred_element_type=F32)
            al_col = jnp.where(rowp, a_cumt[h0:h0 + 1, BLK - 1:BLK],
                               a_cumt[h0 + 1:h0 + 2, BLK - 1:BLK])
            st_ref[pair] = jnp.exp(al_col) * r_pair + st_new
            y_parts.append(y_pair)
    y = jnp.concatenate(y_parts, axis=1) + dsk_ref[...] * xs
    z = z_ref[...].astype(F32)
    y = y * (z * _sigmoid(z))
    gw = D // SSD_GROUPS
    for g in range(SSD_GROUPS):
        yg = y[:, g * gw:(g + 1) * gw]
        yg = yg * lax.rsqrt(jnp.mean(yg * yg, axis=-1, keepdims=True) + EPS)
        out_ref[:, g * gw:(g + 1) * gw] = (yg * nw_ref[:, g * gw:(g + 1) * gw]).astype(BF16)


def _ssd(proj, dt, dtt, cw, cb, dtb, dtbt, al, alt, dsk, nw, expand):
    s = proj.shape[0]
    row = lambda c: pl.BlockSpec((BLK, D), lambda i, c=c: (i, c))
    return pl.pallas_call(
        _ssd_kernel,
        grid=(s // BLK,),
        in_specs=[
            row(C_Z), row(C_XS), row(C_BC),
            pl.BlockSpec((BLK, LANE), lambda i: (i, 0)),
            pl.BlockSpec((SSD_HEADS, BLK), lambda i: (0, i)),
            _const_spec((SSD_CONV, 2 * D)), _const_spec((1, 2 * D)),
            _const_spec((1, LANE)), _const_spec((SSD_HEADS, 1)),
            _const_spec((1, LANE)), _const_spec((SSD_HEADS, 1)),
            _const_spec((1, D)), _const_spec((1, D)), _const_spec((LANE, D)),
        ],
        out_specs=pl.BlockSpec((BLK, D), lambda i: (i, 0)),
        out_shape=jax.ShapeDtypeStruct((s, D), BF16),
        scratch_shapes=[pltpu.VMEM((BLK + 8, 2 * D), F32),
                        pltpu.VMEM((SSD_HEADS // 2, LANE, SSD_STATE), F32)],
        compiler_params=_cparams(1),
        name="ssd",
    )(proj, proj, proj, dt, dtt, cw, cb, dtb, dtbt, al, alt, dsk, nw, expand)


S5_LAGPAD = 3


def _s5_prep_kernel(lrr_ref, lir_ref, lsr_ref, lrc_ref, lic_ref, lsc_ref,
                    bre_ref, bim_ref, cre_ref, cim_ref, scat_ref,
                    bd_ref, bxr_ref, bxi_ref, cxr_ref, cxi_ref, alr_ref, ali_ref):
    n = S5_L * S5_CH
    rows = 2 * n
    cols = 2 * n
    bd_ref[0, :, 0:S5_LAGPAD * LANE] = jnp.zeros((LANE, S5_LAGPAD * LANE), BF16)
    bxr_ref[...] = jnp.zeros(bxr_ref.shape, BF16)
    bxi_ref[...] = jnp.zeros(bxi_ref.shape, BF16)
    lane128 = lax.broadcasted_iota(jnp.int32, (S5_CH, LANE), 1) // S5_CH
    rk = lax.broadcasted_iota(jnp.int32, (rows, 1), 0)
    ek = (S5_L - 1 - (rk % n) // S5_CH).astype(F32)
    ct = lax.broadcasted_iota(jnp.int32, (1, cols), 1)
    et = ((ct % n) // S5_CH + 1).astype(F32)
    for pr in range(S5_OCT // 2):
        lr = jnp.minimum(lrr_ref[pr], -S5_CLIP)
        li = lir_ref[pr]
        st = jnp.exp(lsr_ref[pr])
        lrs, lis = lr * st, li * st
        mag = jnp.exp(lrs)
        abr, abi = mag * jnp.cos(lis), mag * jnp.sin(lis)
        nr, ni = abr - 1.0, abi
        den = lr * lr + li * li
        cfr = (nr * lr + ni * li) / den
        cfi = (ni * lr - nr * li) / den
        bre, bim = bre_ref[pr], bim_ref[pr]
        bbr = cfr * bre - cfi * bim
        bbi = cfr * bim + cfi * bre
        pm = jnp.exp(ek * lrs)
        pbr, pbi = pm * jnp.cos(ek * lis), pm * jnp.sin(ek * lis)
        bxr = pbr * bbr - pbi * bbi
        bxi = pbr * bbi + pbi * bbr
        alm = jnp.exp(float(S5_L) * lrs)
        alr_ref[pr] = alm * jnp.cos(float(S5_L) * lis)
        ali_ref[pr] = alm * jnp.sin(float(S5_L) * lis)
        lrc = jnp.minimum(lrc_ref[pr], -S5_CLIP)
        stc = jnp.exp(lsc_ref[pr])
        lrsc, lisc = lrc * stc, lic_ref[pr] * stc
        qm = jnp.exp(et * lrsc)
        qr, qi = qm * jnp.cos(et * lisc), qm * jnp.sin(et * lisc)
        cre, cim = cre_ref[pr], cim_ref[pr]
        rsl = slice(pr * LANE, (pr + 1) * LANE)
        cxr_ref[0, rsl, :] = jnp.dot((cre * qr - cim * qi).astype(BF16), scat_ref[pr],
                                     preferred_element_type=F32).astype(BF16)
        cxi_ref[0, rsl, :] = jnp.dot((-(cre * qi + cim * qr)).astype(BF16), scat_ref[pr],
                                     preferred_element_type=F32).astype(BF16)
        rw = (jnp.dot(bxr, cre, precision=HI, preferred_element_type=F32)
              - jnp.dot(bxi, cim, precision=HI, preferred_element_type=F32))
        bxr_b, bxi_b = bxr.astype(BF16), bxi.astype(BF16)
        for gi in range(2):
            g8 = 2 * pr + gi
            for k in range(S5_L):
                src = slice(gi * n + k * S5_CH, gi * n + (k + 1) * S5_CH)
                dst = slice(k * LANE + g8 * S5_CH, k * LANE + (g8 + 1) * S5_CH)
                bxr_ref[0, dst, rsl] = bxr_b[src, :]
                bxi_ref[0, dst, rsl] = bxi_b[src, :]
            for j in range(S5_L):
                m = S5_L - 1 - j
                blk = rw[gi * n + m * S5_CH:gi * n + (m + 1) * S5_CH, gi * n:gi * n + LANE]
                bd_ref[0, g8 * S5_CH:(g8 + 1) * S5_CH,
                       (S5_LAGPAD + j) * LANE:(S5_LAGPAD + j + 1) * LANE] = jnp.where(
                           lane128 == g8, blk, 0.0).astype(BF16)


def _s5_prep(lrr, lir, lsr, lrc, lic, lsc, bre2, bim2, cre2, cim2, scat, layer):
    n = S5_L * S5_CH
    half = S5_OCT // 2
    n_oct = S5_GROUPS // S5_OCT
    w8 = S5_L * LANE
    ow = S5_OCT * S5_STATE
    blk4 = lambda a, b: pl.BlockSpec((half, a, b), lambda i: (layer * n_oct + i, 0, 0))
    out4 = lambda a, b: pl.BlockSpec((half, a, b), lambda i: (i, 0, 0))
    oct3 = lambda a, b: pl.BlockSpec((1, a, b), lambda i: (i, 0, 0))
    return pl.pallas_call(
        _s5_prep_kernel,
        grid=(n_oct,),
        in_specs=[blk4(1, LANE)] * 3 + [blk4(LANE, 1)] * 3
                 + [blk4(2 * n, LANE)] * 2 + [blk4(LANE, 2 * n)] * 2
                 + [_const_spec((half, 2 * n, w8))],
        out_specs=[oct3(LANE, (S5_LAGPAD + S5_L) * LANE),
                   oct3(w8, ow), oct3(w8, ow), oct3(ow, w8), oct3(ow, w8),
                   out4(1, LANE), out4(1, LANE)],
        out_shape=[jax.ShapeDtypeStruct((n_oct, LANE, (S5_LAGPAD + S5_L) * LANE), BF16),
                   jax.ShapeDtypeStruct((n_oct, w8, ow), BF16),
                   jax.ShapeDtypeStruct((n_oct, w8, ow), BF16),
                   jax.ShapeDtypeStruct((n_oct, ow, w8), BF16),
                   jax.ShapeDtypeStruct((n_oct, ow, w8), BF16),
                   jax.ShapeDtypeStruct((S5_PAIRS, 1, LANE), F32),
                   jax.ShapeDtypeStruct((S5_PAIRS, 1, LANE), F32)],
        compiler_params=_cparams(1),
        name="s5_prep",
    )(lrr, lir, lsr, lrc, lic, lsc, bre2, bim2, cre2, cim2, scat)


def _s5_local_kernel(u_ref, bxr_ref, bxi_ref, sr_ref, si_ref):
    sr_ref[...] = jnp.dot(u_ref[0], bxr_ref[0], preferred_element_type=F32)
    si_ref[...] = jnp.dot(u_ref[0], bxi_ref[0], preferred_element_type=F32)


def _s5_local(u8, bxr8, bxi8):
    nc, w = u8.shape[1], u8.shape[2]
    ow = S5_OCT * S5_STATE
    return pl.pallas_call(
        _s5_local_kernel,
        grid=(S5_GROUPS // S5_OCT,),
        in_specs=[pl.BlockSpec((1, nc, w), lambda i: (i, 0, 0)),
                  pl.BlockSpec((1, w, ow), lambda i: (i, 0, 0)),
                  pl.BlockSpec((1, w, ow), lambda i: (i, 0, 0))],
        out_specs=[pl.BlockSpec((nc, ow), lambda i: (0, i))] * 2,
        out_shape=[jax.ShapeDtypeStruct((nc, S5_GROUPS * S5_STATE), F32)] * 2,
        compiler_params=_cparams(1),
        name="s5_local",
    )(u8, bxr8, bxi8)


def _s5_scan_kernel(xr_ref, xi_ref, ar_ref, ai_ref, pr_ref, pi_ref):
    ar, ai = ar_ref[...], ai_ref[...]
    w = ar.shape[1]

    def body(r, carry):
        sr, si = carry
        pr_ref[pl.ds(r, 1), :] = sr
        pi_ref[pl.ds(r, 1), :] = si
        xr = xr_ref[pl.ds(r, 1), :]
        xi = xi_ref[pl.ds(r, 1), :]
        return ar * sr - ai * si + xr, ar * si + ai * sr + xi

    lax.fori_loop(0, xr_ref.shape[0], body, (jnp.zeros((1, w), F32), jnp.zeros((1, w), F32)))


def _s5_scan(xr, xi, ar, ai, tw):
    nc, width = xr.shape
    col = pl.BlockSpec((nc, tw), lambda i: (0, i))
    vec = pl.BlockSpec((1, tw), lambda i: (0, i))
    return pl.pallas_call(
        _s5_scan_kernel,
        grid=(width // tw,),
        in_specs=[col, col, vec, vec],
        out_specs=[col, col],
        out_shape=[jax.ShapeDtypeStruct((nc, width), F32)] * 2,
        compiler_params=_cparams(1),
        name="s5_scan",
    )(xr, xi, ar, ai)


def _s5_out_kernel(u_ref, bd_ref, pr_ref, pi_ref, cxr_ref, cxi_ref, y_ref):
    tt = S5_LAGPAD + 1
    prb, pib = pr_ref[...].astype(BF16), pi_ref[...].astype(BF16)
    for jt in range(S5_L // tt):
        osl = slice(jt * tt * LANE, (jt + 1) * tt * LANE)
        acc = (jnp.dot(prb, cxr_ref[0, :, osl], preferred_element_type=F32)
               + jnp.dot(pib, cxi_ref[0, :, osl], preferred_element_type=F32))
        for k in range((jt + 1) * tt):
            lag0 = jt * tt - k + S5_LAGPAD
            acc = acc + jnp.dot(u_ref[0, :, k * LANE:(k + 1) * LANE],
                                bd_ref[0, :, lag0 * LANE:(lag0 + tt) * LANE],
                                preferred_element_type=F32)
        y_ref[0, :, osl] = acc.astype(BF16)


def _s5_out(u8, bd, pr, pi, cxr8, cxi8):
    n_oct, nc, w = u8.shape
    ow = S5_OCT * S5_STATE
    return pl.pallas_call(
        _s5_out_kernel,
        grid=(n_oct,),
        in_specs=[pl.BlockSpec((1, nc, w), lambda i: (i, 0, 0)),
                  pl.BlockSpec((1, LANE, bd.shape[2]), lambda i: (i, 0, 0)),
                  pl.BlockSpec((nc, ow), lambda i: (0, i)),
                  pl.BlockSpec((nc, ow), lambda i: (0, i)),
                  pl.BlockSpec((1, ow, w), lambda i: (i, 0, 0)),
                  pl.BlockSpec((1, ow, w), lambda i: (i, 0, 0))],
        out_specs=pl.BlockSpec((1, nc, w), lambda i: (i, 0, 0)),
        out_shape=jax.ShapeDtypeStruct((n_oct, nc, w), BF16),
        compiler_params=_cparams(1),
        name="s5_out",
    )(u8, bd, pr, pi, cxr8, cxi8)


def _kmean_kernel(k_ref, o_ref):
    for r in range(o_ref.shape[0]):
        o_ref[r:r + 1, :] = jnp.mean(k_ref[r * BLK:(r + 1) * BLK, :].astype(F32), axis=0, keepdims=True)


def _kmean(proj, rows):
    s = proj.shape[0]
    nb = s // BLK
    return pl.pallas_call(
        _kmean_kernel,
        grid=(nb // rows,),
        in_specs=[pl.BlockSpec((rows * BLK, D), lambda i: (i, C_K))],
        out_specs=pl.BlockSpec((rows, D), lambda i: (i, 0)),
        out_shape=jax.ShapeDtypeStruct((nb, D), F32),
        compiler_params=_cparams(1),
        name="kmean",
    )(proj)


def _moba_kernel(q_ref, k_ref, vt_ref, km_ref, o_ref, bias_ref, s_ref):
    i = pl.program_id(1)
    nb = km_ref.shape[0]
    kg = min(MOBA_KG, nb)
    kt = kg * BLK
    q = q_ref[...].astype(F32)
    lane = lax.broadcasted_iota(jnp.int32, (BLK, LANE), 1)
    blk = lax.broadcasted_iota(jnp.int32, (nb, BLK), 0)
    blk_f = blk.astype(F32)
    scale = MOBA_HD ** -0.5 * math.log2(math.e)
    qs = []
    for h in range(2):
        in_head = (lane >= h * MOBA_HD) & (lane < (h + 1) * MOBA_HD)
        qh = jnp.where(in_head, q, 0.0)
        g = lax.dot_general(km_ref[...], qh, NT, precision=HI, preferred_element_type=F32)
        g = jnp.where(blk < i, g, -jnp.inf)
        sel = jnp.zeros((nb, BLK), jnp.bool_)
        for _ in range(MOBA_TOPK):
            mx = jnp.max(g, axis=0, keepdims=True)
            idx = jnp.min(jnp.where(g == mx, blk_f, float(nb)), axis=0, keepdims=True)
            hit = blk_f == idx
            sel = sel | (hit & (mx > -jnp.inf))
            g = jnp.where(hit, -jnp.inf, g)
        bias = jnp.where(sel, 0.0, NEG)
        for t in range(nb // kg):
            bias_ref[h, t, 0:kg, :] = bias[t * kg:(t + 1) * kg, :]
        qs.append((qh * scale).astype(BF16))

    krow = lax.broadcasted_iota(jnp.int32, (BLK, BLK), 0)
    qcol = lax.broadcasted_iota(jnp.int32, (BLK, BLK), 1)
    causal = krow <= qcol
    last_g = nb // kg - 1
    ones = lambda n: jnp.ones((16, n), BF16)

    last_pair = nb // kg // 2 - 1

    def produce(tp, h):
        tcp = jnp.minimum(tp, last_pair)
        kb = k_ref[pl.ds(pl.multiple_of(tcp * 2 * kt, 2 * kt), 2 * kt), :]
        s_t = lax.dot_general(kb, qs[h], NT, preferred_element_type=F32)
        mx = None
        for g in range(2):
            bias = bias_ref[h, 2 * tcp + g, 0:kg, :]
            for u in range(kg):
                rows = slice((g * kg + u) * BLK, (g * kg + u + 1) * BLK)
                su = s_t[rows, :] + bias[u:u + 1, :]
                s_ref[h, rows, :] = su
                mu = jnp.max(su, axis=0, keepdims=True)
                mx = mu if mx is None else jnp.maximum(mx, mu)
        return mx

    def fold(carry, mx, s, va):
        m, acc = carry
        mn = jnp.maximum(m, mx)
        p = jnp.exp2(s - mn).astype(BF16)
        return mn, jnp.exp2(m - mn) * acc + jnp.dot(va, p, preferred_element_type=F32)

    maxima = tuple(produce(0, h) for h in range(2))
    kb_own = k_ref[pl.ds(pl.multiple_of(i * BLK, BLK), BLK), :]
    vt_own = vt_ref[:, pl.ds(pl.multiple_of(i * BLK, BLK), BLK)]
    init = (jnp.full((1, BLK), NEG, F32), jnp.zeros((MOBA_HD + 16, BLK), F32))
    state = []
    for h in range(2):
        s_own = jnp.where(causal, lax.dot_general(kb_own, qs[h], NT, preferred_element_type=F32), NEG)
        va = jnp.concatenate([vt_own[h * MOBA_HD:(h + 1) * MOBA_HD, :], ones(BLK)], axis=0)
        state.append(fold(init, jnp.max(s_own, axis=0, keepdims=True), s_own, va))

    def step(tp, carry, width, look_ahead):
        state, maxima = carry
        tcp = jnp.minimum(tp, last_pair)
        vt = vt_ref[:, pl.ds(pl.multiple_of(tcp * 2 * kt, 2 * kt), width)]
        new_state, new_maxima = [], []
        for h in range(2):
            m, acc = state[h]
            mn = jnp.maximum(m, maxima[h])
            p = jnp.exp2(s_ref[h, 0:width, :] - mn).astype(BF16)
            new_maxima.append(produce(tp + 1, h) if look_ahead else maxima[h])
            va = jnp.concatenate([vt[h * MOBA_HD:(h + 1) * MOBA_HD, :], ones(width)], axis=0)
            new_state.append((mn, jnp.exp2(m - mn) * acc + jnp.dot(va, p, preferred_element_type=F32)))
        return tuple(new_state), tuple(new_maxima)

    n_groups = (i + kg - 1) // kg
    carry = lax.fori_loop(0, n_groups // 2, lambda tp, c: step(tp, c, 2 * kt, True),
                          (tuple(state), maxima))
    carry, _ = lax.cond(n_groups % 2 == 1, lambda c: step(n_groups // 2, c, kt, False),
                        lambda c: c, carry)
    for h in range(2):
        _, acc = carry[h]
        o_ref[h * MOBA_HD:(h + 1) * MOBA_HD, :] = (
            acc[0:MOBA_HD, :] / acc[MOBA_HD:MOBA_HD + 1, :]).astype(BF16)


def _moba(proj, vt, kmean):
    s = proj.shape[0]
    nb = s // BLK
    assert nb % (2 * min(MOBA_KG, nb)) == 0
    per = D // LANE
    return pl.pallas_call(
        _moba_kernel,
        grid=(MOBA_HEADS // 2, nb),
        in_specs=[pl.BlockSpec((BLK, LANE), lambda hp, i: (i, C_Q * per + hp)),
                  pl.BlockSpec((s, LANE), lambda hp, i: (0, C_K * per + hp)),
                  pl.BlockSpec((LANE, s), lambda hp, i: (hp, 0)),
                  pl.BlockSpec((nb, LANE), lambda hp, i: (0, hp))],
        out_specs=pl.BlockSpec((LANE, BLK), lambda hp, i: (hp, i)),
        out_shape=jax.ShapeDtypeStruct((D, s), BF16),
        scratch_shapes=[pltpu.VMEM((2, nb // min(MOBA_KG, nb), SUBLANE, BLK), F32),
                        pltpu.VMEM((2, 2 * min(MOBA_KG, nb) * BLK, BLK), F32)],
        compiler_params=_cparams(2),
        name="moba",
    )(proj, proj, vt, kmean)


def _merge_kernel(final, x_ref, ya_ref, ys_ref, u_ref, sg_ref, att_ref, mg_ref,
                  g0_ref, g1_ref, g2_ref, gb_ref, s5d_ref, glub_ref, fnw_ref,
                  gluw_ref, wa_ref, wb_ref, wc_ref, wo_ref, out_ref):
    mm = lambda a, w_ref: jnp.dot(a.astype(BF16), w_ref[...], preferred_element_type=F32)
    f32 = lambda r: r[...].astype(F32)
    yb = f32(ys_ref) + s5d_ref[...] * f32(u_ref)
    yb = jax.nn.gelu(yb)
    yb = yb * _sigmoid(mm(yb, gluw_ref) + glub_ref[...])
    sg = f32(sg_ref)
    yb = yb * (sg * _sigmoid(sg))
    mg = f32(mg_ref)
    att = f32(att_ref).T * (mg * _sigmoid(mg))
    merged = (_sigmoid(f32(g0_ref) + gb_ref[:, 0:D]) * mm(ya_ref[...], wa_ref)
              + _sigmoid(f32(g1_ref) + gb_ref[:, D:2 * D]) * mm(yb, wb_ref)
              + _sigmoid(f32(g2_ref) + gb_ref[:, 2 * D:3 * D]) * mm(att, wc_ref))
    xn = x_ref[...] + mm(merged, wo_ref)
    if final:
        xn = xn * lax.rsqrt(jnp.mean(xn * xn, axis=-1, keepdims=True) + EPS) * fnw_ref[...]
    out_ref[...] = xn


def _merge(final, x, ya, ys, proj, att_t, gate_b, s5d, glub, fnw, gluw, wa, wb, wc, wo, layer, tm):
    s = x.shape[0]
    rowb = pl.BlockSpec((tm, D), lambda i: (i, 0))
    pcol = lambda c: pl.BlockSpec((tm, D), lambda i, c=c: (i, c))
    wspec = pl.BlockSpec((None, D, D), lambda i: (layer, 0, 0), pipeline_mode=pl.Buffered(1))
    return pl.pallas_call(
        functools.partial(_merge_kernel, final),
        grid=(s // tm,),
        in_specs=[rowb, rowb, rowb, pcol(C_S5U), pcol(C_S5G),
                  pl.BlockSpec((D, tm), lambda i: (0, i)), pcol(C_MG),
                  pcol(C_GL), pcol(C_GL + 1), pcol(C_GL + 2),
                  _const_spec((1, 3 * D)), _const_spec((1, D)), _const_spec((1, D)),
                  _const_spec((1, D)), wspec, wspec, wspec, wspec, wspec],
        out_specs=rowb,
        out_shape=jax.ShapeDtypeStruct((s, D), F32),
        compiler_params=_cparams(1),
        name="merge",
    )(x, ya, ys, proj, proj, att_t, proj, proj, proj, proj,
      gate_b, s5d, glub, fnw, gluw, wa, wb, wc, wo)


def _block_diag_pairs(a):
    g, r, c = a.shape
    a2 = a.reshape(g // 2, 2, r, c)
    out = jnp.einsum("pgrc,gh->pgrhc", a2, jnp.eye(2, dtype=a.dtype))
    return out.reshape(g // 2, 2 * r, 2 * c)


def _lane_scatter():
    half = S5_OCT // 2
    sel = np.zeros((half, 2, S5_OCT), np.float32)
    for pr in range(half):
        for gi in range(2):
            sel[pr, gi, 2 * pr + gi] = 1.0
    eye_t, eye_c = np.eye(S5_L, dtype=np.float32), np.eye(S5_CH, dtype=np.float32)
    sc = np.einsum("pgh,tu,cd->pgtcuhd", sel, eye_t, eye_c)
    return jnp.asarray(sc.reshape(half, 2 * S5_L * S5_CH, S5_L * LANE), dtype=BF16)


def _row_tile(s, want):
    t = min(want, s)
    while s % t:
        t //= 2
    return t


def kernel(x, norm_w, w_in, gate_b, conv_w, conv_b, dt_bias, a_log, ssd_d, ssd_norm_w, w_proj_a,
           lambda_re, lambda_im, log_step, s5_b_re, s5_b_im, s5_c_re, s5_c_im, s5_d, glu_w, glu_b,
           w_proj_b, w_proj_c, w_out, final_norm_w):
    b, s, _ = x.shape
    assert b == 1 and s % BLK == 0 and x.shape[2] == D
    depth = norm_w.shape[0]
    nb = s // BLK
    nc = s // S5_L
    xc = x.reshape(s, D)
    o_dt = 3 * D
    o_s5u = o_dt + SSD_HEADS
    o_q = o_s5u + 2 * D
    o_v = o_q + 2 * D
    o_mg = o_v + D
    expand = (jnp.arange(LANE)[:, None] == (jnp.arange(D)[None, :] // SSD_HEAD_DIM)).astype(F32)
    pad16 = lambda a: jnp.pad(a.astype(F32), (0, LANE - SSD_HEADS)).reshape(1, LANE)
    tile_b = lambda a: jnp.tile(a.transpose(0, 2, 1), (1, S5_L, 1))
    tile_c = lambda a: jnp.tile(a.transpose(0, 2, 1), (1, 1, S5_L))
    scat = _lane_scatter()
    w_nat = jnp.concatenate([w_in[:, :, 0:o_dt], w_in[:, :, o_s5u:o_v], w_in[:, :, o_mg:]],
                            axis=2).astype(BF16)
    w_vt = jnp.swapaxes(w_in[:, :, o_v:o_mg], 1, 2).astype(BF16)
    w_dt = jnp.pad(w_in[:, :, o_dt:o_s5u], ((0, 0), (0, 0), (0, LANE - SSD_HEADS)))
    merge_w = [a.astype(BF16) for a in (glu_w, w_proj_a, w_proj_b, w_proj_c, w_out)]
    pairs = depth * S5_PAIRS
    ls_full = jnp.repeat(log_step.reshape(-1), S5_STATE)
    s5_rows = [a.reshape(pairs, 1, LANE) for a in (lambda_re, lambda_im, ls_full)]
    s5_cols = [a.reshape(pairs, LANE, 1) for a in (lambda_re, lambda_im, ls_full)]
    s5_mats = ([_block_diag_pairs(tile_b(a.reshape(depth * S5_GROUPS, S5_STATE, S5_CH)))
                for a in (s5_b_re, s5_b_im)]
               + [_block_diag_pairs(tile_c(a.reshape(depth * S5_GROUPS, S5_CH, S5_STATE)))
                  for a in (s5_c_re, s5_c_im)])

    for l in range(depth):
        proj, vt, dt, dtt = _project(xc, norm_w[l].reshape(1, D), w_nat, w_vt, w_dt, l,
                                     _row_tile(s, 1024))

        ya = _ssd(proj, dt, dtt, conv_w[l], conv_b[l].reshape(1, 2 * D),
                  pad16(dt_bias[l]), dt_bias[l].reshape(SSD_HEADS, 1),
                  pad16(a_log[l]), a_log[l].reshape(SSD_HEADS, 1),
                  jnp.repeat(ssd_d[l], SSD_HEAD_DIM).reshape(1, D),
                  ssd_norm_w[l].reshape(1, D), expand)

        bd, bxr8, bxi8, cxr8, cxi8, alr, ali = _s5_prep(*s5_rows, *s5_cols, *s5_mats, scat, l)
        n_oct = S5_GROUPS // S5_OCT
        u8 = (proj[:, C_S5U * D:(C_S5U + 1) * D].reshape(nc, S5_L, n_oct, LANE)
              .transpose(2, 0, 1, 3).reshape(n_oct, nc, S5_L * LANE))
        sr, si = _s5_local(u8, bxr8, bxi8)
        pr, pi = _s5_scan(sr, si, alr.reshape(1, -1), ali.reshape(1, -1), 512)
        y8 = _s5_out(u8, bd, pr, pi, cxr8, cxi8)
        ys = y8.reshape(n_oct, nc, S5_L, LANE).transpose(1, 2, 0, 3).reshape(s, D)

        km = _kmean(proj, math.gcd(nb, 8))
        att_t = _moba(proj, vt, km)

        xc = _merge(l == depth - 1, xc, ya, ys, proj, att_t, gate_b[l].reshape(1, 3 * D),
                    s5_d[l].reshape(1, D), glu_b[l].reshape(1, D), final_norm_w.reshape(1, D),
                    *merge_w, l, _row_tile(s, 512))
    return xc.reshape(b, s, D)
```

```python
import functools
import math

import jax
import jax.numpy as jnp
import numpy as np
from jax import lax
from jax.experimental import pallas as pl
from jax.experimental.pallas import tpu as pltpu

F32 = jnp.float32
BF16 = jnp.bfloat16
HI = lax.Precision.HIGHEST

D = 1024
EPS = 1e-6
BLK = 256
SSD_HEADS = 16
SSD_HEAD_DIM = 64
SSD_GROUPS = 4
SSD_STATE = 128
SSD_CONV = 4
S5_GROUPS = 64
S5_CH = 16
S5_STATE = 64
S5_CLIP = 1e-4
S5_L = 16
S5_PAIRS = S5_GROUPS // 2
S5_OCT = 8
MOBA_HEADS = 16
MOBA_HD = 64
MOBA_TOPK = 3
MOBA_KG = 4
MOBA_TRIP = 2
NEG = -1e30
LANE = 128
SUBLANE = 8
VMEM_LIMIT = 56 * 1024 * 1024

C_Z, C_XS, C_BC, C_S5U, C_S5G, C_Q, C_K, C_MG, C_GL = 0, 1, 2, 3, 4, 5, 6, 7, 8
N_NAT = 11 * D

NT = (((1,), (1,)), ((), ()))
TN = (((0,), (0,)), ((), ()))


def _cparams(n_axes):
    return pltpu.CompilerParams(dimension_semantics=("arbitrary",) * n_axes,
                                vmem_limit_bytes=VMEM_LIMIT)


def _const_spec(shape):
    nd = len(shape)
    return pl.BlockSpec(shape, lambda *_: (0,) * nd)


def _proj_kernel(x_ref, nw_ref, w_ref, wvt_ref, wdt_ref,
                 proj_ref, vt_ref, dt_ref, dtt_ref, h_ref):
    @pl.when(pl.program_id(1) == 0)
    def _():
        x = x_ref[...]
        h = x * lax.rsqrt(jnp.mean(x * x, axis=-1, keepdims=True) + EPS) * nw_ref[...]
        hb = h.astype(BF16)
        h_ref[...] = hb
        vt = lax.dot_general(wvt_ref[...], hb, NT, preferred_element_type=F32)
        vt_ref[...] = vt.astype(BF16)
        dt_ref[...] = jnp.dot(h, wdt_ref[...], precision=HI, preferred_element_type=F32)
        dtt_ref[...] = dt_ref[...].T[0:SSD_HEADS, :]

    proj_ref[...] = jnp.dot(h_ref[...], w_ref[...], preferred_element_type=F32).astype(BF16)


def _project(x, norm_w, w_nat, w_vt, w_dt, layer, tm):
    s = x.shape[0]
    grid = (s // tm, N_NAT // D)
    return pl.pallas_call(
        _proj_kernel,
        grid=grid,
        in_specs=[
            pl.BlockSpec((tm, D), lambda i, j: (i, 0)),
            _const_spec((1, D)),
            pl.BlockSpec((None, D, D), lambda i, j: (layer, 0, j)),
            pl.BlockSpec((None, D, D), lambda i, j: (layer, 0, 0)),
            pl.BlockSpec((None, D, LANE), lambda i, j: (layer, 0, 0)),
        ],
        out_specs=[
            pl.BlockSpec((tm, D), lambda i, j: (i, j)),
            pl.BlockSpec((D, tm), lambda i, j: (0, i)),
            pl.BlockSpec((tm, LANE), lambda i, j: (i, 0)),
            pl.BlockSpec((SSD_HEADS, tm), lambda i, j: (0, i)),
        ],
        out_shape=[
            jax.ShapeDtypeStruct((s, N_NAT), BF16),
            jax.ShapeDtypeStruct((D, s), BF16),
            jax.ShapeDtypeStruct((s, LANE), F32),
            jax.ShapeDtypeStruct((SSD_HEADS, s), F32),
        ],
        scratch_shapes=[pltpu.VMEM((tm, D), BF16)],
        compiler_params=_cparams(2),
        name="proj",
    )(x, norm_w, w_nat, w_vt, w_dt)


def _sigmoid(v):
    return 0.5 * jnp.tanh(0.5 * v) + 0.5


def _split3(v):
    hi = v.astype(BF16)
    r1 = v - hi.astype(F32)
    mid = r1.astype(BF16)
    lo = (r1 - mid.astype(F32)).astype(BF16)
    return hi, mid, lo


def _dot_exact_lhs(a_exact, v):
    ab = a_exact.astype(BF16)
    return sum(jnp.dot(ab, t, preferred_element_type=F32) for t in _split3(v))


def _dot_exact_rhs(v, b_exact):
    bb = b_exact.astype(BF16)
    return sum(jnp.dot(t, bb, preferred_element_type=F32) for t in _split3(v))


def _softplus(v):
    return jnp.maximum(v, 0.0) + jnp.log1p(jnp.exp(-jnp.abs(v)))


def _ssd_kernel(z_ref, xs_ref, bc_ref, dt_ref, dtt_ref, cw_ref, cb_ref, dtb_ref, dtbt_ref,
                al_ref, alt_ref, dsk_ref, nw_ref, e_ref, out_ref, ext_ref, st_ref):
    @pl.when(pl.program_id(0) == 0)
    def _():
        ext_ref[0:8, :] = jnp.zeros((8, 2 * D), F32)
        st_ref[...] = jnp.zeros(st_ref.shape, F32)

    ext_ref[8:8 + BLK, 0:D] = xs_ref[...].astype(F32)
    ext_ref[8:8 + BLK, D:2 * D] = bc_ref[...].astype(F32)
    acc = cb_ref[...] + cw_ref[0:1, :] * ext_ref[5:5 + BLK, :]
    for i in range(1, SSD_CONV):
        acc = acc + cw_ref[i:i + 1, :] * ext_ref[5 + i:5 + i + BLK, :]
    ext_ref[0:8, :] = ext_ref[BLK:BLK + 8, :]
    act = acc * _sigmoid(acc)
    xs = act[:, 0:D]

    dt = _softplus(dt_ref[...] + dtb_ref[...])
    da = dt * (-jnp.exp(al_ref[...]))
    dtt = _softplus(dtt_ref[...] + dtbt_ref[...])
    dat = dtt * (-jnp.exp(alt_ref[...]))
    row = lax.broadcasted_iota(jnp.int32, (BLK, BLK), 0)
    col = lax.broadcasted_iota(jnp.int32, (BLK, BLK), 1)
    lower = row >= col
    tri = lower.astype(F32)
    a_cum = _dot_exact_lhs(tri, da)
    a_cumt = _dot_exact_rhs(dat, (row <= col).astype(F32))
    xdt = xs * _dot_exact_rhs(dt, e_ref[...])
    xdt_b = xdt.astype(BF16)

    lane = lax.broadcasted_iota(jnp.int32, (BLK, LANE), 1)
    first = lane < SSD_HEAD_DIM
    rowp = lax.broadcasted_iota(jnp.int32, (LANE, 1), 0) < SSD_HEAD_DIM
    y_parts = []
    for g in range(SSD_GROUPS):
        b_g = act[:, D + g * SSD_STATE:D + (g + 1) * SSD_STATE]
        c_g = act[:, D + SSD_GROUPS * SSD_STATE + g * SSD_STATE:
                  D + SSD_GROUPS * SSD_STATE + (g + 1) * SSD_STATE]
        b_gb = b_g.astype(BF16)
        c_gb = c_g.astype(BF16)
        cbm = lax.dot_general(c_gb, b_gb, NT, preferred_element_type=F32)
        for pp in range(2):
            pair = 2 * g + pp
            h0 = 2 * pair
            x_pair = xdt_b[:, pair * LANE:(pair + 1) * LANE]
            y_pair = jnp.zeros((BLK, LANE), F32)
            for hh in range(2):
                h = h0 + hh
                lm = jnp.where(lower, jnp.exp(a_cum[:, h:h + 1] - a_cumt[h:h + 1, :]), 0.0)
                w = (cbm * lm).astype(BF16)
                xm = jnp.where(first if hh == 0 else jnp.logical_not(first), x_pair,
                               jnp.zeros_like(x_pair))
                y_pair = y_pair + jnp.dot(w, xm, preferred_element_type=F32)
            ac_pair = jnp.where(first, a_cum[:, h0:h0 + 1], a_cum[:, h0 + 1:h0 + 2])
            al_pair = ac_pair[BLK - 1:BLK, :]
            r_pair = st_ref[pair]
            y_off = lax.dot_general(c_gb, r_pair.astype(BF16), NT, preferred_element_type=F32)
            y_pair = y_pair + y_off * jnp.exp(ac_pair)
            xdec = (xdt[:, pair * LANE:(pair + 1) * LANE] * jnp.exp(al_pair - ac_pair)).astype(BF16)
            st_new = lax.dot_general(xdec, b_gb, TN, preferred_element_type=F32)
            al_col = jnp.where(rowp, a_cumt[h0:h0 + 1, BLK - 1:BLK],
                               a_cumt[h0 + 1:h0 + 2, BLK - 1:BLK])
            st_ref[pair] = jnp.exp(al_col) * r_pair + st_new
            y_parts.append(y_pair)
    y = jnp.concatenate(y_parts, axis=1) + dsk_ref[...] * xs
    z = z_ref[...].astype(F32)
    y = y * (z * _sigmoid(z))
    gw = D // SSD_GROUPS
    for g in range(SSD_GROUPS):
        yg = y[:, g * gw:(g + 1) * gw]
        yg = yg * lax.rsqrt(jnp.mean(yg * yg, axis=-1, keepdims=True) + EPS)
        out_ref[:, g * gw:(g + 1) * gw] = (yg * nw_ref[:, g * gw:(g + 1) * gw]).astype(BF16)


def _ssd(proj, dt, dtt, cw, cb, dtb, dtbt, al, alt, dsk, nw, expand):
    s = proj.shape[0]
    row = lambda c: pl.BlockSpec((BLK, D), lambda i, c=c: (i, c))
    return pl.pallas_call(
        _ssd_kernel,
        grid=(s // BLK,),
        in_specs=[
            row(C_Z), row(C_XS), row(C_BC),
            pl.BlockSpec((BLK, LANE), lambda i: (i, 0)),
            pl.BlockSpec((SSD_HEADS, BLK), lambda i: (0, i)),
            _const_spec((SSD_CONV, 2 * D)), _const_spec((1, 2 * D)),
            _const_spec((1, LANE)), _const_spec((SSD_HEADS, 1)),
            _const_spec((1, LANE)), _const_spec((SSD_HEADS, 1)),
            _const_spec((1, D)), _const_spec((1, D)), _const_spec((LANE, D)),
        ],
        out_specs=pl.BlockSpec((BLK, D), lambda i: (i, 0)),
        out_shape=jax.ShapeDtypeStruct((s, D), BF16),
        scratch_shapes=[pltpu.VMEM((BLK + 8, 2 * D), F32),
                        pltpu.VMEM((SSD_HEADS // 2, LANE, SSD_STATE), F32)],
        compiler_params=_cparams(1),
        name="ssd",
    )(proj, proj, proj, dt, dtt, cw, cb, dtb, dtbt, al, alt, dsk, nw, expand)


S5_LAGPAD = 3


def _s5_prep_kernel(lrr_ref, lir_ref, lsr_ref, lrc_ref, lic_ref, lsc_ref,
                    bre_ref, bim_ref, cre_ref, cim_ref, scat_ref,
                    bd_ref, bxr_ref, bxi_ref, cxr_ref, cxi_ref, alr_ref, ali_ref):
    n = S5_L * S5_CH
    rows = 2 * n
    cols = 2 * n
    bd_ref[0, :, 0:S5_LAGPAD * LANE] = jnp.zeros((LANE, S5_LAGPAD * LANE), BF16)
    bxr_ref[...] = jnp.zeros(bxr_ref.shape, BF16)
    bxi_ref[...] = jnp.zeros(bxi_ref.shape, BF16)
    lane128 = lax.broadcasted_iota(jnp.int32, (S5_CH, LANE), 1) // S5_CH
    rk = lax.broadcasted_iota(jnp.int32, (rows, 1), 0)
    ek = (S5_L - 1 - (rk % n) // S5_CH).astype(F32)
    ct = lax.broadcasted_iota(jnp.int32, (1, cols), 1)
    et = ((ct % n) // S5_CH + 1).astype(F32)
    for pr in range(S5_OCT // 2):
        lr = jnp.minimum(lrr_ref[pr], -S5_CLIP)
        li = lir_ref[pr]
        st = jnp.exp(lsr_ref[pr])
        lrs, lis = lr * st, li * st
        mag = jnp.exp(lrs)
        abr, abi = mag * jnp.cos(lis), mag * jnp.sin(lis)
        nr, ni = abr - 1.0, abi
        den = lr * lr + li * li
        cfr = (nr * lr + ni * li) / den
        cfi = (ni * lr - nr * li) / den
        bre, bim = bre_ref[pr], bim_ref[pr]
        bbr = cfr * bre - cfi * bim
        bbi = cfr * bim + cfi * bre
        pm = jnp.exp(ek * lrs)
        pbr, pbi = pm * jnp.cos(ek * lis), pm * jnp.sin(ek * lis)
        bxr = pbr * bbr - pbi * bbi
        bxi = pbr * bbi + pbi * bbr
        alm = jnp.exp(float(S5_L) * lrs)
        alr_ref[pr] = alm * jnp.cos(float(S5_L) * lis)
        ali_ref[pr] = alm * jnp.sin(float(S5_L) * lis)
        lrc = jnp.minimum(lrc_ref[pr], -S5_CLIP)
        stc = jnp.exp(lsc_ref[pr])
        lrsc, lisc = lrc * stc, lic_ref[pr] * stc
        qm = jnp.exp(et * lrsc)
        qr, qi = qm * jnp.cos(et * lisc), qm * jnp.sin(et * lisc)
        cre, cim = cre_ref[pr], cim_ref[pr]
        rsl = slice(pr * LANE, (pr + 1) * LANE)
        cxr_ref[0, rsl, :] = jnp.dot((cre * qr - cim * qi).astype(BF16), scat_ref[pr],
                                     preferred_element_type=F32).astype(BF16)
        cxi_ref[0, rsl, :] = jnp.dot((-(cre * qi + cim * qr)).astype(BF16), scat_ref[pr],
                                     preferred_element_type=F32).astype(BF16)
        rw = (jnp.dot(bxr, cre, precision=HI, preferred_element_type=F32)
              - jnp.dot(bxi, cim, precision=HI, preferred_element_type=F32))
        bxr_b, bxi_b = bxr.astype(BF16), bxi.astype(BF16)
        for gi in range(2):
            g8 = 2 * pr + gi
            for k in range(S5_L):
                src = slice(gi * n + k * S5_CH, gi * n + (k + 1) * S5_CH)
                dst = slice(k * LANE + g8 * S5_CH, k * LANE + (g8 + 1) * S5_CH)
                bxr_ref[0, dst, rsl] = bxr_b[src, :]
                bxi_ref[0, dst, rsl] = bxi_b[src, :]
            for j in range(S5_L):
                m = S5_L - 1 - j
                blk = rw[gi * n + m * S5_CH:gi * n + (m + 1) * S5_CH, gi * n:gi * n + LANE]
                bd_ref[0, g8 * S5_CH:(g8 + 1) * S5_CH,
                       (S5_LAGPAD + j) * LANE:(S5_LAGPAD + j + 1) * LANE] = jnp.where(
                           lane128 == g8, blk, 0.0).astype(BF16)


def _s5_prep(lrr, lir, lsr, lrc, lic, lsc, bre2, bim2, cre2, cim2, scat, layer):
    n = S5_L * S5_CH
    half = S5_OCT // 2
    n_oct = S5_GROUPS // S5_OCT
    w8 = S5_L * LANE
    ow = S5_OCT * S5_STATE
    blk4 = lambda a, b: pl.BlockSpec((half, a, b), lambda i: (layer * n_oct + i, 0, 0))
    out4 = lambda a, b: pl.BlockSpec((half, a, b), lambda i: (i, 0, 0))
    oct3 = lambda a, b: pl.BlockSpec((1, a, b), lambda i: (i, 0, 0))
    return pl.pallas_call(
        _s5_prep_kernel,
        grid=(n_oct,),
        in_specs=[blk4(1, LANE)] * 3 + [blk4(LANE, 1)] * 3
                 + [blk4(2 * n, LANE)] * 2 + [blk4(LANE, 2 * n)] * 2
                 + [_const_spec((half, 2 * n, w8))],
        out_specs=[oct3(LANE, (S5_LAGPAD + S5_L) * LANE),
                   oct3(w8, ow), oct3(w8, ow), oct3(ow, w8), oct3(ow, w8),
                   out4(1, LANE), out4(1, LANE)],
        out_shape=[jax.ShapeDtypeStruct((n_oct, LANE, (S5_LAGPAD + S5_L) * LANE), BF16),
                   jax.ShapeDtypeStruct((n_oct, w8, ow), BF16),
                   jax.ShapeDtypeStruct((n_oct, w8, ow), BF16),
                   jax.ShapeDtypeStruct((n_oct, ow, w8), BF16),
                   jax.ShapeDtypeStruct((n_oct, ow, w8), BF16),
                   jax.ShapeDtypeStruct((S5_PAIRS, 1, LANE), F32),
                   jax.ShapeDtypeStruct((S5_PAIRS, 1, LANE), F32)],
        compiler_params=_cparams(1),
        name="s5_prep",
    )(lrr, lir, lsr, lrc, lic, lsc, bre2, bim2, cre2, cim2, scat)


def _s5_local_kernel(u_ref, bxr_ref, bxi_ref, sr_ref, si_ref):
    sr_ref[...] = jnp.dot(u_ref[0], bxr_ref[0], preferred_element_type=F32)
    si_ref[...] = jnp.dot(u_ref[0], bxi_ref[0], preferred_element_type=F32)


def _s5_local(u8, bxr8, bxi8):
    nc, w = u8.shape[1], u8.shape[2]
    ow = S5_OCT * S5_STATE
    return pl.pallas_call(
        _s5_local_kernel,
        grid=(S5_GROUPS // S5_OCT,),
        in_specs=[pl.BlockSpec((1, nc, w), lambda i: (i, 0, 0)),
                  pl.BlockSpec((1, w, ow), lambda i: (i, 0, 0)),
                  pl.BlockSpec((1, w, ow), lambda i: (i, 0, 0))],
        out_specs=[pl.BlockSpec((nc, ow), lambda i: (0, i))] * 2,
        out_shape=[jax.ShapeDtypeStruct((nc, S5_GROUPS * S5_STATE), F32)] * 2,
        compiler_params=_cparams(1),
        name="s5_local",
    )(u8, bxr8, bxi8)


def _s5_scan_kernel(xr_ref, xi_ref, ar_ref, ai_ref, pr_ref, pi_ref):
    ar, ai = ar_ref[...], ai_ref[...]
    w = ar.shape[1]

    def body(r, carry):
        sr, si = carry
        pr_ref[pl.ds(r, 1), :] = sr
        pi_ref[pl.ds(r, 1), :] = si
        xr = xr_ref[pl.ds(r, 1), :]
        xi = xi_ref[pl.ds(r, 1), :]
        return ar * sr - ai * si + xr, ar * si + ai * sr + xi

    lax.fori_loop(0, xr_ref.shape[0], body, (jnp.zeros((1, w), F32), jnp.zeros((1, w), F32)))


def _s5_scan(xr, xi, ar, ai, tw):
    nc, width = xr.shape
    col = pl.BlockSpec((nc, tw), lambda i: (0, i))
    vec = pl.BlockSpec((1, tw), lambda i: (0, i))
    return pl.pallas_call(
        _s5_scan_kernel,
        grid=(width // tw,),
        in_specs=[col, col, vec, vec],
        out_specs=[col, col],
        out_shape=[jax.ShapeDtypeStruct((nc, width), F32)] * 2,
        compiler_params=_cparams(1),
        name="s5_scan",
    )(xr, xi, ar, ai)


def _s5_out_kernel(u_ref, bd_ref, pr_ref, pi_ref, cxr_ref, cxi_ref, y_ref):
    tt = S5_LAGPAD + 1
    prb, pib = pr_ref[...].astype(BF16), pi_ref[...].astype(BF16)
    for jt in range(S5_L // tt):
        osl = slice(jt * tt * LANE, (jt + 1) * tt * LANE)
        acc = (jnp.dot(prb, cxr_ref[0, :, osl], preferred_element_type=F32)
               + jnp.dot(pib, cxi_ref[0, :, osl], preferred_element_type=F32))
        for k in range((jt + 1) * tt):
            lag0 = jt * tt - k + S5_LAGPAD
            acc = acc + jnp.dot(u_ref[0, :, k * LANE:(k + 1) * LANE],
                                bd_ref[0, :, lag0 * LANE:(lag0 + tt) * LANE],
                                preferred_element_type=F32)
        y_ref[0, :, osl] = acc.astype(BF16)


def _s5_out(u8, bd, pr, pi, cxr8, cxi8):
    n_oct, nc, w = u8.shape
    ow = S5_OCT * S5_STATE
    return pl.pallas_call(
        _s5_out_kernel,
        grid=(n_oct,),
        in_specs=[pl.BlockSpec((1, nc, w), lambda i: (i, 0, 0)),
                  pl.BlockSpec((1, LANE, bd.shape[2]), lambda i: (i, 0, 0)),
                  pl.BlockSpec((nc, ow), lambda i: (0, i)),
                  pl.BlockSpec((nc, ow), lambda i: (0, i)),
                  pl.BlockSpec((1, ow, w), lambda i: (i, 0, 0)),
                  pl.BlockSpec((1, ow, w), lambda i: (i, 0, 0))],
        out_specs=pl.BlockSpec((1, nc, w), lambda i: (i, 0, 0)),
        out_shape=jax.ShapeDtypeStruct((n_oct, nc, w), BF16),
        compiler_params=_cparams(1),
        name="s5_out",
    )(u8, bd, pr, pi, cxr8, cxi8)


def _kmean_kernel(k_ref, o_ref):
    for r in range(o_ref.shape[0]):
        o_ref[r:r + 1, :] = jnp.mean(k_ref[r * BLK:(r + 1) * BLK, :].astype(F32), axis=0, keepdims=True)


def _kmean(proj, rows):
    s = proj.shape[0]
    nb = s // BLK
    return pl.pallas_call(
        _kmean_kernel,
        grid=(nb // rows,),
        in_specs=[pl.BlockSpec((rows * BLK, D), lambda i: (i, C_K))],
        out_specs=pl.BlockSpec((rows, D), lambda i: (i, 0)),
        out_shape=jax.ShapeDtypeStruct((nb, D), F32),
        compiler_params=_cparams(1),
        name="kmean",
    )(proj)


def _moba_kernel(q_ref, k_ref, vt_ref, km_ref, o_ref, bias_ref, s_ref):
    i = pl.program_id(1)
    nb = km_ref.shape[0]
    kg = min(MOBA_KG, nb)
    kt = kg * BLK
    q = q_ref[...].astype(F32)
    lane = lax.broadcasted_iota(jnp.int32, (BLK, LANE), 1)
    blk = lax.broadcasted_iota(jnp.int32, (nb, BLK), 0)
    blk_f = blk.astype(F32)
    scale = MOBA_HD ** -0.5 * math.log2(math.e)
    qs = []
    for h in range(2):
        in_head = (lane >= h * MOBA_HD) & (lane < (h + 1) * MOBA_HD)
        qh = jnp.where(in_head, q, 0.0)
        g = sum(lax.dot_general(t, qh.astype(BF16), NT, preferred_element_type=F32)
                for t in _split3(km_ref[...]))
        g = jnp.where(blk < i, g, -jnp.inf)
        sel = jnp.zeros((nb, BLK), jnp.bool_)
        for _ in range(MOBA_TOPK):
            mx = jnp.max(g, axis=0, keepdims=True)
            idx = jnp.min(jnp.where(g == mx, blk_f, float(nb)), axis=0, keepdims=True)
            hit = blk_f == idx
            sel = sel | (hit & (mx > -jnp.inf))
            g = jnp.where(hit, -jnp.inf, g)
        bias = jnp.where(sel, 0.0, NEG)
        for t in range(nb // kg):
            bias_ref[h, t, 0:kg, :] = bias[t * kg:(t + 1) * kg, :]
        qs.append((qh * scale).astype(BF16))

    krow = lax.broadcasted_iota(jnp.int32, (BLK, BLK), 0)
    qcol = lax.broadcasted_iota(jnp.int32, (BLK, BLK), 1)
    causal = krow <= qcol
    last_g = nb // kg - 1
    ones = lambda n: jnp.ones((16, n), BF16)

    ng = MOBA_TRIP
    last_trip = nb // kg // ng - 1

    def produce(tp, h):
        tcp = jnp.minimum(tp, last_trip)
        kb = k_ref[pl.ds(pl.multiple_of(tcp * ng * kt, ng * kt), ng * kt), :]
        s_t = lax.dot_general(kb, qs[h], NT, preferred_element_type=F32)
        mx = None
        for g in range(ng):
            bias = bias_ref[h, ng * tcp + g, 0:kg, :]
            for u in range(kg):
                rows = slice((g * kg + u) * BLK, (g * kg + u + 1) * BLK)
                su = s_t[rows, :] + bias[u:u + 1, :]
                s_ref[h, rows, :] = su
                mu = jnp.max(su, axis=0, keepdims=True)
                mx = mu if mx is None else jnp.maximum(mx, mu)
        return mx

    def step(tp, carry, width, look_ahead, own=False):
        state, maxima = carry
        start = i * BLK if own else jnp.minimum(tp, last_trip) * (ng * kt)
        vt = vt_ref[:, pl.ds(pl.multiple_of(start, BLK), width)]
        new_state, new_maxima = [], []
        for h in range(2):
            m, acc = state[h]
            mn = jnp.maximum(m, maxima[h])
            p = jnp.exp2(s_ref[h, 0:width, :] - mn).astype(BF16)
            new_maxima.append(produce(tp + 1, h) if look_ahead else maxima[h])
            va = jnp.concatenate([vt[h * MOBA_HD:(h + 1) * MOBA_HD, :], ones(width)], axis=0)
            new_state.append((mn, jnp.exp2(m - mn) * acc + jnp.dot(va, p, preferred_element_type=F32)))
        return tuple(new_state), tuple(new_maxima)

    kb_own = k_ref[pl.ds(pl.multiple_of(i * BLK, BLK), BLK), :]
    own_max = []
    for h in range(2):
        s_own = jnp.where(causal, lax.dot_general(kb_own, qs[h], NT, preferred_element_type=F32), NEG)
        s_ref[h, 0:BLK, :] = s_own
        own_max.append(jnp.max(s_own, axis=0, keepdims=True))
    init = (jnp.full((1, BLK), NEG, F32), jnp.zeros((MOBA_HD + 16, BLK), F32))
    state, maxima = step(-1, ((init, init), tuple(own_max)), BLK, True, own=True)

    n_groups = (i + kg - 1) // kg
    carry = lax.fori_loop(0, n_groups // ng, lambda tp, c: step(tp, c, ng * kt, True),
                          (tuple(state), maxima))
    tails = [lambda c: c] + [functools.partial(lambda r, c: step(n_groups // ng, c, r * kt, False), r)
                             for r in range(1, ng)]
    carry, _ = lax.switch(n_groups % ng, tails, carry)
    for h in range(2):
        _, acc = carry[h]
        o_ref[h * MOBA_HD:(h + 1) * MOBA_HD, :] = (
            acc[0:MOBA_HD, :] / acc[MOBA_HD:MOBA_HD + 1, :]).astype(BF16)


def _moba(proj, vt, kmean):
    s = proj.shape[0]
    nb = s // BLK
    assert nb % (MOBA_TRIP * min(MOBA_KG, nb)) == 0
    per = D // LANE
    return pl.pallas_call(
        _moba_kernel,
        grid=(MOBA_HEADS // 2, nb),
        in_specs=[pl.BlockSpec((BLK, LANE), lambda hp, i: (i, C_Q * per + hp)),
                  pl.BlockSpec((s, LANE), lambda hp, i: (0, C_K * per + hp)),
                  pl.BlockSpec((LANE, s), lambda hp, i: (hp, 0)),
                  pl.BlockSpec((nb, LANE), lambda hp, i: (0, hp))],
        out_specs=pl.BlockSpec((LANE, BLK), lambda hp, i: (hp, i)),
        out_shape=jax.ShapeDtypeStruct((D, s), BF16),
        scratch_shapes=[pltpu.VMEM((2, nb // min(MOBA_KG, nb), SUBLANE, BLK), F32),
                        pltpu.VMEM((2, MOBA_TRIP * min(MOBA_KG, nb) * BLK, BLK), F32)],
        compiler_params=_cparams(2),
        name="moba",
    )(proj, proj, vt, kmean)


def _merge_kernel(final, x_ref, ya_ref, ys_ref, u_ref, sg_ref, att_ref, mg_ref,
                  g0_ref, g1_ref, g2_ref, gb_ref, s5d_ref, glub_ref, fnw_ref,
                  gluw_ref, wa_ref, wb_ref, wc_ref, wo_ref, out_ref):
    mm = lambda a, w_ref: jnp.dot(a.astype(BF16), w_ref[...], preferred_element_type=F32)
    f32 = lambda r: r[...].astype(F32)
    yb = f32(ys_ref) + s5d_ref[...] * f32(u_ref)
    yb = jax.nn.gelu(yb)
    yb = yb * _sigmoid(mm(yb, gluw_ref) + glub_ref[...])
    sg = f32(sg_ref)
    yb = yb * (sg * _sigmoid(sg))
    mg = f32(mg_ref)
    att = f32(att_ref).T * (mg * _sigmoid(mg))
    merged = (_sigmoid(f32(g0_ref) + gb_ref[:, 0:D]) * mm(ya_ref[...], wa_ref)
              + _sigmoid(f32(g1_ref) + gb_ref[:, D:2 * D]) * mm(yb, wb_ref)
              + _sigmoid(f32(g2_ref) + gb_ref[:, 2 * D:3 * D]) * mm(att, wc_ref))
    xn = x_ref[...] + mm(merged, wo_ref)
    if final:
        xn = xn * lax.rsqrt(jnp.mean(xn * xn, axis=-1, keepdims=True) + EPS) * fnw_ref[...]
    out_ref[...] = xn


def _merge(final, x, ya, ys, proj, att_t, gate_b, s5d, glub, fnw, gluw, wa, wb, wc, wo, layer, tm):
    s = x.shape[0]
    rowb = pl.BlockSpec((tm, D), lambda i: (i, 0))
    pcol = lambda c: pl.BlockSpec((tm, D), lambda i, c=c: (i, c))
    wspec = pl.BlockSpec((None, D, D), lambda i: (layer, 0, 0), pipeline_mode=pl.Buffered(1))
    return pl.pallas_call(
        functools.partial(_merge_kernel, final),
        grid=(s // tm,),
        in_specs=[rowb, rowb, rowb, pcol(C_S5U), pcol(C_S5G),
                  pl.BlockSpec((D, tm), lambda i: (0, i)), pcol(C_MG),
                  pcol(C_GL), pcol(C_GL + 1), pcol(C_GL + 2),
                  _const_spec((1, 3 * D)), _const_spec((1, D)), _const_spec((1, D)),
                  _const_spec((1, D)), wspec, wspec, wspec, wspec, wspec],
        out_specs=rowb,
        out_shape=jax.ShapeDtypeStruct((s, D), F32),
        compiler_params=_cparams(1),
        name="merge",
    )(x, ya, ys, proj, proj, att_t, proj, proj, proj, proj,
      gate_b, s5d, glub, fnw, gluw, wa, wb, wc, wo)


def _block_diag_pairs(a):
    g, r, c = a.shape
    a2 = a.reshape(g // 2, 2, r, c)
    out = jnp.einsum("pgrc,gh->pgrhc", a2, jnp.eye(2, dtype=a.dtype))
    return out.reshape(g // 2, 2 * r, 2 * c)


def _lane_scatter():
    half = S5_OCT // 2
    sel = np.zeros((half, 2, S5_OCT), np.float32)
    for pr in range(half):
        for gi in range(2):
            sel[pr, gi, 2 * pr + gi] = 1.0
    eye_t, eye_c = np.eye(S5_L, dtype=np.float32), np.eye(S5_CH, dtype=np.float32)
    sc = np.einsum("pgh,tu,cd->pgtcuhd", sel, eye_t, eye_c)
    return jnp.asarray(sc.reshape(half, 2 * S5_L * S5_CH, S5_L * LANE), dtype=BF16)


def _row_tile(s, want):
    t = min(want, s)
    while s % t:
        t //= 2
    return t


def kernel(x, norm_w, w_in, gate_b, conv_w, conv_b, dt_bias, a_log, ssd_d, ssd_norm_w, w_proj_a,
           lambda_re, lambda_im, log_step, s5_b_re, s5_b_im, s5_c_re, s5_c_im, s5_d, glu_w, glu_b,
           w_proj_b, w_proj_c, w_out, final_norm_w):
    b, s, _ = x.shape
    assert b == 1 and s % BLK == 0 and x.shape[2] == D
    depth = norm_w.shape[0]
    nb = s // BLK
    nc = s // S5_L
    xc = x.reshape(s, D)
    o_dt = 3 * D
    o_s5u = o_dt + SSD_HEADS
    o_q = o_s5u + 2 * D
    o_v = o_q + 2 * D
    o_mg = o_v + D
    expand = (jnp.arange(LANE)[:, None] == (jnp.arange(D)[None, :] // SSD_HEAD_DIM)).astype(F32)
    pad16 = lambda a: jnp.pad(a.astype(F32), (0, LANE - SSD_HEADS)).reshape(1, LANE)
    tile_b = lambda a: jnp.tile(a.transpose(0, 2, 1), (1, S5_L, 1))
    tile_c = lambda a: jnp.tile(a.transpose(0, 2, 1), (1, 1, S5_L))
    scat = _lane_scatter()
    w_nat = jnp.concatenate([w_in[:, :, 0:o_dt], w_in[:, :, o_s5u:o_v], w_in[:, :, o_mg:]],
                            axis=2).astype(BF16)
    w_vt = jnp.swapaxes(w_in[:, :, o_v:o_mg], 1, 2).astype(BF16)
    w_dt = jnp.pad(w_in[:, :, o_dt:o_s5u], ((0, 0), (0, 0), (0, LANE - SSD_HEADS)))
    merge_w = [a.astype(BF16) for a in (glu_w, w_proj_a, w_proj_b, w_proj_c, w_out)]
    pairs = depth * S5_PAIRS
    ls_full = jnp.repeat(log_step.reshape(-1), S5_STATE)
    s5_rows = [a.reshape(pairs, 1, LANE) for a in (lambda_re, lambda_im, ls_full)]
    s5_cols = [a.reshape(pairs, LANE, 1) for a in (lambda_re, lambda_im, ls_full)]
    s5_mats = ([_block_diag_pairs(tile_b(a.reshape(depth * S5_GROUPS, S5_STATE, S5_CH)))
                for a in (s5_b_re, s5_b_im)]
               + [_block_diag_pairs(tile_c(a.reshape(depth * S5_GROUPS, S5_CH, S5_STATE)))
                  for a in (s5_c_re, s5_c_im)])

    for l in range(depth):
        proj, vt, dt, dtt = _project(xc, norm_w[l].reshape(1, D), w_nat, w_vt, w_dt, l,
                                     _row_tile(s, 1024))

        ya = _ssd(proj, dt, dtt, conv_w[l], conv_b[l].reshape(1, 2 * D),
                  pad16(dt_bias[l]), dt_bias[l].reshape(SSD_HEADS, 1),
                  pad16(a_log[l]), a_log[l].reshape(SSD_HEADS, 1),
                  jnp.repeat(ssd_d[l], SSD_HEAD_DIM).reshape(1, D),
                  ssd_norm_w[l].reshape(1, D), expand)

        bd, bxr8, bxi8, cxr8, cxi8, alr, ali = _s5_prep(*s5_rows, *s5_cols, *s5_mats, scat, l)
        n_oct = S5_GROUPS // S5_OCT
        u8 = (proj[:, C_S5U * D:(C_S5U + 1) * D].reshape(nc, S5_L, n_oct, LANE)
              .transpose(2, 0, 1, 3).reshape(n_oct, nc, S5_L * LANE))
        sr, si = _s5_local(u8, bxr8, bxi8)
        pr, pi = _s5_scan(sr, si, alr.reshape(1, -1), ali.reshape(1, -1), 512)
        y8 = _s5_out(u8, bd, pr, pi, cxr8, cxi8)
        ys = y8.reshape(n_oct, nc, S5_L, LANE).transpose(1, 2, 0, 3).reshape(s, D)

        km = _kmean(proj, math.gcd(nb, 8))
        att_t = _moba(proj, vt, km)

        xc = _merge(l == depth - 1, xc, ya, ys, proj, att_t, gate_b[l].reshape(1, 3 * D),
                    s5_d[l].reshape(1, D), glu_b[l].reshape(1, D), final_norm_w.reshape(1, D),
                    *merge_w, l, _row_tile(s, 512))
    return xc.reshape(b, s, D)
```

```python
import functools
import math

import jax
import jax.numpy as jnp
import numpy as np
from jax import lax
from jax.experimental import pallas as pl
from jax.experimental.pallas import tpu as pltpu

F32 = jnp.float32
BF16 = jnp.bfloat16
HI = lax.Precision.HIGHEST

D = 1024
EPS = 1e-6
BLK = 256
SSD_HEADS = 16
SSD_HEAD_DIM = 64
SSD_GROUPS = 4
SSD_STATE = 128
SSD_CONV = 4
S5_GROUPS = 64
S5_CH = 16
S5_STATE = 64
S5_CLIP = 1e-4
S5_L = 16
S5_PAIRS = S5_GROUPS // 2
S5_OCT = 8
MOBA_HEADS = 16
MOBA_HD = 64
MOBA_TOPK = 3
MOBA_KG = 4
MOBA_TRIP = 2
NEG = -1e30
LANE = 128
SUBLANE = 8
VMEM_LIMIT = 56 * 1024 * 1024

C_Z, C_XS, C_BC, C_S5U, C_S5G, C_Q, C_K, C_MG, C_GL = 0, 1, 2, 3, 4, 5, 6, 7, 8
N_NAT = 11 * D

NT = (((1,), (1,)), ((), ()))
TN = (((0,), (0,)), ((), ()))


def _cparams(n_axes):
    return pltpu.CompilerParams(dimension_semantics=("arbitrary",) * n_axes,
                                vmem_limit_bytes=VMEM_LIMIT)


def _const_spec(shape):
    nd = len(shape)
    return pl.BlockSpec(shape, lambda *_: (0,) * nd)


def _proj_kernel(x_ref, nw_ref, w_ref, wvt_ref, wdt_ref,
                 proj_ref, vt_ref, dt_ref, h_ref):
    @pl.when(pl.program_id(1) == 0)
    def _():
        x = x_ref[...]
        h = x * lax.rsqrt(jnp.mean(x * x, axis=-1, keepdims=True) + EPS) * nw_ref[...]
        hb = h.astype(BF16)
        h_ref[...] = hb
        vt = lax.dot_general(wvt_ref[...], hb, NT, preferred_element_type=F32)
        vt_ref[...] = vt.astype(BF16)
        dt_ref[...] = jnp.dot(h, wdt_ref[...], precision=HI, preferred_element_type=F32)

    proj_ref[...] = jnp.dot(h_ref[...], w_ref[...], preferred_element_type=F32).astype(BF16)


def _project(x, norm_w, w_nat, w_vt, w_dt, layer, tm):
    s = x.shape[0]
    grid = (s // tm, N_NAT // D)
    return pl.pallas_call(
        _proj_kernel,
        grid=grid,
        in_specs=[
            pl.BlockSpec((tm, D), lambda i, j: (i, 0)),
            _const_spec((1, D)),
            pl.BlockSpec((None, D, D), lambda i, j: (layer, 0, j)),
            pl.BlockSpec((None, D, D), lambda i, j: (layer, 0, 0)),
            pl.BlockSpec((None, D, LANE), lambda i, j: (layer, 0, 0)),
        ],
        out_specs=[
            pl.BlockSpec((tm, D), lambda i, j: (i, j)),
            pl.BlockSpec((D, tm), lambda i, j: (0, i)),
            pl.BlockSpec((tm, LANE), lambda i, j: (i, 0)),
        ],
        out_shape=[
            jax.ShapeDtypeStruct((s, N_NAT), BF16),
            jax.ShapeDtypeStruct((D, s), BF16),
            jax.ShapeDtypeStruct((s, LANE), F32),
        ],
        scratch_shapes=[pltpu.VMEM((tm, D), BF16)],
        compiler_params=_cparams(2),
        name="proj",
    )(x, norm_w, w_nat, w_vt, w_dt)


def _sigmoid(v):
    return 0.5 * jnp.tanh(0.5 * v) + 0.5


def _split3(v):
    hi = v.astype(BF16)
    r1 = v - hi.astype(F32)
    mid = r1.astype(BF16)
    lo = (r1 - mid.astype(F32)).astype(BF16)
    return hi, mid, lo


def _dot_exact_lhs(a_exact, v):
    ab = a_exact.astype(BF16)
    return sum(jnp.dot(ab, t, preferred_element_type=F32) for t in _split3(v))


def _dot_exact_rhs(v, b_exact):
    bb = b_exact.astype(BF16)
    return sum(jnp.dot(t, bb, preferred_element_type=F32) for t in _split3(v))


def _softplus(v):
    return jnp.maximum(v, 0.0) + jnp.log1p(jnp.exp(-jnp.abs(v)))


def _ssd_kernel(z_ref, xs_ref, bc_ref, dt_ref, cw_ref, cb_ref, dtb_ref,
                al_ref, dsk_ref, nw_ref, e_ref, out_ref, ext_ref, st_ref):
    @pl.when(pl.program_id(0) == 0)
    def _():
        ext_ref[0:8, :] = jnp.zeros((8, 2 * D), F32)
        st_ref[...] = jnp.zeros(st_ref.shape, F32)

    ext_ref[8:8 + BLK, 0:D] = xs_ref[...].astype(F32)
    ext_ref[8:8 + BLK, D:2 * D] = bc_ref[...].astype(F32)
    acc = cb_ref[...] + cw_ref[0:1, :] * ext_ref[5:5 + BLK, :]
    for i in range(1, SSD_CONV):
        acc = acc + cw_ref[i:i + 1, :] * ext_ref[5 + i:5 + i + BLK, :]
    ext_ref[0:8, :] = ext_ref[BLK:BLK + 8, :]
    act = acc * _sigmoid(acc)
    xs = act[:, 0:D]

    dt = _softplus(dt_ref[...] + dtb_ref[...])
    da = dt * (-jnp.exp(al_ref[...]))
    row = lax.broadcasted_iota(jnp.int32, (BLK, BLK), 0)
    col = lax.broadcasted_iota(jnp.int32, (BLK, BLK), 1)
    lower = row >= col
    tri = lower.astype(F32)
    a_cum = _dot_exact_lhs(tri, da)
    a_cumt = a_cum.T[0:SSD_HEADS, :]
    xdt = xs * _dot_exact_rhs(dt, e_ref[...])
    xdt_b = xdt.astype(BF16)

    lane = lax.broadcasted_iota(jnp.int32, (BLK, LANE), 1)
    first = lane < SSD_HEAD_DIM
    rowp = lax.broadcasted_iota(jnp.int32, (LANE, 1), 0) < SSD_HEAD_DIM
    y_parts = []
    for g in range(SSD_GROUPS):
        b_g = act[:, D + g * SSD_STATE:D + (g + 1) * SSD_STATE]
        c_g = act[:, D + SSD_GROUPS * SSD_STATE + g * SSD_STATE:
                  D + SSD_GROUPS * SSD_STATE + (g + 1) * SSD_STATE]
        b_gb = b_g.astype(BF16)
        c_gb = c_g.astype(BF16)
        cbm = lax.dot_general(c_gb, b_gb, NT, preferred_element_type=F32)
        for pp in range(2):
            pair = 2 * g + pp
            h0 = 2 * pair
            x_pair = xdt_b[:, pair * LANE:(pair + 1) * LANE]
            y_pair = jnp.zeros((BLK, LANE), F32)
            for hh in range(2):
                h = h0 + hh
                lm = jnp.where(lower, jnp.exp(a_cum[:, h:h + 1] - a_cumt[h:h + 1, :]), 0.0)
                w = (cbm * lm).astype(BF16)
                xm = jnp.where(first if hh == 0 else jnp.logical_not(first), x_pair,
                               jnp.zeros_like(x_pair))
                y_pair = y_pair + jnp.dot(w, xm, preferred_element_type=F32)
            ac_pair = jnp.where(first, a_cum[:, h0:h0 + 1], a_cum[:, h0 + 1:h0 + 2])
            al_pair = ac_pair[BLK - 1:BLK, :]
            r_pair = st_ref[pair]
            y_off = lax.dot_general(c_gb, r_pair.astype(BF16), NT, preferred_element_type=F32)
            y_pair = y_pair + y_off * jnp.exp(ac_pair)
            xdec = (xdt[:, pair * LANE:(pair + 1) * LANE] * jnp.exp(al_pair - ac_pair)).astype(BF16)
            st_new = lax.dot_general(xdec, b_gb, TN, preferred_element_type=F32)
            al_col = jnp.where(rowp, a_cumt[h0:h0 + 1, BLK - 1:BLK],
                               a_cumt[h0 + 1:h0 + 2, BLK - 1:BLK])
            st_ref[pair] = jnp.exp(al_col) * r_pair + st_new
            y_parts.append(y_pair)
    y = jnp.concatenate(y_parts, axis=1) + dsk_ref[...] * xs
    z = z_ref[...].astype(F32)
    y = y * (z * _sigmoid(z))
    gw = D // SSD_GROUPS
    for g in range(SSD_GROUPS):
        yg = y[:, g * gw:(g + 1) * gw]
        yg = yg * lax.rsqrt(jnp.mean(yg * yg, axis=-1, keepdims=True) + EPS)
        out_ref[:, g * gw:(g + 1) * gw] = (yg * nw_ref[:, g * gw:(g + 1) * gw]).astype(BF16)


def _ssd(proj, dt, cw, cb, dtb, al, dsk, nw, expand):
    s = proj.shape[0]
    row = lambda c: pl.BlockSpec((BLK, D), lambda i, c=c: (i, c))
    return pl.pallas_call(
        _ssd_kernel,
        grid=(s // BLK,),
        in_specs=[
            row(C_Z), row(C_XS), row(C_BC),
            pl.BlockSpec((BLK, LANE), lambda i: (i, 0)),
            _const_spec((SSD_CONV, 2 * D)), _const_spec((1, 2 * D)),
            _const_spec((1, LANE)), _const_spec((1, LANE)),
            _const_spec((1, D)), _const_spec((1, D)), _const_spec((LANE, D)),
        ],
        out_specs=pl.BlockSpec((BLK, D), lambda i: (i, 0)),
        out_shape=jax.ShapeDtypeStruct((s, D), BF16),
        scratch_shapes=[pltpu.VMEM((BLK + 8, 2 * D), F32),
                        pltpu.VMEM((SSD_HEADS // 2, LANE, SSD_STATE), F32)],
        compiler_params=_cparams(1),
        name="ssd",
    )(proj, proj, proj, dt, cw, cb, dtb, al, dsk, nw, expand)


S5_LAGPAD = 3


def _s5_prep_kernel(lrr_ref, lir_ref, lsr_ref, lrc_ref, lic_ref, lsc_ref,
                    bre_ref, bim_ref, cre_ref, cim_ref, scat_ref,
                    bd_ref, bxr_ref, bxi_ref, cxr_ref, cxi_ref, alr_ref, ali_ref):
    n = S5_L * S5_CH
    rows = 2 * n
    cols = 2 * n
    bd_ref[0, :, 0:S5_LAGPAD * LANE] = jnp.zeros((LANE, S5_LAGPAD * LANE), BF16)
    bxr_ref[...] = jnp.zeros(bxr_ref.shape, BF16)
    bxi_ref[...] = jnp.zeros(bxi_ref.shape, BF16)
    lane128 = lax.broadcasted_iota(jnp.int32, (S5_CH, LANE), 1) // S5_CH
    rk = lax.broadcasted_iota(jnp.int32, (rows, 1), 0)
    ek = (S5_L - 1 - (rk % n) // S5_CH).astype(F32)
    ct = lax.broadcasted_iota(jnp.int32, (1, cols), 1)
    et = ((ct % n) // S5_CH + 1).astype(F32)
    for pr in range(S5_OCT // 2):
        lr = jnp.minimum(lrr_ref[pr], -S5_CLIP)
        li = lir_ref[pr]
        st = jnp.exp(lsr_ref[pr])
        lrs, lis = lr * st, li * st
        mag = jnp.exp(lrs)
        abr, abi = mag * jnp.cos(lis), mag * jnp.sin(lis)
        nr, ni = abr - 1.0, abi
        den = lr * lr + li * li
        cfr = (nr * lr + ni * li) / den
        cfi = (ni * lr - nr * li) / den
        bre, bim = bre_ref[pr], bim_ref[pr]
        bbr = cfr * bre - cfi * bim
        bbi = cfr * bim + cfi * bre
        pm = jnp.exp(ek * lrs)
        pbr, pbi = pm * jnp.cos(ek * lis), pm * jnp.sin(ek * lis)
        bxr = pbr * bbr - pbi * bbi
        bxi = pbr * bbi + pbi * bbr
        alm = jnp.exp(float(S5_L) * lrs)
        alr_ref[pr] = alm * jnp.cos(float(S5_L) * lis)
        ali_ref[pr] = alm * jnp.sin(float(S5_L) * lis)
        lrc = jnp.minimum(lrc_ref[pr], -S5_CLIP)
        stc = jnp.exp(lsc_ref[pr])
        lrsc, lisc = lrc * stc, lic_ref[pr] * stc
        qm = jnp.exp(et * lrsc)
        qr, qi = qm * jnp.cos(et * lisc), qm * jnp.sin(et * lisc)
        cre, cim = cre_ref[pr], cim_ref[pr]
        rsl = slice(pr * LANE, (pr + 1) * LANE)
        cxr_ref[0, rsl, :] = jnp.dot((cre * qr - cim * qi).astype(BF16), scat_ref[pr],
                                     preferred_element_type=F32).astype(BF16)
        cxi_ref[0, rsl, :] = jnp.dot((-(cre * qi + cim * qr)).astype(BF16), scat_ref[pr],
                                     preferred_element_type=F32).astype(BF16)
        rw = (jnp.dot(bxr, cre, precision=HI, preferred_element_type=F32)
              - jnp.dot(bxi, cim, precision=HI, preferred_element_type=F32))
        bxr_b, bxi_b = bxr.astype(BF16), bxi.astype(BF16)
        for gi in range(2):
            g8 = 2 * pr + gi
            for k in range(S5_L):
                src = slice(gi * n + k * S5_CH, gi * n + (k + 1) * S5_CH)
                dst = slice(k * LANE + g8 * S5_CH, k * LANE + (g8 + 1) * S5_CH)
                bxr_ref[0, dst, rsl] = bxr_b[src, :]
                bxi_ref[0, dst, rsl] = bxi_b[src, :]
            for j in range(S5_L):
                m = S5_L - 1 - j
                blk = rw[gi * n + m * S5_CH:gi * n + (m + 1) * S5_CH, gi * n:gi * n + LANE]
                bd_ref[0, g8 * S5_CH:(g8 + 1) * S5_CH,
                       (S5_LAGPAD + j) * LANE:(S5_LAGPAD + j + 1) * LANE] = jnp.where(
                           lane128 == g8, blk, 0.0).astype(BF16)


def _s5_prep(lrr, lir, lsr, lrc, lic, lsc, bre2, bim2, cre2, cim2, scat, layer):
    n = S5_L * S5_CH
    half = S5_OCT // 2
    n_oct = S5_GROUPS // S5_OCT
    w8 = S5_L * LANE
    ow = S5_OCT * S5_STATE
    blk4 = lambda a, b: pl.BlockSpec((half, a, b), lambda i: (layer * n_oct + i, 0, 0))
    out4 = lambda a, b: pl.BlockSpec((half, a, b), lambda i: (i, 0, 0))
    oct3 = lambda a, b: pl.BlockSpec((1, a, b), lambda i: (i, 0, 0))
    return pl.pallas_call(
        _s5_prep_kernel,
        grid=(n_oct,),
        in_specs=[blk4(1, LANE)] * 3 + [blk4(LANE, 1)] * 3
                 + [blk4(2 * n, LANE)] * 2 + [blk4(LANE, 2 * n)] * 2
                 + [_const_spec((half, 2 * n, w8))],
        out_specs=[oct3(LANE, (S5_LAGPAD + S5_L) * LANE),
                   oct3(w8, ow), oct3(w8, ow), oct3(ow, w8), oct3(ow, w8),
                   out4(1, LANE), out4(1, LANE)],
        out_shape=[jax.ShapeDtypeStruct((n_oct, LANE, (S5_LAGPAD + S5_L) * LANE), BF16),
                   jax.ShapeDtypeStruct((n_oct, w8, ow), BF16),
                   jax.ShapeDtypeStruct((n_oct, w8, ow), BF16),
                   jax.ShapeDtypeStruct((n_oct, ow, w8), BF16),
                   jax.ShapeDtypeStruct((n_oct, ow, w8), BF16),
                   jax.ShapeDtypeStruct((S5_PAIRS, 1, LANE), F32),
                   jax.ShapeDtypeStruct((S5_PAIRS, 1, LANE), F32)],
        compiler_params=_cparams(1),
        name="s5_prep",
    )(lrr, lir, lsr, lrc, lic, lsc, bre2, bim2, cre2, cim2, scat)


def _s5_local_kernel(u_ref, bxr_ref, bxi_ref, sr_ref, si_ref):
    sr_ref[...] = jnp.dot(u_ref[0], bxr_ref[0], preferred_element_type=F32)
    si_ref[...] = jnp.dot(u_ref[0], bxi_ref[0], preferred_element_type=F32)


def _s5_local(u8, bxr8, bxi8):
    nc, w = u8.shape[1], u8.shape[2]
    ow = S5_OCT * S5_STATE
    return pl.pallas_call(
        _s5_local_kernel,
        grid=(S5_GROUPS // S5_OCT,),
        in_specs=[pl.BlockSpec((1, nc, w), lambda i: (i, 0, 0)),
                  pl.BlockSpec((1, w, ow), lambda i: (i, 0, 0)),
                  pl.BlockSpec((1, w, ow), lambda i: (i, 0, 0))],
        out_specs=[pl.BlockSpec((nc, ow), lambda i: (0, i))] * 2,
        out_shape=[jax.ShapeDtypeStruct((nc, S5_GROUPS * S5_STATE), F32)] * 2,
        compiler_params=_cparams(1),
        name="s5_local",
    )(u8, bxr8, bxi8)


def _s5_scan_kernel(xr_ref, xi_ref, ar_ref, ai_ref, pr_ref, pi_ref):
    ar, ai = ar_ref[...], ai_ref[...]
    w = ar.shape[1]

    def body(b, carry):
        sr, si = carry
        r0 = pl.multiple_of(b * SUBLANE, SUBLANE)
        xr = xr_ref[pl.ds(r0, SUBLANE), :]
        xi = xi_ref[pl.ds(r0, SUBLANE), :]
        out_r, out_i = [], []
        for j in range(SUBLANE):
            out_r.append(sr)
            out_i.append(si)
            sr, si = ar * sr - ai * si + xr[j:j + 1, :], ar * si + ai * sr + xi[j:j + 1, :]
        pr_ref[pl.ds(r0, SUBLANE), :] = jnp.concatenate(out_r, axis=0)
        pi_ref[pl.ds(r0, SUBLANE), :] = jnp.concatenate(out_i, axis=0)
        return sr, si

    assert xr_ref.shape[0] % SUBLANE == 0
    lax.fori_loop(0, xr_ref.shape[0] // SUBLANE, body,
                  (jnp.zeros((1, w), F32), jnp.zeros((1, w), F32)))


def _s5_scan(xr, xi, ar, ai, tw):
    nc, width = xr.shape
    col = pl.BlockSpec((nc, tw), lambda i: (0, i))
    vec = pl.BlockSpec((1, tw), lambda i: (0, i))
    return pl.pallas_call(
        _s5_scan_kernel,
        grid=(width // tw,),
        in_specs=[col, col, vec, vec],
        out_specs=[col, col],
        out_shape=[jax.ShapeDtypeStruct((nc, width), F32)] * 2,
        compiler_params=_cparams(1),
        name="s5_scan",
    )(xr, xi, ar, ai)


def _s5_out_kernel(u_ref, bd_ref, pr_ref, pi_ref, cxr_ref, cxi_ref, y_ref):
    tt = S5_LAGPAD + 1
    prb, pib = pr_ref[...].astype(BF16), pi_ref[...].astype(BF16)
    for jt in range(S5_L // tt):
        osl = slice(jt * tt * LANE, (jt + 1) * tt * LANE)
        acc = (jnp.dot(prb, cxr_ref[0, :, osl], preferred_element_type=F32)
               + jnp.dot(pib, cxi_ref[0, :, osl], preferred_element_type=F32))
        for k in range((jt + 1) * tt):
            lag0 = jt * tt - k + S5_LAGPAD
            acc = acc + jnp.dot(u_ref[0, :, k * LANE:(k + 1) * LANE],
                                bd_ref[0, :, lag0 * LANE:(lag0 + tt) * LANE],
                                preferred_element_type=F32)
        y_ref[0, :, osl] = acc.astype(BF16)


def _s5_out(u8, bd, pr, pi, cxr8, cxi8):
    n_oct, nc, w = u8.shape
    ow = S5_OCT * S5_STATE
    return pl.pallas_call(
        _s5_out_kernel,
        grid=(n_oct,),
        in_specs=[pl.BlockSpec((1, nc, w), lambda i: (i, 0, 0)),
                  pl.BlockSpec((1, LANE, bd.shape[2]), lambda i: (i, 0, 0)),
                  pl.BlockSpec((nc, ow), lambda i: (0, i)),
                  pl.BlockSpec((nc, ow), lambda i: (0, i)),
                  pl.BlockSpec((1, ow, w), lambda i: (i, 0, 0)),
                  pl.BlockSpec((1, ow, w), lambda i: (i, 0, 0))],
        out_specs=pl.BlockSpec((1, nc, w), lambda i: (i, 0, 0)),
        out_shape=jax.ShapeDtypeStruct((n_oct, nc, w), BF16),
        compiler_params=_cparams(1),
        name="s5_out",
    )(u8, bd, pr, pi, cxr8, cxi8)


def _kmean_kernel(k_ref, o_ref):
    for r in range(o_ref.shape[0]):
        o_ref[r:r + 1, :] = jnp.mean(k_ref[r * BLK:(r + 1) * BLK, :].astype(F32), axis=0, keepdims=True)


def _kmean(proj, rows):
    s = proj.shape[0]
    nb = s // BLK
    return pl.pallas_call(
        _kmean_kernel,
        grid=(nb // rows,),
        in_specs=[pl.BlockSpec((rows * BLK, D), lambda i: (i, C_K))],
        out_specs=pl.BlockSpec((rows, D), lambda i: (i, 0)),
        out_shape=jax.ShapeDtypeStruct((nb, D), F32),
        compiler_params=_cparams(1),
        name="kmean",
    )(proj)


def _moba_kernel(q_ref, k_ref, vt_ref, km_ref, o_ref, bias_ref, s_ref):
    i = pl.program_id(1)
    nb = km_ref.shape[0]
    kg = min(MOBA_KG, nb)
    kt = kg * BLK
    q = q_ref[...].astype(F32)
    lane = lax.broadcasted_iota(jnp.int32, (BLK, LANE), 1)
    blk = lax.broadcasted_iota(jnp.int32, (nb, BLK), 0)
    blk_f = blk.astype(F32)
    scale = MOBA_HD ** -0.5 * math.log2(math.e)
    qs = []
    for h in range(2):
        in_head = (lane >= h * MOBA_HD) & (lane < (h + 1) * MOBA_HD)
        qh = jnp.where(in_head, q, 0.0)
        g = sum(lax.dot_general(t, qh.astype(BF16), NT, preferred_element_type=F32)
                for t in _split3(km_ref[...]))
        g = jnp.where(blk < i, g, -jnp.inf)
        sel = jnp.zeros((nb, BLK), jnp.bool_)
        for _ in range(MOBA_TOPK):
            mx = jnp.max(g, axis=0, keepdims=True)
            idx = jnp.min(jnp.where(g == mx, blk_f, float(nb)), axis=0, keepdims=True)
            hit = blk_f == idx
            sel = sel | (hit & (mx > -jnp.inf))
            g = jnp.where(hit, -jnp.inf, g)
        bias = jnp.where(sel, 0.0, NEG)
        for t in range(nb // kg):
            bias_ref[h, t, 0:kg, :] = bias[t * kg:(t + 1) * kg, :]
        qs.append((qh * scale).astype(BF16))

    krow = lax.broadcasted_iota(jnp.int32, (BLK, BLK), 0)
    qcol = lax.broadcasted_iota(jnp.int32, (BLK, BLK), 1)
    causal = krow <= qcol
    last_g = nb // kg - 1
    ones = lambda n: jnp.ones((16, n), BF16)

    ng = MOBA_TRIP
    last_trip = nb // kg // ng - 1

    def produce(tp, h):
        tcp = jnp.minimum(tp, last_trip)
        kb = k_ref[pl.ds(pl.multiple_of(tcp * ng * kt, ng * kt), ng * kt), :]
        s_t = lax.dot_general(kb, qs[h], NT, preferred_element_type=F32)
        mx = None
        for g in range(ng):
            bias = bias_ref[h, ng * tcp + g, 0:kg, :]
            for u in range(kg):
                rows = slice((g * kg + u) * BLK, (g * kg + u + 1) * BLK)
                su = s_t[rows, :] + bias[u:u + 1, :]
                s_ref[h, rows, :] = su
                mu = jnp.max(su, axis=0, keepdims=True)
                mx = mu if mx is None else jnp.maximum(mx, mu)
        return mx

    def step(tp, carry, width, look_ahead, own=False):
        state, maxima = carry
        start = i * BLK if own else jnp.minimum(tp, last_trip) * (ng * kt)
        vt = vt_ref[:, pl.ds(pl.multiple_of(start, BLK), width)]
        new_state, new_maxima = [], []
        for h in range(2):
            m, acc = state[h]
            mn = jnp.maximum(m, maxima[h])
            p = jnp.exp2(s_ref[h, 0:width, :] - mn).astype(BF16)
            new_maxima.append(produce(tp + 1, h) if look_ahead else maxima[h])
            va = jnp.concatenate([vt[h * MOBA_HD:(h + 1) * MOBA_HD, :], ones(width)], axis=0)
            new_state.append((mn, jnp.exp2(m - mn) * acc + jnp.dot(va, p, preferred_element_type=F32)))
        return tuple(new_state), tuple(new_maxima)

    kb_own = k_ref[pl.ds(pl.multiple_of(i * BLK, BLK), BLK), :]
    own_max = []
    for h in range(2):
        s_own = jnp.where(causal, lax.dot_general(kb_own, qs[h], NT, preferred_element_type=F32), NEG)
        s_ref[h, 0:BLK, :] = s_own
        own_max.append(jnp.max(s_own, axis=0, keepdims=True))
    init = (jnp.full((1, BLK), NEG, F32), jnp.zeros((MOBA_HD + 16, BLK), F32))
    state, maxima = step(-1, ((init, init), tuple(own_max)), BLK, True, own=True)

    n_groups = (i + kg - 1) // kg
    carry = lax.fori_loop(0, n_groups // ng, lambda tp, c: step(tp, c, ng * kt, True),
                          (tuple(state), maxima))
    tails = [lambda c: c] + [functools.partial(lambda r, c: step(n_groups // ng, c, r * kt, False), r)
                             for r in range(1, ng)]
    carry, _ = lax.switch(n_groups % ng, tails, carry)
    for h in range(2):
        _, acc = carry[h]
        o_ref[h * MOBA_HD:(h + 1) * MOBA_HD, :] = (
            acc[0:MOBA_HD, :] / acc[MOBA_HD:MOBA_HD + 1, :]).astype(BF16)


def _moba(proj, vt, kmean):
    s = proj.shape[0]
    nb = s // BLK
    assert nb % (MOBA_TRIP * min(MOBA_KG, nb)) == 0
    per = D // LANE
    return pl.pallas_call(
        _moba_kernel,
        grid=(MOBA_HEADS // 2, nb),
        in_specs=[pl.BlockSpec((BLK, LANE), lambda hp, i: (i, C_Q * per + hp)),
                  pl.BlockSpec((s, LANE), lambda hp, i: (0, C_K * per + hp)),
                  pl.BlockSpec((LANE, s), lambda hp, i: (hp, 0)),
                  pl.BlockSpec((nb, LANE), lambda hp, i: (0, hp))],
        out_specs=pl.BlockSpec((LANE, BLK), lambda hp, i: (hp, i)),
        out_shape=jax.ShapeDtypeStruct((D, s), BF16),
        scratch_shapes=[pltpu.VMEM((2, nb // min(MOBA_KG, nb), SUBLANE, BLK), F32),
                        pltpu.VMEM((2, MOBA_TRIP * min(MOBA_KG, nb) * BLK, BLK), F32)],
        compiler_params=_cparams(2),
        name="moba",
    )(proj, proj, vt, kmean)


def _merge_kernel(final, x_ref, ya_ref, ys_ref, u_ref, sg_ref, att_ref, mg_ref,
                  g0_ref, g1_ref, g2_ref, gb_ref, s5d_ref, glub_ref, fnw_ref,
                  gluw_ref, wa_ref, wb_ref, wc_ref, wo_ref, out_ref):
    mm = lambda a, w_ref: jnp.dot(a.astype(BF16), w_ref[...], preferred_element_type=F32)
    f32 = lambda r: r[...].astype(F32)
    yb = f32(ys_ref) + s5d_ref[...] * f32(u_ref)
    yb = jax.nn.gelu(yb)
    yb = yb * _sigmoid(mm(yb, gluw_ref) + glub_ref[...])
    sg = f32(sg_ref)
    yb = yb * (sg * _sigmoid(sg))
    mg = f32(mg_ref)
    att = f32(att_ref).T * (mg * _sigmoid(mg))
    merged = (_sigmoid(f32(g0_ref) + gb_ref[:, 0:D]) * mm(ya_ref[...], wa_ref)
              + _sigmoid(f32(g1_ref) + gb_ref[:, D:2 * D]) * mm(yb, wb_ref)
              + _sigmoid(f32(g2_ref) + gb_ref[:, 2 * D:3 * D]) * mm(att, wc_ref))
    xn = x_ref[...] + mm(merged, wo_ref)
    if final:
        xn = xn * lax.rsqrt(jnp.mean(xn * xn, axis=-1, keepdims=True) + EPS) * fnw_ref[...]
    out_ref[...] = xn


def _merge(final, x, ya, ys, proj, att_t, gate_b, s5d, glub, fnw, gluw, wa, wb, wc, wo, layer, tm):
    s = x.shape[0]
    rowb = pl.BlockSpec((tm, D), lambda i: (i, 0))
    pcol = lambda c: pl.BlockSpec((tm, D), lambda i, c=c: (i, c))
    wspec = pl.BlockSpec((None, D, D), lambda i: (layer, 0, 0), pipeline_mode=pl.Buffered(1))
    return pl.pallas_call(
        functools.partial(_merge_kernel, final),
        grid=(s // tm,),
        in_specs=[rowb, rowb, rowb, pcol(C_S5U), pcol(C_S5G),
                  pl.BlockSpec((D, tm), lambda i: (0, i)), pcol(C_MG),
                  pcol(C_GL), pcol(C_GL + 1), pcol(C_GL + 2),
                  _const_spec((1, 3 * D)), _const_spec((1, D)), _const_spec((1, D)),
                  _const_spec((1, D)), wspec, wspec, wspec, wspec, wspec],
        out_specs=rowb,
        out_shape=jax.ShapeDtypeStruct((s, D), F32),
        compiler_params=_cparams(1),
        name="merge",
    )(x, ya, ys, proj, proj, att_t, proj, proj, proj, proj,
      gate_b, s5d, glub, fnw, gluw, wa, wb, wc, wo)


def _block_diag_pairs(a):
    g, r, c = a.shape
    a2 = a.reshape(g // 2, 2, r, c)
    out = jnp.einsum("pgrc,gh->pgrhc", a2, jnp.eye(2, dtype=a.dtype))
    return out.reshape(g // 2, 2 * r, 2 * c)


def _lane_scatter():
    half = S5_OCT // 2
    sel = np.zeros((half, 2, S5_OCT), np.float32)
    for pr in range(half):
        for gi in range(2):
            sel[pr, gi, 2 * pr + gi] = 1.0
    eye_t, eye_c = np.eye(S5_L, dtype=np.float32), np.eye(S5_CH, dtype=np.float32)
    sc = np.einsum("pgh,tu,cd->pgtcuhd", sel, eye_t, eye_c)
    return jnp.asarray(sc.reshape(half, 2 * S5_L * S5_CH, S5_L * LANE), dtype=BF16)


def _row_tile(s, want):
    t = min(want, s)
    while s % t:
        t //= 2
    return t


def kernel(x, norm_w, w_in, gate_b, conv_w, conv_b, dt_bias, a_log, ssd_d, ssd_norm_w, w_proj_a,
           lambda_re, lambda_im, log_step, s5_b_re, s5_b_im, s5_c_re, s5_c_im, s5_d, glu_w, glu_b,
           w_proj_b, w_proj_c, w_out, final_norm_w):
    b, s, _ = x.shape
    assert b == 1 and s % BLK == 0 and x.shape[2] == D
    depth = norm_w.shape[0]
    nb = s // BLK
    nc = s // S5_L
    xc = x.reshape(s, D)
    o_dt = 3 * D
    o_s5u = o_dt + SSD_HEADS
    o_q = o_s5u + 2 * D
    o_v = o_q + 2 * D
    o_mg = o_v + D
    expand = (jnp.arange(LANE)[:, None] == (jnp.arange(D)[None, :] // SSD_HEAD_DIM)).astype(F32)
    pad16 = lambda a: jnp.pad(a.astype(F32), (0, LANE - SSD_HEADS)).reshape(1, LANE)
    tile_b = lambda a: jnp.tile(a.transpose(0, 2, 1), (1, S5_L, 1))
    tile_c = lambda a: jnp.tile(a.transpose(0, 2, 1), (1, 1, S5_L))
    scat = _lane_scatter()
    w_nat = jnp.concatenate([w_in[:, :, 0:o_dt], w_in[:, :, o_s5u:o_v], w_in[:, :, o_mg:]],
                            axis=2).astype(BF16)
    w_vt = jnp.swapaxes(w_in[:, :, o_v:o_mg], 1, 2).astype(BF16)
    w_dt = jnp.pad(w_in[:, :, o_dt:o_s5u], ((0, 0), (0, 0), (0, LANE - SSD_HEADS)))
    merge_w = [a.astype(BF16) for a in (glu_w, w_proj_a, w_proj_b, w_proj_c, w_out)]
    pairs = depth * S5_PAIRS
    ls_full = jnp.repeat(log_step.reshape(-1), S5_STATE)
    s5_rows = [a.reshape(pairs, 1, LANE) for a in (lambda_re, lambda_im, ls_full)]
    s5_cols = [a.reshape(pairs, LANE, 1) for a in (lambda_re, lambda_im, ls_full)]
    s5_mats = ([_block_diag_pairs(tile_b(a.reshape(depth * S5_GROUPS, S5_STATE, S5_CH)))
                for a in (s5_b_re, s5_b_im)]
               + [_block_diag_pairs(tile_c(a.reshape(depth * S5_GROUPS, S5_CH, S5_STATE)))
                  for a in (s5_c_re, s5_c_im)])

    for l in range(depth):
        proj, vt, dt = _project(xc, norm_w[l].reshape(1, D), w_nat, w_vt, w_dt, l,
                                     _row_tile(s, 1024))

        ya = _ssd(proj, dt, conv_w[l], conv_b[l].reshape(1, 2 * D),
                  pad16(dt_bias[l]), pad16(a_log[l]),
                  jnp.repeat(ssd_d[l], SSD_HEAD_DIM).reshape(1, D),
                  ssd_norm_w[l].reshape(1, D), expand)

        bd, bxr8, bxi8, cxr8, cxi8, alr, ali = _s5_prep(*s5_rows, *s5_cols, *s5_mats, scat, l)
        n_oct = S5_GROUPS // S5_OCT
        u8 = (proj[:, C_S5U * D:(C_S5U + 1) * D].reshape(nc, S5_L, n_oct, LANE)
              .transpose(2, 0, 1, 3).reshape(n_oct, nc, S5_L * LANE))
        sr, si = _s5_local(u8, bxr8, bxi8)
        pr, pi = _s5_scan(sr, si, alr.reshape(1, -1), ali.reshape(1, -1), 512)
        y8 = _s5_out(u8, bd, pr, pi, cxr8, cxi8)
        ys = y8.reshape(n_oct, nc, S5_L, LANE).transpose(1, 2, 0, 3).reshape(s, D)

        km = _kmean(proj, math.gcd(nb, 8))
        att_t = _moba(proj, vt, km)

        xc = _merge(l == depth - 1, xc, ya, ys, proj, att_t, gate_b[l].reshape(1, 3 * D),
                    s5_d[l].reshape(1, D), glu_b[l].reshape(1, D), final_norm_w.reshape(1, D),
                    *merge_w, l, _row_tile(s, 512))
    return xc.reshape(b, s, D)
```

```python
import functools
import math

import jax
import jax.numpy as jnp
import numpy as np
from jax import lax
from jax.experimental import pallas as pl
from jax.experimental.pallas import tpu as pltpu

F32 = jnp.float32
BF16 = jnp.bfloat16
HI = lax.Precision.HIGHEST

D = 1024
EPS = 1e-6
BLK = 256
SSD_HEADS = 16
SSD_HEAD_DIM = 64
SSD_GROUPS = 4
SSD_STATE = 128
SSD_CONV = 4
S5_GROUPS = 64
S5_CH = 16
S5_STATE = 64
S5_CLIP = 1e-4
S5_L = 16
S5_PAIRS = S5_GROUPS // 2
S5_OCT = 8
MOBA_HEADS = 16
MOBA_HD = 64
MOBA_TOPK = 3
MOBA_KG = 4
MOBA_TRIP = 2
NEG = -1e30
LANE = 128
SUBLANE = 8
VMEM_LIMIT = 56 * 1024 * 1024

C_Z, C_XS, C_BC, C_S5U, C_S5G, C_Q, C_K, C_MG, C_GL = 0, 1, 2, 3, 4, 5, 6, 7, 8
N_NAT = 11 * D

NT = (((1,), (1,)), ((), ()))
TN = (((0,), (0,)), ((), ()))


def _cparams(n_axes):
    return pltpu.CompilerParams(dimension_semantics=("arbitrary",) * n_axes,
                                vmem_limit_bytes=VMEM_LIMIT)


def _const_spec(shape):
    nd = len(shape)
    return pl.BlockSpec(shape, lambda *_: (0,) * nd)


def _proj_kernel(x_ref, nw_ref, w_ref, wvt_ref, wdt_ref,
                 proj_ref, vt_ref, dt_ref, h_ref):
    @pl.when(pl.program_id(1) == 0)
    def _():
        x = x_ref[...]
        h = x * lax.rsqrt(jnp.mean(x * x, axis=-1, keepdims=True) + EPS) * nw_ref[...]
        hb = h.astype(BF16)
        h_ref[...] = hb
        vt = lax.dot_general(wvt_ref[...], hb, NT, preferred_element_type=F32)
        vt_ref[...] = vt.astype(BF16)
        dt_ref[...] = jnp.dot(h, wdt_ref[...], precision=HI, preferred_element_type=F32)

    proj_ref[...] = jnp.dot(h_ref[...], w_ref[...], preferred_element_type=F32).astype(BF16)


def _project(x, norm_w, w_nat, w_vt, w_dt, layer, tm):
    s = x.shape[0]
    grid = (s // tm, N_NAT // D)
    return pl.pallas_call(
        _proj_kernel,
        grid=grid,
        in_specs=[
            pl.BlockSpec((tm, D), lambda i, j: (i, 0)),
            _const_spec((1, D)),
            pl.BlockSpec((None, D, D), lambda i, j: (layer, 0, j)),
            pl.BlockSpec((None, D, D), lambda i, j: (layer, 0, 0)),
            pl.BlockSpec((None, D, LANE), lambda i, j: (layer, 0, 0)),
        ],
        out_specs=[
            pl.BlockSpec((tm, D), lambda i, j: (i, j)),
            pl.BlockSpec((D, tm), lambda i, j: (0, i)),
            pl.BlockSpec((tm, LANE), lambda i, j: (i, 0)),
        ],
        out_shape=[
            jax.ShapeDtypeStruct((s, N_NAT), BF16),
            jax.ShapeDtypeStruct((D, s), BF16),
            jax.ShapeDtypeStruct((s, LANE), F32),
        ],
        scratch_shapes=[pltpu.VMEM((tm, D), BF16)],
        compiler_params=_cparams(2),
        name="proj",
    )(x, norm_w, w_nat, w_vt, w_dt)


def _sigmoid(v):
    return 0.5 * jnp.tanh(0.5 * v) + 0.5


def _split3(v):
    hi = v.astype(BF16)
    r1 = v - hi.astype(F32)
    mid = r1.astype(BF16)
    lo = (r1 - mid.astype(F32)).astype(BF16)
    return hi, mid, lo


def _dot_exact_lhs(a_exact, v):
    ab = a_exact.astype(BF16)
    return sum(jnp.dot(ab, t, preferred_element_type=F32) for t in _split3(v))


def _dot_exact_rhs(v, b_exact):
    bb = b_exact.astype(BF16)
    return sum(jnp.dot(t, bb, preferred_element_type=F32) for t in _split3(v))


def _softplus(v):
    return jnp.maximum(v, 0.0) + jnp.log1p(jnp.exp(-jnp.abs(v)))


def _ssd_kernel(z_ref, xs_ref, bc_ref, dt_ref, cw_ref, cb_ref, dtb_ref,
                al_ref, dsk_ref, nw_ref, e_ref, out_ref, ext_ref, st_ref):
    @pl.when(pl.program_id(0) == 0)
    def _():
        ext_ref[0:8, :] = jnp.zeros((8, 2 * D), F32)
        st_ref[...] = jnp.zeros(st_ref.shape, F32)

    ext_ref[8:8 + BLK, 0:D] = xs_ref[...].astype(F32)
    ext_ref[8:8 + BLK, D:2 * D] = bc_ref[...].astype(F32)
    acc = cb_ref[...] + cw_ref[0:1, :] * ext_ref[5:5 + BLK, :]
    for i in range(1, SSD_CONV):
        acc = acc + cw_ref[i:i + 1, :] * ext_ref[5 + i:5 + i + BLK, :]
    ext_ref[0:8, :] = ext_ref[BLK:BLK + 8, :]
    act = acc * _sigmoid(acc)
    xs = act[:, 0:D]

    dt = _softplus(dt_ref[...] + dtb_ref[...])
    da = dt * (-jnp.exp(al_ref[...]))
    row = lax.broadcasted_iota(jnp.int32, (BLK, BLK), 0)
    col = lax.broadcasted_iota(jnp.int32, (BLK, BLK), 1)
    lower = row >= col
    tri = lower.astype(F32)
    a_cum = _dot_exact_lhs(tri, da)
    a_cumt = a_cum.T[0:SSD_HEADS, :]
    xdt = xs * _dot_exact_rhs(dt, e_ref[...])
    xdt_b = xdt.astype(BF16)

    lane = lax.broadcasted_iota(jnp.int32, (BLK, LANE), 1)
    first = lane < SSD_HEAD_DIM
    rowp = lax.broadcasted_iota(jnp.int32, (LANE, 1), 0) < SSD_HEAD_DIM
    y_parts = []
    for g in range(SSD_GROUPS):
        b_g = act[:, D + g * SSD_STATE:D + (g + 1) * SSD_STATE]
        c_g = act[:, D + SSD_GROUPS * SSD_STATE + g * SSD_STATE:
                  D + SSD_GROUPS * SSD_STATE + (g + 1) * SSD_STATE]
        b_gb = b_g.astype(BF16)
        c_gb = c_g.astype(BF16)
        cbm = lax.dot_general(c_gb, b_gb, NT, preferred_element_type=F32)
        for pp in range(2):
            pair = 2 * g + pp
            h0 = 2 * pair
            x_pair = xdt_b[:, pair * LANE:(pair + 1) * LANE]
            y_pair = jnp.zeros((BLK, LANE), F32)
            for hh in range(2):
                h = h0 + hh
                lm = jnp.where(lower, jnp.exp(a_cum[:, h:h + 1] - a_cumt[h:h + 1, :]), 0.0)
                w = (cbm * lm).astype(BF16)
                xm = jnp.where(first if hh == 0 else jnp.logical_not(first), x_pair,
                               jnp.zeros_like(x_pair))
                y_pair = y_pair + jnp.dot(w, xm, preferred_element_type=F32)
            ac_pair = jnp.where(first, a_cum[:, h0:h0 + 1], a_cum[:, h0 + 1:h0 + 2])
            al_pair = ac_pair[BLK - 1:BLK, :]
            r_pair = st_ref[pair]
            y_off = lax.dot_general(c_gb, r_pair.astype(BF16), NT, preferred_element_type=F32)
            y_pair = y_pair + y_off * jnp.exp(ac_pair)
            xdec = (xdt[:, pair * LANE:(pair + 1) * LANE] * jnp.exp(al_pair - ac_pair)).astype(BF16)
            st_new = lax.dot_general(xdec, b_gb, TN, preferred_element_type=F32)
            al_col = jnp.where(rowp, a_cumt[h0:h0 + 1, BLK - 1:BLK],
                               a_cumt[h0 + 1:h0 + 2, BLK - 1:BLK])
            st_ref[pair] = jnp.exp(al_col) * r_pair + st_new
            y_parts.append(y_pair)
    y = jnp.concatenate(y_parts, axis=1) + dsk_ref[...] * xs
    z = z_ref[...].astype(F32)
    y = y * (z * _sigmoid(z))
    gw = D // SSD_GROUPS
    for g in range(SSD_GROUPS):
        yg = y[:, g * gw:(g + 1) * gw]
        yg = yg * lax.rsqrt(jnp.mean(yg * yg, axis=-1, keepdims=True) + EPS)
        out_ref[:, g * gw:(g + 1) * gw] = (yg * nw_ref[:, g * gw:(g + 1) * gw]).astype(BF16)


def _ssd(proj, dt, cw, cb, dtb, al, dsk, nw, expand):
    s = proj.shape[0]
    row = lambda c: pl.BlockSpec((BLK, D), lambda i, c=c: (i, c))
    return pl.pallas_call(
        _ssd_kernel,
        grid=(s // BLK,),
        in_specs=[
            row(C_Z), row(C_XS), row(C_BC),
            pl.BlockSpec((BLK, LANE), lambda i: (i, 0)),
            _const_spec((SSD_CONV, 2 * D)), _const_spec((1, 2 * D)),
            _const_spec((1, LANE)), _const_spec((1, LANE)),
            _const_spec((1, D)), _const_spec((1, D)), _const_spec((LANE, D)),
        ],
        out_specs=pl.BlockSpec((BLK, D), lambda i: (i, 0)),
        out_shape=jax.ShapeDtypeStruct((s, D), BF16),
        scratch_shapes=[pltpu.VMEM((BLK + 8, 2 * D), F32),
                        pltpu.VMEM((SSD_HEADS // 2, LANE, SSD_STATE), F32)],
        compiler_params=_cparams(1),
        name="ssd",
    )(proj, proj, proj, dt, cw, cb, dtb, al, dsk, nw, expand)


S5_LAGPAD = 3


def _s5_prep_kernel(lrr_ref, lir_ref, lsr_ref, lrc_ref, lic_ref, lsc_ref,
                    bre_ref, bim_ref, cre_ref, cim_ref, scat_ref,
                    bd_ref, bxr_ref, bxi_ref, cxr_ref, cxi_ref, alr_ref, ali_ref):
    n = S5_L * S5_CH
    rows = 2 * n
    cols = 2 * n
    bd_ref[0, :, 0:S5_LAGPAD * LANE] = jnp.zeros((LANE, S5_LAGPAD * LANE), BF16)
    bxr_ref[...] = jnp.zeros(bxr_ref.shape, BF16)
    bxi_ref[...] = jnp.zeros(bxi_ref.shape, BF16)
    lane128 = lax.broadcasted_iota(jnp.int32, (S5_CH, LANE), 1) // S5_CH
    rk = lax.broadcasted_iota(jnp.int32, (rows, 1), 0)
    ek = (S5_L - 1 - (rk % n) // S5_CH).astype(F32)
    ct = lax.broadcasted_iota(jnp.int32, (1, cols), 1)
    et = ((ct % n) // S5_CH + 1).astype(F32)
    for pr in range(S5_OCT // 2):
        lr = jnp.minimum(lrr_ref[pr], -S5_CLIP)
        li = lir_ref[pr]
        st = jnp.exp(lsr_ref[pr])
        lrs, lis = lr * st, li * st
        mag = jnp.exp(lrs)
        abr, abi = mag * jnp.cos(lis), mag * jnp.sin(lis)
        nr, ni = abr - 1.0, abi
        den = lr * lr + li * li
        cfr = (nr * lr + ni * li) / den
        cfi = (ni * lr - nr * li) / den
        bre, bim = bre_ref[pr], bim_ref[pr]
        bbr = cfr * bre - cfi * bim
        bbi = cfr * bim + cfi * bre
        pm = jnp.exp(ek * lrs)
        pbr, pbi = pm * jnp.cos(ek * lis), pm * jnp.sin(ek * lis)
        bxr = pbr * bbr - pbi * bbi
        bxi = pbr * bbi + pbi * bbr
        alm = jnp.exp(float(S5_L) * lrs)
        alr_ref[pr] = alm * jnp.cos(float(S5_L) * lis)
        ali_ref[pr] = alm * jnp.sin(float(S5_L) * lis)
        lrc = jnp.minimum(lrc_ref[pr], -S5_CLIP)
        stc = jnp.exp(lsc_ref[pr])
        lrsc, lisc = lrc * stc, lic_ref[pr] * stc
        qm = jnp.exp(et * lrsc)
        qr, qi = qm * jnp.cos(et * lisc), qm * jnp.sin(et * lisc)
        cre, cim = cre_ref[pr], cim_ref[pr]
        rsl = slice(pr * LANE, (pr + 1) * LANE)
        cxr_ref[0, rsl, :] = jnp.dot((cre * qr - cim * qi).astype(BF16), scat_ref[pr],
                                     preferred_element_type=F32).astype(BF16)
        cxi_ref[0, rsl, :] = jnp.dot((-(cre * qi + cim * qr)).astype(BF16), scat_ref[pr],
                                     preferred_element_type=F32).astype(BF16)
        rw = (jnp.dot(bxr, cre, precision=HI, preferred_element_type=F32)
              - jnp.dot(bxi, cim, precision=HI, preferred_element_type=F32))
        bxr_b, bxi_b = bxr.astype(BF16), bxi.astype(BF16)
        for gi in range(2):
            g8 = 2 * pr + gi
            for k in range(S5_L):
                src = slice(gi * n + k * S5_CH, gi * n + (k + 1) * S5_CH)
                dst = slice(k * LANE + g8 * S5_CH, k * LANE + (g8 + 1) * S5_CH)
                bxr_ref[0, dst, rsl] = bxr_b[src, :]
                bxi_ref[0, dst, rsl] = bxi_b[src, :]
            for j in range(S5_L):
                m = S5_L - 1 - j
                blk = rw[gi * n + m * S5_CH:gi * n + (m + 1) * S5_CH, gi * n:gi * n + LANE]
                bd_ref[0, g8 * S5_CH:(g8 + 1) * S5_CH,
                       (S5_LAGPAD + j) * LANE:(S5_LAGPAD + j + 1) * LANE] = jnp.where(
                           lane128 == g8, blk, 0.0).astype(BF16)


def _s5_prep(lrr, lir, lsr, lrc, lic, lsc, bre2, bim2, cre2, cim2, scat, layer):
    n = S5_L * S5_CH
    half = S5_OCT // 2
    n_oct = S5_GROUPS // S5_OCT
    w8 = S5_L * LANE
    ow = S5_OCT * S5_STATE
    blk4 = lambda a, b: pl.BlockSpec((half, a, b), lambda i: (layer * n_oct + i, 0, 0))
    out4 = lambda a, b: pl.BlockSpec((half, a, b), lambda i: (i, 0, 0))
    oct3 = lambda a, b: pl.BlockSpec((1, a, b), lambda i: (i, 0, 0))
    return pl.pallas_call(
        _s5_prep_kernel,
        grid=(n_oct,),
        in_specs=[blk4(1, LANE)] * 3 + [blk4(LANE, 1)] * 3
                 + [blk4(2 * n, LANE)] * 2 + [blk4(LANE, 2 * n)] * 2
                 + [_const_spec((half, 2 * n, w8))],
        out_specs=[oct3(LANE, (S5_LAGPAD + S5_L) * LANE),
                   oct3(w8, ow), oct3(w8, ow), oct3(ow, w8), oct3(ow, w8),
                   out4(1, LANE), out4(1, LANE)],
        out_shape=[jax.ShapeDtypeStruct((n_oct, LANE, (S5_LAGPAD + S5_L) * LANE), BF16),
                   jax.ShapeDtypeStruct((n_oct, w8, ow), BF16),
                   jax.ShapeDtypeStruct((n_oct, w8, ow), BF16),
                   jax.ShapeDtypeStruct((n_oct, ow, w8), BF16),
                   jax.ShapeDtypeStruct((n_oct, ow, w8), BF16),
                   jax.ShapeDtypeStruct((S5_PAIRS, 1, LANE), F32),
                   jax.ShapeDtypeStruct((S5_PAIRS, 1, LANE), F32)],
        compiler_params=_cparams(1),
        name="s5_prep",
    )(lrr, lir, lsr, lrc, lic, lsc, bre2, bim2, cre2, cim2, scat)


def _s5_local_kernel(u_ref, bxr_ref, bxi_ref, sr_ref, si_ref):
    sr_ref[...] = jnp.dot(u_ref[0], bxr_ref[0], preferred_element_type=F32)
    si_ref[...] = jnp.dot(u_ref[0], bxi_ref[0], preferred_element_type=F32)


def _s5_local(u8, bxr8, bxi8):
    nc, w = u8.shape[1], u8.shape[2]
    ow = S5_OCT * S5_STATE
    return pl.pallas_call(
        _s5_local_kernel,
        grid=(S5_GROUPS // S5_OCT,),
        in_specs=[pl.BlockSpec((1, nc, w), lambda i: (i, 0, 0)),
                  pl.BlockSpec((1, w, ow), lambda i: (i, 0, 0)),
                  pl.BlockSpec((1, w, ow), lambda i: (i, 0, 0))],
        out_specs=[pl.BlockSpec((nc, ow), lambda i: (0, i))] * 2,
        out_shape=[jax.ShapeDtypeStruct((nc, S5_GROUPS * S5_STATE), F32)] * 2,
        compiler_params=_cparams(1),
        name="s5_local",
    )(u8, bxr8, bxi8)


def _s5_scan_kernel(xr_ref, xi_ref, ar_ref, ai_ref, pr_ref, pi_ref):
    ar, ai = ar_ref[...], ai_ref[...]
    w = ar.shape[1]

    def body(b, carry):
        sr, si = carry
        r0 = pl.multiple_of(b * SUBLANE, SUBLANE)
        xr = xr_ref[pl.ds(r0, SUBLANE), :]
        xi = xi_ref[pl.ds(r0, SUBLANE), :]
        out_r, out_i = [], []
        for j in range(SUBLANE):
            out_r.append(sr)
            out_i.append(si)
            sr, si = ar * sr - ai * si + xr[j:j + 1, :], ar * si + ai * sr + xi[j:j + 1, :]
        pr_ref[pl.ds(r0, SUBLANE), :] = jnp.concatenate(out_r, axis=0)
        pi_ref[pl.ds(r0, SUBLANE), :] = jnp.concatenate(out_i, axis=0)
        return sr, si

    assert xr_ref.shape[0] % SUBLANE == 0
    lax.fori_loop(0, xr_ref.shape[0] // SUBLANE, body,
                  (jnp.zeros((1, w), F32), jnp.zeros((1, w), F32)))


def _s5_scan(xr, xi, ar, ai, tw):
    nc, width = xr.shape
    col = pl.BlockSpec((nc, tw), lambda i: (0, i))
    vec = pl.BlockSpec((1, tw), lambda i: (0, i))
    return pl.pallas_call(
        _s5_scan_kernel,
        grid=(width // tw,),
        in_specs=[col, col, vec, vec],
        out_specs=[col, col],
        out_shape=[jax.ShapeDtypeStruct((nc, width), F32)] * 2,
        compiler_params=_cparams(1),
        name="s5_scan",
    )(xr, xi, ar, ai)


def _s5_out_kernel(u_ref, bd_ref, pr_ref, pi_ref, cxr_ref, cxi_ref, y_ref):
    tt = S5_LAGPAD + 1
    prb, pib = pr_ref[...].astype(BF16), pi_ref[...].astype(BF16)
    for jt in range(S5_L // tt):
        osl = slice(jt * tt * LANE, (jt + 1) * tt * LANE)
        acc = (jnp.dot(prb, cxr_ref[0, :, osl], preferred_element_type=F32)
               + jnp.dot(pib, cxi_ref[0, :, osl], preferred_element_type=F32))
        for k in range((jt + 1) * tt):
            lag0 = jt * tt - k + S5_LAGPAD
            acc = acc + jnp.dot(u_ref[0, :, k * LANE:(k + 1) * LANE],
                                bd_ref[0, :, lag0 * LANE:(lag0 + tt) * LANE],
                                preferred_element_type=F32)
        y_ref[0, :, osl] = acc.astype(BF16)


def _s5_out(u8, bd, pr, pi, cxr8, cxi8):
    n_oct, nc, w = u8.shape
    ow = S5_OCT * S5_STATE
    return pl.pallas_call(
        _s5_out_kernel,
        grid=(n_oct,),
        in_specs=[pl.BlockSpec((1, nc, w), lambda i: (i, 0, 0)),
                  pl.BlockSpec((1, LANE, bd.shape[2]), lambda i: (i, 0, 0)),
                  pl.BlockSpec((nc, ow), lambda i: (0, i)),
                  pl.BlockSpec((nc, ow), lambda i: (0, i)),
                  pl.BlockSpec((1, ow, w), lambda i: (i, 0, 0)),
                  pl.BlockSpec((1, ow, w), lambda i: (i, 0, 0))],
        out_specs=pl.BlockSpec((1, nc, w), lambda i: (i, 0, 0)),
        out_shape=jax.ShapeDtypeStruct((n_oct, nc, w), BF16),
        compiler_params=_cparams(1),
        name="s5_out",
    )(u8, bd, pr, pi, cxr8, cxi8)


def _kprep_kernel(k_ref, km_ref, ka_ref):
    rows = km_ref.shape[0]
    lane = lax.broadcasted_iota(jnp.int32, (BLK, D), 1) % LANE
    even = lane < MOBA_HD
    for r in range(rows):
        k = k_ref[r * BLK:(r + 1) * BLK, :]
        km_ref[r:r + 1, :] = jnp.mean(k.astype(F32), axis=0, keepdims=True)
        blk = pl.program_id(0) * rows + r
        ka_ref[0, r * BLK:(r + 1) * BLK, :] = jnp.where(
            even, k, (lane - MOBA_HD == blk).astype(BF16))
        ka_ref[1, r * BLK:(r + 1) * BLK, :] = jnp.where(
            even, (lane == blk).astype(BF16), k)


def _kprep(proj, rows):
    s = proj.shape[0]
    nb = s // BLK
    assert nb <= MOBA_HD
    return pl.pallas_call(
        _kprep_kernel,
        grid=(nb // rows,),
        in_specs=[pl.BlockSpec((rows * BLK, D), lambda i: (i, C_K))],
        out_specs=[pl.BlockSpec((rows, D), lambda i: (i, 0)),
                   pl.BlockSpec((2, rows * BLK, D), lambda i: (0, i, 0))],
        out_shape=[jax.ShapeDtypeStruct((nb, D), F32),
                   jax.ShapeDtypeStruct((2, s, D), BF16)],
        compiler_params=_cparams(1),
        name="kprep",
    )(proj)


def _moba_kernel(q_ref, ka_ref, vt_ref, km_ref, o_ref, s_ref):
    i = pl.program_id(1)
    nb = km_ref.shape[0]
    kg = min(MOBA_KG, nb)
    kt = kg * BLK
    q = q_ref[...].astype(F32)
    lane = lax.broadcasted_iota(jnp.int32, (BLK, LANE), 1)
    blk = lax.broadcasted_iota(jnp.int32, (nb, BLK), 0)
    blk_f = blk.astype(F32)
    scale = MOBA_HD ** -0.5 * math.log2(math.e)
    qs, qs_own = [], []
    for h in range(2):
        in_head = (lane >= h * MOBA_HD) & (lane < (h + 1) * MOBA_HD)
        qh = jnp.where(in_head, q, 0.0)
        g = sum(lax.dot_general(t, qh.astype(BF16), NT, preferred_element_type=F32)
                for t in _split3(km_ref[...]))
        g = jnp.where(blk < i, g, -jnp.inf)
        sel = jnp.zeros((nb, BLK), jnp.bool_)
        for _ in range(MOBA_TOPK):
            mx = jnp.max(g, axis=0, keepdims=True)
            idx = jnp.min(jnp.where(g == mx, blk_f, float(nb)), axis=0, keepdims=True)
            hit = blk_f == idx
            sel = sel | (hit & (mx > -jnp.inf))
            g = jnp.where(hit, -jnp.inf, g)
        bias = jnp.where(sel, 0.0, NEG)
        pads = [jnp.full((MOBA_HD - nb, BLK), NEG, F32)] if nb < MOBA_HD else []
        other = jnp.zeros((MOBA_HD, BLK), F32)
        bias_q = jnp.concatenate(([other, bias] + pads) if h == 0 else ([bias] + pads + [other]),
                                 axis=0).T
        qs.append(jnp.where(in_head, qh * scale, bias_q).astype(BF16))
        qs_own.append((qh * scale).astype(BF16))

    krow = lax.broadcasted_iota(jnp.int32, (BLK, BLK), 0)
    qcol = lax.broadcasted_iota(jnp.int32, (BLK, BLK), 1)
    causal = krow <= qcol
    last_g = nb // kg - 1
    ones = lambda n: jnp.ones((16, n), BF16)

    ng = MOBA_TRIP
    last_trip = nb // kg // ng - 1

    def produce(tp, h):
        tcp = jnp.minimum(tp, last_trip)
        kb = ka_ref[h, pl.ds(pl.multiple_of(tcp * ng * kt, ng * kt), ng * kt), :]
        s_t = lax.dot_general(kb, qs[h], NT, preferred_element_type=F32)
        s_ref[h] = s_t
        return jnp.max(s_t, axis=0, keepdims=True)

    def step(tp, carry, width, look_ahead, own=False):
        state, maxima = carry
        start = i * BLK if own else jnp.minimum(tp, last_trip) * (ng * kt)
        vt = vt_ref[:, pl.ds(pl.multiple_of(start, BLK), width)]
        new_state, new_maxima = [], []
        for h in range(2):
            m, acc = state[h]
            mn = jnp.maximum(m, maxima[h])
            p = jnp.exp2(s_ref[h, 0:width, :] - mn).astype(BF16)
            new_maxima.append(produce(tp + 1, h) if look_ahead else maxima[h])
            va = jnp.concatenate([vt[h * MOBA_HD:(h + 1) * MOBA_HD, :], ones(width)], axis=0)
            new_state.append((mn, jnp.exp2(m - mn) * acc + jnp.dot(va, p, preferred_element_type=F32)))
        return tuple(new_state), tuple(new_maxima)

    own_max = []
    for h in range(2):
        kb_own = ka_ref[h, pl.ds(pl.multiple_of(i * BLK, BLK), BLK), :]
        s_own = jnp.where(causal, lax.dot_general(kb_own, qs_own[h], NT, preferred_element_type=F32), NEG)
        s_ref[h, 0:BLK, :] = s_own
        own_max.append(jnp.max(s_own, axis=0, keepdims=True))
    init = (jnp.full((1, BLK), NEG, F32), jnp.zeros((MOBA_HD + 16, BLK), F32))
    state, maxima = step(-1, ((init, init), tuple(own_max)), BLK, True, own=True)

    n_groups = (i + kg - 1) // kg
    carry = lax.fori_loop(0, n_groups // ng, lambda tp, c: step(tp, c, ng * kt, True),
                          (tuple(state), maxima))
    tails = [lambda c: c] + [functools.partial(lambda r, c: step(n_groups // ng, c, r * kt, False), r)
                             for r in range(1, ng)]
    carry, _ = lax.switch(n_groups % ng, tails, carry)
    for h in range(2):
        _, acc = carry[h]
        o_ref[h * MOBA_HD:(h + 1) * MOBA_HD, :] = (
            acc[0:MOBA_HD, :] / acc[MOBA_HD:MOBA_HD + 1, :]).astype(BF16)


def _moba(proj, kaug, vt, kmean):
    s = proj.shape[0]
    nb = s // BLK
    assert nb % (MOBA_TRIP * min(MOBA_KG, nb)) == 0
    per = D // LANE
    return pl.pallas_call(
        _moba_kernel,
        grid=(MOBA_HEADS // 2, nb),
        in_specs=[pl.BlockSpec((BLK, LANE), lambda hp, i: (i, C_Q * per + hp)),
                  pl.BlockSpec((2, s, LANE), lambda hp, i: (0, 0, hp)),
                  pl.BlockSpec((LANE, s), lambda hp, i: (hp, 0)),
                  pl.BlockSpec((nb, LANE), lambda hp, i: (0, hp))],
        out_specs=pl.BlockSpec((LANE, BLK), lambda hp, i: (hp, i)),
        out_shape=jax.ShapeDtypeStruct((D, s), BF16),
        scratch_shapes=[pltpu.VMEM((2, MOBA_TRIP * min(MOBA_KG, nb) * BLK, BLK), F32)],
        compiler_params=_cparams(2),
        name="moba",
    )(proj, kaug, vt, kmean)


def _merge_kernel(final, x_ref, ya_ref, ys_ref, u_ref, sg_ref, att_ref, mg_ref,
                  g0_ref, g1_ref, g2_ref, gb_ref, s5d_ref, glub_ref, fnw_ref,
                  gluw_ref, wa_ref, wb_ref, wc_ref, wo_ref, out_ref):
    mm = lambda a, w_ref: jnp.dot(a.astype(BF16), w_ref[...], preferred_element_type=F32)
    f32 = lambda r: r[...].astype(F32)
    yb = f32(ys_ref) + s5d_ref[...] * f32(u_ref)
    yb = jax.nn.gelu(yb)
    yb = yb * _sigmoid(mm(yb, gluw_ref) + glub_ref[...])
    sg = f32(sg_ref)
    yb = yb * (sg * _sigmoid(sg))
    mg = f32(mg_ref)
    att = f32(att_ref).T * (mg * _sigmoid(mg))
    merged = (_sigmoid(f32(g0_ref) + gb_ref[:, 0:D]) * mm(ya_ref[...], wa_ref)
              + _sigmoid(f32(g1_ref) + gb_ref[:, D:2 * D]) * mm(yb, wb_ref)
              + _sigmoid(f32(g2_ref) + gb_ref[:, 2 * D:3 * D]) * mm(att, wc_ref))
    xn = x_ref[...] + mm(merged, wo_ref)
    if final:
        xn = xn * lax.rsqrt(jnp.mean(xn * xn, axis=-1, keepdims=True) + EPS) * fnw_ref[...]
    out_ref[...] = xn


def _merge(final, x, ya, ys, proj, att_t, gate_b, s5d, glub, fnw, gluw, wa, wb, wc, wo, layer, tm):
    s = x.shape[0]
    rowb = pl.BlockSpec((tm, D), lambda i: (i, 0))
    pcol = lambda c: pl.BlockSpec((tm, D), lambda i, c=c: (i, c))
    wspec = pl.BlockSpec((None, D, D), lambda i: (layer, 0, 0), pipeline_mode=pl.Buffered(1))
    return pl.pallas_call(
        functools.partial(_merge_kernel, final),
        grid=(s // tm,),
        in_specs=[rowb, rowb, rowb, pcol(C_S5U), pcol(C_S5G),
                  pl.BlockSpec((D, tm), lambda i: (0, i)), pcol(C_MG),
                  pcol(C_GL), pcol(C_GL + 1), pcol(C_GL + 2),
                  _const_spec((1, 3 * D)), _const_spec((1, D)), _const_spec((1, D)),
                  _const_spec((1, D)), wspec, wspec, wspec, wspec, wspec],
        out_specs=rowb,
        out_shape=jax.ShapeDtypeStruct((s, D), F32),
        compiler_params=_cparams(1),
        name="merge",
    )(x, ya, ys, proj, proj, att_t, proj, proj, proj, proj,
      gate_b, s5d, glub, fnw, gluw, wa, wb, wc, wo)


def _block_diag_pairs(a):
    g, r, c = a.shape
    a2 = a.reshape(g // 2, 2, r, c)
    out = jnp.einsum("pgrc,gh->pgrhc", a2, jnp.eye(2, dtype=a.dtype))
    return out.reshape(g // 2, 2 * r, 2 * c)


def _lane_scatter():
    half = S5_OCT // 2
    sel = np.zeros((half, 2, S5_OCT), np.float32)
    for pr in range(half):
        for gi in range(2):
            sel[pr, gi, 2 * pr + gi] = 1.0
    eye_t, eye_c = np.eye(S5_L, dtype=np.float32), np.eye(S5_CH, dtype=np.float32)
    sc = np.einsum("pgh,tu,cd->pgtcuhd", sel, eye_t, eye_c)
    return jnp.asarray(sc.reshape(half, 2 * S5_L * S5_CH, S5_L * LANE), dtype=BF16)


def _row_tile(s, want):
    t = min(want, s)
    while s % t:
        t //= 2
    return t


def kernel(x, norm_w, w_in, gate_b, conv_w, conv_b, dt_bias, a_log, ssd_d, ssd_norm_w, w_proj_a,
           lambda_re, lambda_im, log_step, s5_b_re, s5_b_im, s5_c_re, s5_c_im, s5_d, glu_w, glu_b,
           w_proj_b, w_proj_c, w_out, final_norm_w):
    b, s, _ = x.shape
    assert b == 1 and s % BLK == 0 and x.shape[2] == D
    depth = norm_w.shape[0]
    nb = s // BLK
    nc = s // S5_L
    xc = x.reshape(s, D)
    o_dt = 3 * D
    o_s5u = o_dt + SSD_HEADS
    o_q = o_s5u + 2 * D
    o_v = o_q + 2 * D
    o_mg = o_v + D
    expand = (jnp.arange(LANE)[:, None] == (jnp.arange(D)[None, :] // SSD_HEAD_DIM)).astype(F32)
    pad16 = lambda a: jnp.pad(a.astype(F32), (0, LANE - SSD_HEADS)).reshape(1, LANE)
    tile_b = lambda a: jnp.tile(a.transpose(0, 2, 1), (1, S5_L, 1))
    tile_c = lambda a: jnp.tile(a.transpose(0, 2, 1), (1, 1, S5_L))
    scat = _lane_scatter()
    w_nat = jnp.concatenate([w_in[:, :, 0:o_dt], w_in[:, :, o_s5u:o_v], w_in[:, :, o_mg:]],
                            axis=2).astype(BF16)
    w_vt = jnp.swapaxes(w_in[:, :, o_v:o_mg], 1, 2).astype(BF16)
    w_dt = jnp.pad(w_in[:, :, o_dt:o_s5u], ((0, 0), (0, 0), (0, LANE - SSD_HEADS)))
    merge_w = [a.astype(BF16) for a in (glu_w, w_proj_a, w_proj_b, w_proj_c, w_out)]
    pairs = depth * S5_PAIRS
    ls_full = jnp.repeat(log_step.reshape(-1), S5_STATE)
    s5_rows = [a.reshape(pairs, 1, LANE) for a in (lambda_re, lambda_im, ls_full)]
    s5_cols = [a.reshape(pairs, LANE, 1) for a in (lambda_re, lambda_im, ls_full)]
    s5_mats = ([_block_diag_pairs(tile_b(a.reshape(depth * S5_GROUPS, S5_STATE, S5_CH)))
                for a in (s5_b_re, s5_b_im)]
               + [_block_diag_pairs(tile_c(a.reshape(depth * S5_GROUPS, S5_CH, S5_STATE)))
                  for a in (s5_c_re, s5_c_im)])

    for l in range(depth):
        proj, vt, dt = _project(xc, norm_w[l].reshape(1, D), w_nat, w_vt, w_dt, l,
                                     _row_tile(s, 1024))

        ya = _ssd(proj, dt, conv_w[l], conv_b[l].reshape(1, 2 * D),
                  pad16(dt_bias[l]), pad16(a_log[l]),
                  jnp.repeat(ssd_d[l], SSD_HEAD_DIM).reshape(1, D),
                  ssd_norm_w[l].reshape(1, D), expand)

        bd, bxr8, bxi8, cxr8, cxi8, alr, ali = _s5_prep(*s5_rows, *s5_cols, *s5_mats, scat, l)
        n_oct = S5_GROUPS // S5_OCT
        u8 = (proj[:, C_S5U * D:(C_S5U + 1) * D].reshape(nc, S5_L, n_oct, LANE)
              .transpose(2, 0, 1, 3).reshape(n_oct, nc, S5_L * LANE))
        sr, si = _s5_local(u8, bxr8, bxi8)
        pr, pi = _s5_scan(sr, si, alr.reshape(1, -1), ali.reshape(1, -1), 512)
        y8 = _s5_out(u8, bd, pr, pi, cxr8, cxi8)
        ys = y8.reshape(n_oct, nc, S5_L, LANE).transpose(1, 2, 0, 3).reshape(s, D)

        km, kaug = _kprep(proj, math.gcd(nb, 8))
        att_t = _moba(proj, kaug, vt, km)

        xc = _merge(l == depth - 1, xc, ya, ys, proj, att_t, gate_b[l].reshape(1, 3 * D),
                    s5_d[l].reshape(1, D), glu_b[l].reshape(1, D), final_norm_w.reshape(1, D),
                    *merge_w, l, _row_tile(s, 512))
    return xc.reshape(b, s, D)
```

```python
import functools
import math

import jax
import jax.numpy as jnp
import numpy as np
from jax import lax
from jax.experimental import pallas as pl
from jax.experimental.pallas import tpu as pltpu

F32 = jnp.float32
BF16 = jnp.bfloat16
HI = lax.Precision.HIGHEST

D = 1024
EPS = 1e-6
BLK = 256
SSD_HEADS = 16
SSD_HEAD_DIM = 64
SSD_GROUPS = 4
SSD_STATE = 128
SSD_CONV = 4
S5_GROUPS = 64
S5_CH = 16
S5_STATE = 64
S5_CLIP = 1e-4
S5_L = 16
S5_PAIRS = S5_GROUPS // 2
S5_OCT = 8
MOBA_HEADS = 16
MOBA_HD = 64
MOBA_TOPK = 3
MOBA_KG = 4
MOBA_TRIP = 2
NEG = -1e30
LANE = 128
SUBLANE = 8
VMEM_LIMIT = 56 * 1024 * 1024

C_Z, C_XS, C_BC, C_S5U, C_S5G, C_Q, C_K, C_MG, C_GL = 0, 1, 2, 3, 4, 5, 6, 7, 8
N_NAT = 11 * D

NT = (((1,), (1,)), ((), ()))
TN = (((0,), (0,)), ((), ()))


def _cparams(n_axes):
    return pltpu.CompilerParams(dimension_semantics=("arbitrary",) * n_axes,
                                vmem_limit_bytes=VMEM_LIMIT)


def _const_spec(shape):
    nd = len(shape)
    return pl.BlockSpec(shape, lambda *_: (0,) * nd)


def _proj_kernel(x_ref, nw_ref, w_ref, wvt_ref, wdt_ref,
                 proj_ref, vt_ref, dt_ref, h_ref):
    @pl.when(pl.program_id(1) == 0)
    def _():
        x = x_ref[...]
        h = x * lax.rsqrt(jnp.mean(x * x, axis=-1, keepdims=True) + EPS) * nw_ref[...]
        hb = h.astype(BF16)
        h_ref[...] = hb
        vt = lax.dot_general(wvt_ref[...], hb, NT, preferred_element_type=F32)
        vt_ref[...] = vt.astype(BF16)
        dt_ref[...] = jnp.dot(h, wdt_ref[...], precision=HI, preferred_element_type=F32)

    proj_ref[...] = jnp.dot(h_ref[...], w_ref[...], preferred_element_type=F32).astype(BF16)


def _project(x, norm_w, w_nat, w_vt, w_dt, layer, tm):
    s = x.shape[0]
    grid = (s // tm, N_NAT // D)
    return pl.pallas_call(
        _proj_kernel,
        grid=grid,
        in_specs=[
            pl.BlockSpec((tm, D), lambda i, j: (i, 0)),
            _const_spec((1, D)),
            pl.BlockSpec((None, D, D), lambda i, j: (layer, 0, j)),
            pl.BlockSpec((None, D, D), lambda i, j: (layer, 0, 0)),
            pl.BlockSpec((None, D, LANE), lambda i, j: (layer, 0, 0)),
        ],
        out_specs=[
            pl.BlockSpec((tm, D), lambda i, j: (i, j)),
            pl.BlockSpec((D, tm), lambda i, j: (0, i)),
            pl.BlockSpec((tm, LANE), lambda i, j: (i, 0)),
        ],
        out_shape=[
            jax.ShapeDtypeStruct((s, N_NAT), BF16),
            jax.ShapeDtypeStruct((D, s), BF16),
            jax.ShapeDtypeStruct((s, LANE), F32),
        ],
        scratch_shapes=[pltpu.VMEM((tm, D), BF16)],
        compiler_params=_cparams(2),
        name="proj",
    )(x, norm_w, w_nat, w_vt, w_dt)


def _sigmoid(v):
    return 0.5 * jnp.tanh(0.5 * v) + 0.5


def _split3(v):
    hi = v.astype(BF16)
    r1 = v - hi.astype(F32)
    mid = r1.astype(BF16)
    lo = (r1 - mid.astype(F32)).astype(BF16)
    return hi, mid, lo


def _dot_exact_lhs(a_exact, v):
    ab = a_exact.astype(BF16)
    return sum(jnp.dot(ab, t, preferred_element_type=F32) for t in _split3(v))


def _dot_exact_rhs(v, b_exact):
    bb = b_exact.astype(BF16)
    return sum(jnp.dot(t, bb, preferred_element_type=F32) for t in _split3(v))


def _softplus(v):
    return jnp.maximum(v, 0.0) + jnp.log1p(jnp.exp(-jnp.abs(v)))


def _ssd_kernel(z_ref, xs_ref, bc_ref, dt_ref, cw_ref, cb_ref, dtb_ref,
                al_ref, dsk_ref, nw_ref, e_ref, out_ref, ext_ref, st_ref):
    @pl.when(pl.program_id(0) == 0)
    def _():
        ext_ref[0:8, :] = jnp.zeros((8, 2 * D), F32)
        st_ref[...] = jnp.zeros(st_ref.shape, F32)

    ext_ref[8:8 + BLK, 0:D] = xs_ref[...].astype(F32)
    ext_ref[8:8 + BLK, D:2 * D] = bc_ref[...].astype(F32)
    acc = cb_ref[...] + cw_ref[0:1, :] * ext_ref[5:5 + BLK, :]
    for i in range(1, SSD_CONV):
        acc = acc + cw_ref[i:i + 1, :] * ext_ref[5 + i:5 + i + BLK, :]
    ext_ref[0:8, :] = ext_ref[BLK:BLK + 8, :]
    act = acc * _sigmoid(acc)
    xs = act[:, 0:D]

    dt = _softplus(dt_ref[...] + dtb_ref[...])
    da = dt * (-jnp.exp(al_ref[...]))
    row = lax.broadcasted_iota(jnp.int32, (BLK, BLK), 0)
    col = lax.broadcasted_iota(jnp.int32, (BLK, BLK), 1)
    lower = row >= col
    tri = lower.astype(F32)
    a_cum = _dot_exact_lhs(tri, da)
    a_cumt = a_cum.T[0:SSD_HEADS, :]
    xdt = xs * _dot_exact_rhs(dt, e_ref[...])
    xdt_b = xdt.astype(BF16)

    lane = lax.broadcasted_iota(jnp.int32, (BLK, LANE), 1)
    first = lane < SSD_HEAD_DIM
    rowp = lax.broadcasted_iota(jnp.int32, (LANE, 1), 0) < SSD_HEAD_DIM
    y_parts = []
    for g in range(SSD_GROUPS):
        b_g = act[:, D + g * SSD_STATE:D + (g + 1) * SSD_STATE]
        c_g = act[:, D + SSD_GROUPS * SSD_STATE + g * SSD_STATE:
                  D + SSD_GROUPS * SSD_STATE + (g + 1) * SSD_STATE]
        b_gb = b_g.astype(BF16)
        c_gb = c_g.astype(BF16)
        cbm = lax.dot_general(c_gb, b_gb, NT, preferred_element_type=F32)
        for pp in range(2):
            pair = 2 * g + pp
            h0 = 2 * pair
            x_pair = xdt_b[:, pair * LANE:(pair + 1) * LANE]
            y_pair = jnp.zeros((BLK, LANE), F32)
            for hh in range(2):
                h = h0 + hh
                lm = jnp.where(lower, jnp.exp(a_cum[:, h:h + 1] - a_cumt[h:h + 1, :]), 0.0)
                w = (cbm * lm).astype(BF16)
                xm = jnp.where(first if hh == 0 else jnp.logical_not(first), x_pair,
                               jnp.zeros_like(x_pair))
                y_pair = y_pair + jnp.dot(w, xm, preferred_element_type=F32)
            ac_pair = jnp.where(first, a_cum[:, h0:h0 + 1], a_cum[:, h0 + 1:h0 + 2])
            al_pair = ac_pair[BLK - 1:BLK, :]
            r_pair = st_ref[pair]
            y_off = lax.dot_general(c_gb, r_pair.astype(BF16), NT, preferred_element_type=F32)
            y_pair = y_pair + y_off * jnp.exp(ac_pair)
            xdec = (xdt[:, pair * LANE:(pair + 1) * LANE] * jnp.exp(al_pair - ac_pair)).astype(BF16)
            st_new = lax.dot_general(xdec, b_gb, TN, preferred_element_type=F32)
            al_col = jnp.where(rowp, a_cumt[h0:h0 + 1, BLK - 1:BLK],
                               a_cumt[h0 + 1:h0 + 2, BLK - 1:BLK])
            st_ref[pair] = jnp.exp(al_col) * r_pair + st_new
            y_parts.append(y_pair)
    y = jnp.concatenate(y_parts, axis=1) + dsk_ref[...] * xs
    z = z_ref[...].astype(F32)
    y = y * (z * _sigmoid(z))
    gw = D // SSD_GROUPS
    for g in range(SSD_GROUPS):
        yg = y[:, g * gw:(g + 1) * gw]
        yg = yg * lax.rsqrt(jnp.mean(yg * yg, axis=-1, keepdims=True) + EPS)
        out_ref[:, g * gw:(g + 1) * gw] = (yg * nw_ref[:, g * gw:(g + 1) * gw]).astype(BF16)


def _ssd(proj, dt, cw, cb, dtb, al, dsk, nw, expand):
    s = proj.shape[0]
    row = lambda c: pl.BlockSpec((BLK, D), lambda i, c=c: (i, c))
    return pl.pallas_call(
        _ssd_kernel,
        grid=(s // BLK,),
        in_specs=[
            row(C_Z), row(C_XS), row(C_BC),
            pl.BlockSpec((BLK, LANE), lambda i: (i, 0)),
            _const_spec((SSD_CONV, 2 * D)), _const_spec((1, 2 * D)),
            _const_spec((1, LANE)), _const_spec((1, LANE)),
            _const_spec((1, D)), _const_spec((1, D)), _const_spec((LANE, D)),
        ],
        out_specs=pl.BlockSpec((BLK, D), lambda i: (i, 0)),
        out_shape=jax.ShapeDtypeStruct((s, D), BF16),
        scratch_shapes=[pltpu.VMEM((BLK + 8, 2 * D), F32),
                        pltpu.VMEM((SSD_HEADS // 2, LANE, SSD_STATE), F32)],
        compiler_params=_cparams(1),
        name="ssd",
    )(proj, proj, proj, dt, cw, cb, dtb, al, dsk, nw, expand)


S5_LAGPAD = 3


def _s5_prep_kernel(lrr_ref, lir_ref, lsr_ref, lrc_ref, lic_ref, lsc_ref,
                    bre_ref, bim_ref, cre_ref, cim_ref, scat_ref,
                    bd_ref, bxr_ref, bxi_ref, cxr_ref, cxi_ref, alr_ref, ali_ref):
    n = S5_L * S5_CH
    rows = 2 * n
    cols = 2 * n
    bd_ref[0, :, 0:S5_LAGPAD * LANE] = jnp.zeros((LANE, S5_LAGPAD * LANE), BF16)
    bxr_ref[...] = jnp.zeros(bxr_ref.shape, BF16)
    bxi_ref[...] = jnp.zeros(bxi_ref.shape, BF16)
    lane128 = lax.broadcasted_iota(jnp.int32, (S5_CH, LANE), 1) // S5_CH
    ek = (S5_L - 1 - lax.broadcasted_iota(jnp.int32, (S5_L, 1), 0)).astype(F32)
    et = (lax.broadcasted_iota(jnp.int32, (1, S5_L), 1) + 1).astype(F32)
    row_k = (lax.broadcasted_iota(jnp.int32, (rows, S5_L), 0) % n) // S5_CH
    spread_rows = (row_k == lax.broadcasted_iota(jnp.int32, (rows, S5_L), 1)).astype(F32)
    col_t = (lax.broadcasted_iota(jnp.int32, (S5_L, cols), 1) % n) // S5_CH
    spread_cols = (col_t == lax.broadcasted_iota(jnp.int32, (S5_L, cols), 0)).astype(F32)
    for pr in range(S5_OCT // 2):
        lr = jnp.minimum(lrr_ref[pr], -S5_CLIP)
        li = lir_ref[pr]
        st = jnp.exp(lsr_ref[pr])
        lrs, lis = lr * st, li * st
        mag = jnp.exp(lrs)
        abr, abi = mag * jnp.cos(lis), mag * jnp.sin(lis)
        nr, ni = abr - 1.0, abi
        den = lr * lr + li * li
        cfr = (nr * lr + ni * li) / den
        cfi = (ni * lr - nr * li) / den
        bre, bim = bre_ref[pr], bim_ref[pr]
        bbr = cfr * bre - cfi * bim
        bbi = cfr * bim + cfi * bre
        pm = jnp.exp(ek * lrs)
        pbr = _dot_exact_lhs(spread_rows, pm * jnp.cos(ek * lis))
        pbi = _dot_exact_lhs(spread_rows, pm * jnp.sin(ek * lis))
        bxr = pbr * bbr - pbi * bbi
        bxi = pbr * bbi + pbi * bbr
        alm = jnp.exp(float(S5_L) * lrs)
        alr_ref[pr] = alm * jnp.cos(float(S5_L) * lis)
        ali_ref[pr] = alm * jnp.sin(float(S5_L) * lis)
        lrc = jnp.minimum(lrc_ref[pr], -S5_CLIP)
        stc = jnp.exp(lsc_ref[pr])
        lrsc, lisc = lrc * stc, lic_ref[pr] * stc
        qm = jnp.exp(et * lrsc)
        qr = _dot_exact_rhs(qm * jnp.cos(et * lisc), spread_cols)
        qi = _dot_exact_rhs(qm * jnp.sin(et * lisc), spread_cols)
        cre, cim = cre_ref[pr], cim_ref[pr]
        rsl = slice(pr * LANE, (pr + 1) * LANE)
        cxr_ref[0, rsl, :] = jnp.dot((cre * qr - cim * qi).astype(BF16), scat_ref[pr],
                                     preferred_element_type=F32).astype(BF16)
        cxi_ref[0, rsl, :] = jnp.dot((-(cre * qi + cim * qr)).astype(BF16), scat_ref[pr],
                                     preferred_element_type=F32).astype(BF16)
        bxr_b, bxi_b = bxr.astype(BF16), bxi.astype(BF16)
        for gi in range(2):
            g8 = 2 * pr + gi
            gr, gc = slice(gi * n, (gi + 1) * n), slice(gi * n, gi * n + LANE)
            rw = (jnp.dot(bxr[gr, :], cre[:, gc], precision=HI, preferred_element_type=F32)
                  - jnp.dot(bxi[gr, :], cim[:, gc], precision=HI, preferred_element_type=F32))
            for k in range(S5_L):
                src = slice(gi * n + k * S5_CH, gi * n + (k + 1) * S5_CH)
                dst = slice(k * LANE + g8 * S5_CH, k * LANE + (g8 + 1) * S5_CH)
                bxr_ref[0, dst, rsl] = bxr_b[src, :]
                bxi_ref[0, dst, rsl] = bxi_b[src, :]
            for j in range(S5_L):
                m = S5_L - 1 - j
                blk = rw[m * S5_CH:(m + 1) * S5_CH, :]
                bd_ref[0, g8 * S5_CH:(g8 + 1) * S5_CH,
                       (S5_LAGPAD + j) * LANE:(S5_LAGPAD + j + 1) * LANE] = jnp.where(
                           lane128 == g8, blk, 0.0).astype(BF16)


def _s5_prep(lrr, lir, lsr, lrc, lic, lsc, bre2, bim2, cre2, cim2, scat, layer):
    n = S5_L * S5_CH
    half = S5_OCT // 2
    n_oct = S5_GROUPS // S5_OCT
    w8 = S5_L * LANE
    ow = S5_OCT * S5_STATE
    blk4 = lambda a, b: pl.BlockSpec((half, a, b), lambda i: (layer * n_oct + i, 0, 0))
    out4 = lambda a, b: pl.BlockSpec((half, a, b), lambda i: (i, 0, 0))
    oct3 = lambda a, b: pl.BlockSpec((1, a, b), lambda i: (i, 0, 0))
    return pl.pallas_call(
        _s5_prep_kernel,
        grid=(n_oct,),
        in_specs=[blk4(1, LANE)] * 3 + [blk4(LANE, 1)] * 3
                 + [blk4(2 * n, LANE)] * 2 + [blk4(LANE, 2 * n)] * 2
                 + [_const_spec((half, 2 * n, w8))],
        out_specs=[oct3(LANE, (S5_LAGPAD + S5_L) * LANE),
                   oct3(w8, ow), oct3(w8, ow), oct3(ow, w8), oct3(ow, w8),
                   out4(1, LANE), out4(1, LANE)],
        out_shape=[jax.ShapeDtypeStruct((n_oct, LANE, (S5_LAGPAD + S5_L) * LANE), BF16),
                   jax.ShapeDtypeStruct((n_oct, w8, ow), BF16),
                   jax.ShapeDtypeStruct((n_oct, w8, ow), BF16),
                   jax.ShapeDtypeStruct((n_oct, ow, w8), BF16),
                   jax.ShapeDtypeStruct((n_oct, ow, w8), BF16),
                   jax.ShapeDtypeStruct((S5_PAIRS, 1, LANE), F32),
                   jax.ShapeDtypeStruct((S5_PAIRS, 1, LANE), F32)],
        compiler_params=_cparams(1),
        name="s5_prep",
    )(lrr, lir, lsr, lrc, lic, lsc, bre2, bim2, cre2, cim2, scat)


def _s5_local_kernel(u_ref, bxr_ref, bxi_ref, sr_ref, si_ref):
    sr_ref[...] = jnp.dot(u_ref[0], bxr_ref[0], preferred_element_type=F32)
    si_ref[...] = jnp.dot(u_ref[0], bxi_ref[0], preferred_element_type=F32)


def _s5_local(u8, bxr8, bxi8):
    nc, w = u8.shape[1], u8.shape[2]
    ow = S5_OCT * S5_STATE
    return pl.pallas_call(
        _s5_local_kernel,
        grid=(S5_GROUPS // S5_OCT,),
        in_specs=[pl.BlockSpec((1, nc, w), lambda i: (i, 0, 0)),
                  pl.BlockSpec((1, w, ow), lambda i: (i, 0, 0)),
                  pl.BlockSpec((1, w, ow), lambda i: (i, 0, 0))],
        out_specs=[pl.BlockSpec((nc, ow), lambda i: (0, i))] * 2,
        out_shape=[jax.ShapeDtypeStruct((nc, S5_GROUPS * S5_STATE), F32)] * 2,
        compiler_params=_cparams(1),
        name="s5_local",
    )(u8, bxr8, bxi8)


def _s5_scan_kernel(xr_ref, xi_ref, ar_ref, ai_ref, pr_ref, pi_ref):
    ar, ai = ar_ref[...], ai_ref[...]
    w = ar.shape[1]

    def body(b, carry):
        sr, si = carry
        r0 = pl.multiple_of(b * SUBLANE, SUBLANE)
        xr = xr_ref[pl.ds(r0, SUBLANE), :]
        xi = xi_ref[pl.ds(r0, SUBLANE), :]
        out_r, out_i = [], []
        for j in range(SUBLANE):
            out_r.append(sr)
            out_i.append(si)
            sr, si = ar * sr - ai * si + xr[j:j + 1, :], ar * si + ai * sr + xi[j:j + 1, :]
        pr_ref[pl.ds(r0, SUBLANE), :] = jnp.concatenate(out_r, axis=0)
        pi_ref[pl.ds(r0, SUBLANE), :] = jnp.concatenate(out_i, axis=0)
        return sr, si

    assert xr_ref.shape[0] % SUBLANE == 0
    lax.fori_loop(0, xr_ref.shape[0] // SUBLANE, body,
                  (jnp.zeros((1, w), F32), jnp.zeros((1, w), F32)))


def _s5_scan(xr, xi, ar, ai, tw):
    nc, width = xr.shape
    col = pl.BlockSpec((nc, tw), lambda i: (0, i))
    vec = pl.BlockSpec((1, tw), lambda i: (0, i))
    return pl.pallas_call(
        _s5_scan_kernel,
        grid=(width // tw,),
        in_specs=[col, col, vec, vec],
        out_specs=[col, col],
        out_shape=[jax.ShapeDtypeStruct((nc, width), F32)] * 2,
        compiler_params=_cparams(1),
        name="s5_scan",
    )(xr, xi, ar, ai)


def _s5_out_kernel(u_ref, bd_ref, pr_ref, pi_ref, cxr_ref, cxi_ref, y_ref):
    tt = S5_LAGPAD + 1
    prb, pib = pr_ref[...].astype(BF16), pi_ref[...].astype(BF16)
    for jt in range(S5_L // tt):
        osl = slice(jt * tt * LANE, (jt + 1) * tt * LANE)
        acc = (jnp.dot(prb, cxr_ref[0, :, osl], preferred_element_type=F32)
               + jnp.dot(pib, cxi_ref[0, :, osl], preferred_element_type=F32))
        for k in range(0, (jt + 1) * tt, 2):
            lag0 = jt * tt - k + S5_LAGPAD
            lags = jnp.concatenate([bd_ref[0, :, lag0 * LANE:(lag0 + tt) * LANE],
                                    bd_ref[0, :, (lag0 - 1) * LANE:(lag0 - 1 + tt) * LANE]], axis=0)
            acc = acc + jnp.dot(u_ref[0, :, k * LANE:(k + 2) * LANE], lags,
                                preferred_element_type=F32)
        y_ref[0, :, osl] = acc.astype(BF16)


def _s5_out(u8, bd, pr, pi, cxr8, cxi8):
    n_oct, nc, w = u8.shape
    ow = S5_OCT * S5_STATE
    return pl.pallas_call(
        _s5_out_kernel,
        grid=(n_oct,),
        in_specs=[pl.BlockSpec((1, nc, w), lambda i: (i, 0, 0)),
                  pl.BlockSpec((1, LANE, bd.shape[2]), lambda i: (i, 0, 0)),
                  pl.BlockSpec((nc, ow), lambda i: (0, i)),
                  pl.BlockSpec((nc, ow), lambda i: (0, i)),
                  pl.BlockSpec((1, ow, w), lambda i: (i, 0, 0)),
                  pl.BlockSpec((1, ow, w), lambda i: (i, 0, 0))],
        out_specs=pl.BlockSpec((1, nc, w), lambda i: (i, 0, 0)),
        out_shape=jax.ShapeDtypeStruct((n_oct, nc, w), BF16),
        compiler_params=_cparams(1),
        name="s5_out",
    )(u8, bd, pr, pi, cxr8, cxi8)


def _kprep_kernel(k_ref, km_ref, ka_ref):
    rows = km_ref.shape[0]
    lane = lax.broadcasted_iota(jnp.int32, (BLK, D), 1) % LANE
    even = lane < MOBA_HD
    for r in range(rows):
        k = k_ref[r * BLK:(r + 1) * BLK, :]
        km_ref[r:r + 1, :] = jnp.mean(k.astype(F32), axis=0, keepdims=True)
        blk = pl.program_id(0) * rows + r
        ka_ref[0, r * BLK:(r + 1) * BLK, :] = jnp.where(
            even, k, (lane - MOBA_HD == blk).astype(BF16))
        ka_ref[1, r * BLK:(r + 1) * BLK, :] = jnp.where(
            even, (lane == blk).astype(BF16), k)


def _kprep(proj, rows):
    s = proj.shape[0]
    nb = s // BLK
    assert nb <= MOBA_HD
    return pl.pallas_call(
        _kprep_kernel,
        grid=(nb // rows,),
        in_specs=[pl.BlockSpec((rows * BLK, D), lambda i: (i, C_K))],
        out_specs=[pl.BlockSpec((rows, D), lambda i: (i, 0)),
                   pl.BlockSpec((2, rows * BLK, D), lambda i: (0, i, 0))],
        out_shape=[jax.ShapeDtypeStruct((nb, D), F32),
                   jax.ShapeDtypeStruct((2, s, D), BF16)],
        compiler_params=_cparams(1),
        name="kprep",
    )(proj)


def _moba_kernel(q_ref, ka_ref, vt_ref, km_ref, o_ref, s_ref):
    i = pl.program_id(1)
    nb = km_ref.shape[0]
    kg = min(MOBA_KG, nb)
    kt = kg * BLK
    q = q_ref[...].astype(F32)
    lane = lax.broadcasted_iota(jnp.int32, (BLK, LANE), 1)
    blk = lax.broadcasted_iota(jnp.int32, (nb, BLK), 0)
    blk_f = blk.astype(F32)
    scale = MOBA_HD ** -0.5 * math.log2(math.e)
    qs, qs_own = [], []
    for h in range(2):
        in_head = (lane >= h * MOBA_HD) & (lane < (h + 1) * MOBA_HD)
        qh = jnp.where(in_head, q, 0.0)
        g = sum(lax.dot_general(t, qh.astype(BF16), NT, preferred_element_type=F32)
                for t in _split3(km_ref[...]))
        g = jnp.where(blk < i, g, -jnp.inf)
        sel = jnp.zeros((nb, BLK), jnp.bool_)
        for _ in range(MOBA_TOPK):
            mx = jnp.max(g, axis=0, keepdims=True)
            idx = jnp.min(jnp.where(g == mx, blk_f, float(nb)), axis=0, keepdims=True)
            hit = blk_f == idx
            sel = sel | (hit & (mx > -jnp.inf))
            g = jnp.where(hit, -jnp.inf, g)
        bias = jnp.where(sel, 0.0, NEG)
        pads = [jnp.full((MOBA_HD - nb, BLK), NEG, F32)] if nb < MOBA_HD else []
        other = jnp.zeros((MOBA_HD, BLK), F32)
        bias_q = jnp.concatenate(([other, bias] + pads) if h == 0 else ([bias] + pads + [other]),
                                 axis=0).T
        qs.append(jnp.where(in_head, qh * scale, bias_q).astype(BF16))
        qs_own.append((qh * scale).astype(BF16))

    krow = lax.broadcasted_iota(jnp.int32, (BLK, BLK), 0)
    qcol = lax.broadcasted_iota(jnp.int32, (BLK, BLK), 1)
    causal = krow <= qcol
    last_g = nb // kg - 1
    ones = lambda n: jnp.ones((16, n), BF16)

    ng = MOBA_TRIP
    last_trip = nb // kg // ng - 1

    def produce(tp, h):
        tcp = jnp.minimum(tp, last_trip)
        kb = ka_ref[h, pl.ds(pl.multiple_of(tcp * ng * kt, ng * kt), ng * kt), :]
        s_t = lax.dot_general(kb, qs[h], NT, preferred_element_type=F32)
        s_ref[h] = s_t
        return jnp.max(s_t, axis=0, keepdims=True)

    def step(tp, carry, width, look_ahead, own=False):
        state, maxima = carry
        start = i * BLK if own else jnp.minimum(tp, last_trip) * (ng * kt)
        vt = vt_ref[:, pl.ds(pl.multiple_of(start, BLK), width)]
        new_state, new_maxima = [], []
        for h in range(2):
            m, acc = state[h]
            mn = jnp.maximum(m, maxima[h])
            p = jnp.exp2(s_ref[h, 0:width, :] - mn).astype(BF16)
            new_maxima.append(produce(tp + 1, h) if look_ahead else maxima[h])
            va = jnp.concatenate([vt[h * MOBA_HD:(h + 1) * MOBA_HD, :], ones(width)], axis=0)
            new_state.append((mn, jnp.exp2(m - mn) * acc + jnp.dot(va, p, preferred_element_type=F32)))
        return tuple(new_state), tuple(new_maxima)

    own_max = []
    for h in range(2):
        kb_own = ka_ref[h, pl.ds(pl.multiple_of(i * BLK, BLK), BLK), :]
        s_own = jnp.where(causal, lax.dot_general(kb_own, qs_own[h], NT, preferred_element_type=F32), NEG)
        s_ref[h, 0:BLK, :] = s_own
        own_max.append(jnp.max(s_own, axis=0, keepdims=True))
    init = (jnp.full((1, BLK), NEG, F32), jnp.zeros((MOBA_HD + 16, BLK), F32))
    state, maxima = step(-1, ((init, init), tuple(own_max)), BLK, True, own=True)

    n_groups = (i + kg - 1) // kg
    carry = lax.fori_loop(0, n_groups // ng, lambda tp, c: step(tp, c, ng * kt, True),
                          (tuple(state), maxima))
    tails = [lambda c: c] + [functools.partial(lambda r, c: step(n_groups // ng, c, r * kt, False), r)
                             for r in range(1, ng)]
    carry, _ = lax.switch(n_groups % ng, tails, carry)
    for h in range(2):
        _, acc = carry[h]
        o_ref[h * MOBA_HD:(h + 1) * MOBA_HD, :] = (
            acc[0:MOBA_HD, :] / acc[MOBA_HD:MOBA_HD + 1, :]).astype(BF16)


def _moba(proj, kaug, vt, kmean):
    s = proj.shape[0]
    nb = s // BLK
    assert nb % (MOBA_TRIP * min(MOBA_KG, nb)) == 0
    per = D // LANE
    return pl.pallas_call(
        _moba_kernel,
        grid=(MOBA_HEADS // 2, nb),
        in_specs=[pl.BlockSpec((BLK, LANE), lambda hp, i: (i, C_Q * per + hp)),
                  pl.BlockSpec((2, s, LANE), lambda hp, i: (0, 0, hp)),
                  pl.BlockSpec((LANE, s), lambda hp, i: (hp, 0)),
                  pl.BlockSpec((nb, LANE), lambda hp, i: (0, hp))],
        out_specs=pl.BlockSpec((LANE, BLK), lambda hp, i: (hp, i)),
        out_shape=jax.ShapeDtypeStruct((D, s), BF16),
        scratch_shapes=[pltpu.VMEM((2, MOBA_TRIP * min(MOBA_KG, nb) * BLK, BLK), F32)],
        compiler_params=_cparams(2),
        name="moba",
    )(proj, kaug, vt, kmean)


def _merge_kernel(final, x_ref, ya_ref, ys_ref, u_ref, sg_ref, att_ref, mg_ref,
                  g0_ref, g1_ref, g2_ref, gb_ref, s5d_ref, glub_ref, fnw_ref,
                  gluw_ref, wa_ref, wb_ref, wc_ref, wo_ref, out_ref):
    mm = lambda a, w_ref: jnp.dot(a.astype(BF16), w_ref[...], preferred_element_type=F32)
    f32 = lambda r: r[...].astype(F32)
    yb = f32(ys_ref) + s5d_ref[...] * f32(u_ref)
    yb = jax.nn.gelu(yb)
    yb = yb * _sigmoid(mm(yb, gluw_ref) + glub_ref[...])
    sg = f32(sg_ref)
    yb = yb * (sg * _sigmoid(sg))
    mg = f32(mg_ref)
    att = f32(att_ref).T * (mg * _sigmoid(mg))
    merged = (_sigmoid(f32(g0_ref) + gb_ref[:, 0:D]) * mm(ya_ref[...], wa_ref)
              + _sigmoid(f32(g1_ref) + gb_ref[:, D:2 * D]) * mm(yb, wb_ref)
              + _sigmoid(f32(g2_ref) + gb_ref[:, 2 * D:3 * D]) * mm(att, wc_ref))
    xn = x_ref[...] + mm(merged, wo_ref)
    if final:
        xn = xn * lax.rsqrt(jnp.mean(xn * xn, axis=-1, keepdims=True) + EPS) * fnw_ref[...]
    out_ref[...] = xn


def _merge(final, x, ya, ys, proj, att_t, gate_b, s5d, glub, fnw, gluw, wa, wb, wc, wo, layer, tm):
    s = x.shape[0]
    rowb = pl.BlockSpec((tm, D), lambda i: (i, 0))
    pcol = lambda c: pl.BlockSpec((tm, D), lambda i, c=c: (i, c))
    wspec = pl.BlockSpec((None, D, D), lambda i: (layer, 0, 0), pipeline_mode=pl.Buffered(1))
    return pl.pallas_call(
        functools.partial(_merge_kernel, final),
        grid=(s // tm,),
        in_specs=[rowb, rowb, rowb, pcol(C_S5U), pcol(C_S5G),
                  pl.BlockSpec((D, tm), lambda i: (0, i)), pcol(C_MG),
                  pcol(C_GL), pcol(C_GL + 1), pcol(C_GL + 2),
                  _const_spec((1, 3 * D)), _const_spec((1, D)), _const_spec((1, D)),
                  _const_spec((1, D)), wspec, wspec, wspec, wspec, wspec],
        out_specs=rowb,
        out_shape=jax.ShapeDtypeStruct((s, D), F32),
        compiler_params=_cparams(1),
        name="merge",
    )(x, ya, ys, proj, proj, att_t, proj, proj, proj, proj,
      gate_b, s5d, glub, fnw, gluw, wa, wb, wc, wo)


def _block_diag_pairs(a):
    g, r, c = a.shape
    a2 = a.reshape(g // 2, 2, r, c)
    out = jnp.einsum("pgrc,gh->pgrhc", a2, jnp.eye(2, dtype=a.dtype))
    return out.reshape(g // 2, 2 * r, 2 * c)


def _lane_scatter():
    half = S5_OCT // 2
    sel = np.zeros((half, 2, S5_OCT), np.float32)
    for pr in range(half):
        for gi in range(2):
            sel[pr, gi, 2 * pr + gi] = 1.0
    eye_t, eye_c = np.eye(S5_L, dtype=np.float32), np.eye(S5_CH, dtype=np.float32)
    sc = np.einsum("pgh,tu,cd->pgtcuhd", sel, eye_t, eye_c)
    return jnp.asarray(sc.reshape(half, 2 * S5_L * S5_CH, S5_L * LANE), dtype=BF16)


def _row_tile(s, want):
    t = min(want, s)
    while s % t:
        t //= 2
    return t


def kernel(x, norm_w, w_in, gate_b, conv_w, conv_b, dt_bias, a_log, ssd_d, ssd_norm_w, w_proj_a,
           lambda_re, lambda_im, log_step, s5_b_re, s5_b_im, s5_c_re, s5_c_im, s5_d, glu_w, glu_b,
           w_proj_b, w_proj_c, w_out, final_norm_w):
    b, s, _ = x.shape
    assert b == 1 and s % BLK == 0 and x.shape[2] == D
    depth = norm_w.shape[0]
    nb = s // BLK
    nc = s // S5_L
    xc = x.reshape(s, D)
    o_dt = 3 * D
    o_s5u = o_dt + SSD_HEADS
    o_q = o_s5u + 2 * D
    o_v = o_q + 2 * D
    o_mg = o_v + D
    expand = (jnp.arange(LANE)[:, None] == (jnp.arange(D)[None, :] // SSD_HEAD_DIM)).astype(F32)
    pad16 = lambda a: jnp.pad(a.astype(F32), (0, LANE - SSD_HEADS)).reshape(1, LANE)
    tile_b = lambda a: jnp.tile(a.transpose(0, 2, 1), (1, S5_L, 1))
    tile_c = lambda a: jnp.tile(a.transpose(0, 2, 1), (1, 1, S5_L))
    scat = _lane_scatter()
    w_nat = jnp.concatenate([w_in[:, :, 0:o_dt], w_in[:, :, o_s5u:o_v], w_in[:, :, o_mg:]],
                            axis=2).astype(BF16)
    w_vt = jnp.swapaxes(w_in[:, :, o_v:o_mg], 1, 2).astype(BF16)
    w_dt = jnp.pad(w_in[:, :, o_dt:o_s5u], ((0, 0), (0, 0), (0, LANE - SSD_HEADS)))
    merge_w = [a.astype(BF16) for a in (glu_w, w_proj_a, w_proj_b, w_proj_c, w_out)]
    pairs = depth * S5_PAIRS
    ls_full = jnp.repeat(log_step.reshape(-1), S5_STATE)
    s5_rows = [a.reshape(pairs, 1, LANE) for a in (lambda_re, lambda_im, ls_full)]
    s5_cols = [a.reshape(pairs, LANE, 1) for a in (lambda_re, lambda_im, ls_full)]
    s5_mats = ([_block_diag_pairs(tile_b(a.reshape(depth * S5_GROUPS, S5_STATE, S5_CH)))
                for a in (s5_b_re, s5_b_im)]
               + [_block_diag_pairs(tile_c(a.reshape(depth * S5_GROUPS, S5_CH, S5_STATE)))
                  for a in (s5_c_re, s5_c_im)])

    for l in range(depth):
        proj, vt, dt = _project(xc, norm_w[l].reshape(1, D), w_nat, w_vt, w_dt, l,
                                     _row_tile(s, 1024))

        ya = _ssd(proj, dt, conv_w[l], conv_b[l].reshape(1, 2 * D),
                  pad16(dt_bias[l]), pad16(a_log[l]),
                  jnp.repeat(ssd_d[l], SSD_HEAD_DIM).reshape(1, D),
                  ssd_norm_w[l].reshape(1, D), expand)

        bd, bxr8, bxi8, cxr8, cxi8, alr, ali = _s5_prep(*s5_rows, *s5_cols, *s5_mats, scat, l)
        n_oct = S5_GROUPS // S5_OCT
        u8 = (proj[:, C_S5U * D:(C_S5U + 1) * D].reshape(nc, S5_L, n_oct, LANE)
              .transpose(2, 0, 1, 3).reshape(n_oct, nc, S5_L * LANE))
        sr, si = _s5_local(u8, bxr8, bxi8)
        pr, pi = _s5_scan(sr, si, alr.reshape(1, -1), ali.reshape(1, -1), 512)
        y8 = _s5_out(u8, bd, pr, pi, cxr8, cxi8)
        ys = y8.reshape(n_oct, nc, S5_L, LANE).transpose(1, 2, 0, 3).reshape(s, D)

        km, kaug = _kprep(proj, math.gcd(nb, 8))
        att_t = _moba(proj, kaug, vt, km)

        xc = _merge(l == depth - 1, xc, ya, ys, proj, att_t, gate_b[l].reshape(1, 3 * D),
                    s5_d[l].reshape(1, D), glu_b[l].reshape(1, D), final_norm_w.reshape(1, D),
                    *merge_w, l, _row_tile(s, 512))
    return xc.reshape(b, s, D)
```

```python
import functools
import math

import jax
import jax.numpy as jnp
import numpy as np
from jax import lax
from jax.experimental import pallas as pl
from jax.experimental.pallas import tpu as pltpu

F32 = jnp.float32
BF16 = jnp.bfloat16
HI = lax.Precision.HIGHEST

D = 1024
EPS = 1e-6
BLK = 256
SSD_HEADS = 16
SSD_HEAD_DIM = 64
SSD_GROUPS = 4
SSD_STATE = 128
SSD_CONV = 4
S5_GROUPS = 64
S5_CH = 16
S5_STATE = 64
S5_CLIP = 1e-4
S5_L = 16
S5_PAIRS = S5_GROUPS // 2
S5_OCT = 8
MOBA_HEADS = 16
MOBA_HD = 64
MOBA_TOPK = 3
MOBA_KG = 4
MOBA_TRIP = 2
NEG = -1e30
LANE = 128
SUBLANE = 8
VMEM_LIMIT = 56 * 1024 * 1024

C_Z, C_XS, C_BC, C_S5U, C_S5G, C_Q, C_K, C_MG, C_GL = 0, 1, 2, 3, 4, 5, 6, 7, 8
N_NAT = 11 * D

NT = (((1,), (1,)), ((), ()))
TN = (((0,), (0,)), ((), ()))


def _cparams(n_axes):
    return pltpu.CompilerParams(dimension_semantics=("arbitrary",) * n_axes,
                                vmem_limit_bytes=VMEM_LIMIT)


def _const_spec(shape):
    nd = len(shape)
    return pl.BlockSpec(shape, lambda *_: (0,) * nd)


def _proj_kernel(x_ref, nw_ref, w_ref, wv_ref, wdt_ref,
                 proj_ref, vt_ref, dt_ref, h_ref, wvt_ref):
    @pl.when((pl.program_id(0) == 0) & (pl.program_id(1) == 0))
    def _():
        wvt_ref[...] = wv_ref[...].T.astype(BF16)

    @pl.when(pl.program_id(1) == 0)
    def _():
        x = x_ref[...]
        h = x * lax.rsqrt(jnp.mean(x * x, axis=-1, keepdims=True) + EPS) * nw_ref[...]
        hb = h.astype(BF16)
        h_ref[...] = hb
        vt = lax.dot_general(wvt_ref[...], hb, NT, preferred_element_type=F32)
        vt_ref[...] = vt.astype(BF16)
        dt_ref[...] = jnp.dot(h, wdt_ref[...], precision=HI, preferred_element_type=F32)

    proj_ref[...] = jnp.dot(h_ref[...], w_ref[...], preferred_element_type=F32).astype(BF16)


def _project(x, norm_w, w_nat, w_vt, w_dt, layer, tm):
    s = x.shape[0]
    grid = (s // tm, N_NAT // D)
    return pl.pallas_call(
        _proj_kernel,
        grid=grid,
        in_specs=[
            pl.BlockSpec((tm, D), lambda i, j: (i, 0)),
            _const_spec((1, D)),
            pl.BlockSpec((None, D, D), lambda i, j: (layer, 0, j)),
            pl.BlockSpec((None, D, D), lambda i, j: (layer, 0, 0)),
            pl.BlockSpec((None, D, LANE), lambda i, j: (layer, 0, 0)),
        ],
        out_specs=[
            pl.BlockSpec((tm, D), lambda i, j: (i, j)),
            pl.BlockSpec((D, tm), lambda i, j: (0, i)),
            pl.BlockSpec((tm, LANE), lambda i, j: (i, 0)),
        ],
        out_shape=[
            jax.ShapeDtypeStruct((s, N_NAT), BF16),
            jax.ShapeDtypeStruct((D, s), BF16),
            jax.ShapeDtypeStruct((s, LANE), F32),
        ],
        scratch_shapes=[pltpu.VMEM((tm, D), BF16), pltpu.VMEM((D, D), BF16)],
        compiler_params=_cparams(2),
        name="proj",
    )(x, norm_w, w_nat, w_vt, w_dt)


def _sigmoid(v):
    return 0.5 * jnp.tanh(0.5 * v) + 0.5


def _split3(v):
    hi = v.astype(BF16)
    r1 = v - hi.astype(F32)
    mid = r1.astype(BF16)
    lo = (r1 - mid.astype(F32)).astype(BF16)
    return hi, mid, lo


def _dot_exact_lhs(a_exact, v):
    ab = a_exact.astype(BF16)
    return sum(jnp.dot(ab, t, preferred_element_type=F32) for t in _split3(v))


def _dot_exact_rhs(v, b_exact):
    bb = b_exact.astype(BF16)
    return sum(jnp.dot(t, bb, preferred_element_type=F32) for t in _split3(v))


def _softplus(v):
    return jnp.maximum(v, 0.0) + jnp.log1p(jnp.exp(-jnp.abs(v)))


def _ssd_kernel(z_ref, xs_ref, bc_ref, dt_ref, cw_ref, cb_ref, dtb_ref,
                al_ref, dsk_ref, nw_ref, e_ref, out_ref, ext_ref, st_ref):
    @pl.when(pl.program_id(0) == 0)
    def _():
        ext_ref[0:8, :] = jnp.zeros((8, 2 * D), F32)
        st_ref[...] = jnp.zeros(st_ref.shape, F32)

    ext_ref[8:8 + BLK, 0:D] = xs_ref[...].astype(F32)
    ext_ref[8:8 + BLK, D:2 * D] = bc_ref[...].astype(F32)
    acc = cb_ref[...] + cw_ref[0:1, :] * ext_ref[5:5 + BLK, :]
    for i in range(1, SSD_CONV):
        acc = acc + cw_ref[i:i + 1, :] * ext_ref[5 + i:5 + i + BLK, :]
    ext_ref[0:8, :] = ext_ref[BLK:BLK + 8, :]
    act = acc * _sigmoid(acc)
    xs = act[:, 0:D]

    dt = _softplus(dt_ref[...] + dtb_ref[...])
    da = dt * (-jnp.exp(al_ref[...]))
    row = lax.broadcasted_iota(jnp.int32, (BLK, BLK), 0)
    col = lax.broadcasted_iota(jnp.int32, (BLK, BLK), 1)
    lower = row >= col
    tri = lower.astype(F32)
    a_cum = _dot_exact_lhs(tri, da)
    a_cumt = a_cum.T[0:SSD_HEADS, :]
    xdt = xs * _dot_exact_rhs(dt, e_ref[...])
    xdt_b = xdt.astype(BF16)

    lane = lax.broadcasted_iota(jnp.int32, (BLK, LANE), 1)
    first = lane < SSD_HEAD_DIM
    rowp = lax.broadcasted_iota(jnp.int32, (LANE, 1), 0) < SSD_HEAD_DIM
    y_parts = []
    for g in range(SSD_GROUPS):
        b_g = act[:, D + g * SSD_STATE:D + (g + 1) * SSD_STATE]
        c_g = act[:, D + SSD_GROUPS * SSD_STATE + g * SSD_STATE:
                  D + SSD_GROUPS * SSD_STATE + (g + 1) * SSD_STATE]
        b_gb = b_g.astype(BF16)
        c_gb = c_g.astype(BF16)
        cbm = lax.dot_general(c_gb, b_gb, NT, preferred_element_type=F32)
        for pp in range(2):
            pair = 2 * g + pp
            h0 = 2 * pair
            x_pair = xdt_b[:, pair * LANE:(pair + 1) * LANE]
            y_pair = jnp.zeros((BLK, LANE), F32)
            for hh in range(2):
                h = h0 + hh
                lm = jnp.where(lower, jnp.exp(a_cum[:, h:h + 1] - a_cumt[h:h + 1, :]), 0.0)
                w = (cbm * lm).astype(BF16)
                xm = jnp.where(first if hh == 0 else jnp.logical_not(first), x_pair,
                               jnp.zeros_like(x_pair))
                y_pair = y_pair + jnp.dot(w, xm, preferred_element_type=F32)
            ac_pair = jnp.where(first, a_cum[:, h0:h0 + 1], a_cum[:, h0 + 1:h0 + 2])
            al_pair = ac_pair[BLK - 1:BLK, :]
            r_pair = st_ref[pair]
            y_off = lax.dot_general(c_gb, r_pair.astype(BF16), NT, preferred_element_type=F32)
            y_pair = y_pair + y_off * jnp.exp(ac_pair)
            xdec = (xdt[:, pair * LANE:(pair + 1) * LANE] * jnp.exp(al_pair - ac_pair)).astype(BF16)
            st_new = lax.dot_general(xdec, b_gb, TN, preferred_element_type=F32)
            al_col = jnp.where(rowp, a_cumt[h0:h0 + 1, BLK - 1:BLK],
                               a_cumt[h0 + 1:h0 + 2, BLK - 1:BLK])
            st_ref[pair] = jnp.exp(al_col) * r_pair + st_new
            y_parts.append(y_pair)
    y = jnp.concatenate(y_parts, axis=1) + dsk_ref[...] * xs
    z = z_ref[...].astype(F32)
    y = y * (z * _sigmoid(z))
    gw = D // SSD_GROUPS
    for g in range(SSD_GROUPS):
        yg = y[:, g * gw:(g + 1) * gw]
        yg = yg * lax.rsqrt(jnp.mean(yg * yg, axis=-1, keepdims=True) + EPS)
        out_ref[:, g * gw:(g + 1) * gw] = (yg * nw_ref[:, g * gw:(g + 1) * gw]).astype(BF16)


def _ssd(proj, dt, cw, cb, dtb, al, dsk, nw, expand):
    s = proj.shape[0]
    row = lambda c: pl.BlockSpec((BLK, D), lambda i, c=c: (i, c))
    return pl.pallas_call(
        _ssd_kernel,
        grid=(s // BLK,),
        in_specs=[
            row(C_Z), row(C_XS), row(C_BC),
            pl.BlockSpec((BLK, LANE), lambda i: (i, 0)),
            _const_spec((SSD_CONV, 2 * D)), _const_spec((1, 2 * D)),
            _const_spec((1, LANE)), _const_spec((1, LANE)),
            _const_spec((1, D)), _const_spec((1, D)), _const_spec((LANE, D)),
        ],
        out_specs=pl.BlockSpec((BLK, D), lambda i: (i, 0)),
        out_shape=jax.ShapeDtypeStruct((s, D), BF16),
        scratch_shapes=[pltpu.VMEM((BLK + 8, 2 * D), F32),
                        pltpu.VMEM((SSD_HEADS // 2, LANE, SSD_STATE), F32)],
        compiler_params=_cparams(1),
        name="ssd",
    )(proj, proj, proj, dt, cw, cb, dtb, al, dsk, nw, expand)


S5_LAGPAD = 3


def _s5_prep_kernel(lrr_ref, lir_ref, lsr_ref, lrc_ref, lic_ref, lsc_ref,
                    bre_ref, bim_ref, cre_ref, cim_ref, scat_ref,
                    bd_ref, bxr_ref, bxi_ref, cxr_ref, cxi_ref, alr_ref, ali_ref):
    n = S5_L * S5_CH
    rows = 2 * n
    cols = 2 * n
    bd_ref[0, :, 0:S5_LAGPAD * LANE] = jnp.zeros((LANE, S5_LAGPAD * LANE), BF16)
    bxr_ref[...] = jnp.zeros(bxr_ref.shape, BF16)
    bxi_ref[...] = jnp.zeros(bxi_ref.shape, BF16)
    lane128 = lax.broadcasted_iota(jnp.int32, (S5_CH, LANE), 1) // S5_CH
    ek = (S5_L - 1 - lax.broadcasted_iota(jnp.int32, (S5_L, 1), 0)).astype(F32)
    et = (lax.broadcasted_iota(jnp.int32, (1, S5_L), 1) + 1).astype(F32)
    row_k = (lax.broadcasted_iota(jnp.int32, (rows, S5_L), 0) % n) // S5_CH
    spread_rows = (row_k == lax.broadcasted_iota(jnp.int32, (rows, S5_L), 1)).astype(F32)
    col_t = (lax.broadcasted_iota(jnp.int32, (S5_L, cols), 1) % n) // S5_CH
    spread_cols = (col_t == lax.broadcasted_iota(jnp.int32, (S5_L, cols), 0)).astype(F32)
    for pr in range(S5_OCT // 2):
        lr = jnp.minimum(lrr_ref[pr], -S5_CLIP)
        li = lir_ref[pr]
        st = jnp.exp(lsr_ref[pr])
        lrs, lis = lr * st, li * st
        mag = jnp.exp(lrs)
        abr, abi = mag * jnp.cos(lis), mag * jnp.sin(lis)
        nr, ni = abr - 1.0, abi
        den = lr * lr + li * li
        cfr = (nr * lr + ni * li) / den
        cfi = (ni * lr - nr * li) / den
        bre, bim = bre_ref[pr], bim_ref[pr]
        bbr = cfr * bre - cfi * bim
        bbi = cfr * bim + cfi * bre
        pm = jnp.exp(ek * lrs)
        pbr = _dot_exact_lhs(spread_rows, pm * jnp.cos(ek * lis))
        pbi = _dot_exact_lhs(spread_rows, pm * jnp.sin(ek * lis))
        bxr = pbr * bbr - pbi * bbi
        bxi = pbr * bbi + pbi * bbr
        alm = jnp.exp(float(S5_L) * lrs)
        alr_ref[pr] = alm * jnp.cos(float(S5_L) * lis)
        ali_ref[pr] = alm * jnp.sin(float(S5_L) * lis)
        lrc = jnp.minimum(lrc_ref[pr], -S5_CLIP)
        stc = jnp.exp(lsc_ref[pr])
        lrsc, lisc = lrc * stc, lic_ref[pr] * stc
        qm = jnp.exp(et * lrsc)
        qr = _dot_exact_rhs(qm * jnp.cos(et * lisc), spread_cols)
        qi = _dot_exact_rhs(qm * jnp.sin(et * lisc), spread_cols)
        cre, cim = cre_ref[pr], cim_ref[pr]
        rsl = slice(pr * LANE, (pr + 1) * LANE)
        cxr_ref[0, rsl, :] = jnp.dot((cre * qr - cim * qi).astype(BF16), scat_ref[pr],
                                     preferred_element_type=F32).astype(BF16)
        cxi_ref[0, rsl, :] = jnp.dot((-(cre * qi + cim * qr)).astype(BF16), scat_ref[pr],
                                     preferred_element_type=F32).astype(BF16)
        bxr_b, bxi_b = bxr.astype(BF16), bxi.astype(BF16)
        for gi in range(2):
            g8 = 2 * pr + gi
            gr, gc = slice(gi * n, (gi + 1) * n), slice(gi * n, gi * n + LANE)
            rw = (jnp.dot(bxr[gr, :], cre[:, gc], precision=HI, preferred_element_type=F32)
                  - jnp.dot(bxi[gr, :], cim[:, gc], precision=HI, preferred_element_type=F32))
            for k in range(S5_L):
                src = slice(gi * n + k * S5_CH, gi * n + (k + 1) * S5_CH)
                dst = slice(k * LANE + g8 * S5_CH, k * LANE + (g8 + 1) * S5_CH)
                bxr_ref[0, dst, rsl] = bxr_b[src, :]
                bxi_ref[0, dst, rsl] = bxi_b[src, :]
            for j in range(S5_L):
                m = S5_L - 1 - j
                blk = rw[m * S5_CH:(m + 1) * S5_CH, :]
                bd_ref[0, g8 * S5_CH:(g8 + 1) * S5_CH,
                       (S5_LAGPAD + j) * LANE:(S5_LAGPAD + j + 1) * LANE] = jnp.where(
                           lane128 == g8, blk, 0.0).astype(BF16)


def _s5_prep(lrr, lir, lsr, lrc, lic, lsc, bre2, bim2, cre2, cim2, scat, layer):
    n = S5_L * S5_CH
    half = S5_OCT // 2
    n_oct = S5_GROUPS // S5_OCT
    w8 = S5_L * LANE
    ow = S5_OCT * S5_STATE
    blk4 = lambda a, b: pl.BlockSpec((half, a, b), lambda i: (layer * n_oct + i, 0, 0))
    out4 = lambda a, b: pl.BlockSpec((half, a, b), lambda i: (i, 0, 0))
    oct3 = lambda a, b: pl.BlockSpec((1, a, b), lambda i: (i, 0, 0))
    return pl.pallas_call(
        _s5_prep_kernel,
        grid=(n_oct,),
        in_specs=[blk4(1, LANE)] * 3 + [blk4(LANE, 1)] * 3
                 + [blk4(2 * n, LANE)] * 2 + [blk4(LANE, 2 * n)] * 2
                 + [_const_spec((half, 2 * n, w8))],
        out_specs=[oct3(LANE, (S5_LAGPAD + S5_L) * LANE),
                   oct3(w8, ow), oct3(w8, ow), oct3(ow, w8), oct3(ow, w8),
                   out4(1, LANE), out4(1, LANE)],
        out_shape=[jax.ShapeDtypeStruct((n_oct, LANE, (S5_LAGPAD + S5_L) * LANE), BF16),
                   jax.ShapeDtypeStruct((n_oct, w8, ow), BF16),
                   jax.ShapeDtypeStruct((n_oct, w8, ow), BF16),
                   jax.ShapeDtypeStruct((n_oct, ow, w8), BF16),
                   jax.ShapeDtypeStruct((n_oct, ow, w8), BF16),
                   jax.ShapeDtypeStruct((S5_PAIRS, 1, LANE), F32),
                   jax.ShapeDtypeStruct((S5_PAIRS, 1, LANE), F32)],
        compiler_params=_cparams(1),
        name="s5_prep",
    )(lrr, lir, lsr, lrc, lic, lsc, bre2, bim2, cre2, cim2, scat)


def _s5_local_kernel(u_ref, bxr_ref, bxi_ref, sr_ref, si_ref):
    sr_ref[...] = jnp.dot(u_ref[0], bxr_ref[0], preferred_element_type=F32)
    si_ref[...] = jnp.dot(u_ref[0], bxi_ref[0], preferred_element_type=F32)


def _s5_local(u8, bxr8, bxi8):
    nc, w = u8.shape[1], u8.shape[2]
    ow = S5_OCT * S5_STATE
    return pl.pallas_call(
        _s5_local_kernel,
        grid=(S5_GROUPS // S5_OCT,),
        in_specs=[pl.BlockSpec((1, nc, w), lambda i: (i, 0, 0)),
                  pl.BlockSpec((1, w, ow), lambda i: (i, 0, 0)),
                  pl.BlockSpec((1, w, ow), lambda i: (i, 0, 0))],
        out_specs=[pl.BlockSpec((nc, ow), lambda i: (0, i))] * 2,
        out_shape=[jax.ShapeDtypeStruct((nc, S5_GROUPS * S5_STATE), F32)] * 2,
        compiler_params=_cparams(1),
        name="s5_local",
    )(u8, bxr8, bxi8)


def _s5_scan_kernel(xr_ref, xi_ref, ar_ref, ai_ref, pr_ref, pi_ref):
    ar, ai = ar_ref[...], ai_ref[...]
    w = ar.shape[1]

    def body(b, carry):
        sr, si = carry
        r0 = pl.multiple_of(b * SUBLANE, SUBLANE)
        xr = xr_ref[pl.ds(r0, SUBLANE), :]
        xi = xi_ref[pl.ds(r0, SUBLANE), :]
        out_r, out_i = [], []
        for j in range(SUBLANE):
            out_r.append(sr)
            out_i.append(si)
            sr, si = ar * sr - ai * si + xr[j:j + 1, :], ar * si + ai * sr + xi[j:j + 1, :]
        pr_ref[pl.ds(r0, SUBLANE), :] = jnp.concatenate(out_r, axis=0)
        pi_ref[pl.ds(r0, SUBLANE), :] = jnp.concatenate(out_i, axis=0)
        return sr, si

    assert xr_ref.shape[0] % SUBLANE == 0
    lax.fori_loop(0, xr_ref.shape[0] // SUBLANE, body,
                  (jnp.zeros((1, w), F32), jnp.zeros((1, w), F32)))


def _s5_scan(xr, xi, ar, ai, tw):
    nc, width = xr.shape
    col = pl.BlockSpec((nc, tw), lambda i: (0, i))
    vec = pl.BlockSpec((1, tw), lambda i: (0, i))
    return pl.pallas_call(
        _s5_scan_kernel,
        grid=(width // tw,),
        in_specs=[col, col, vec, vec],
        out_specs=[col, col],
        out_shape=[jax.ShapeDtypeStruct((nc, width), F32)] * 2,
        compiler_params=_cparams(1),
        name="s5_scan",
    )(xr, xi, ar, ai)


def _s5_out_kernel(u_ref, bd_ref, pr_ref, pi_ref, cxr_ref, cxi_ref, y_ref):
    tt = S5_LAGPAD + 1
    prb, pib = pr_ref[...].astype(BF16), pi_ref[...].astype(BF16)
    for jt in range(S5_L // tt):
        osl = slice(jt * tt * LANE, (jt + 1) * tt * LANE)
        acc = (jnp.dot(prb, cxr_ref[0, :, osl], preferred_element_type=F32)
               + jnp.dot(pib, cxi_ref[0, :, osl], preferred_element_type=F32))
        for k in range(0, (jt + 1) * tt, 2):
            lag0 = jt * tt - k + S5_LAGPAD
            lags = jnp.concatenate([bd_ref[0, :, lag0 * LANE:(lag0 + tt) * LANE],
                                    bd_ref[0, :, (lag0 - 1) * LANE:(lag0 - 1 + tt) * LANE]], axis=0)
            acc = acc + jnp.dot(u_ref[0, :, k * LANE:(k + 2) * LANE], lags,
                                preferred_element_type=F32)
        y_ref[0, :, osl] = acc.astype(BF16)


def _s5_out(u8, bd, pr, pi, cxr8, cxi8):
    n_oct, nc, w = u8.shape
    ow = S5_OCT * S5_STATE
    return pl.pallas_call(
        _s5_out_kernel,
        grid=(n_oct,),
        in_specs=[pl.BlockSpec((1, nc, w), lambda i: (i, 0, 0)),
                  pl.BlockSpec((1, LANE, bd.shape[2]), lambda i: (i, 0, 0)),
                  pl.BlockSpec((nc, ow), lambda i: (0, i)),
                  pl.BlockSpec((nc, ow), lambda i: (0, i)),
                  pl.BlockSpec((1, ow, w), lambda i: (i, 0, 0)),
                  pl.BlockSpec((1, ow, w), lambda i: (i, 0, 0))],
        out_specs=pl.BlockSpec((1, nc, w), lambda i: (i, 0, 0)),
        out_shape=jax.ShapeDtypeStruct((n_oct, nc, w), BF16),
        compiler_params=_cparams(1),
        name="s5_out",
    )(u8, bd, pr, pi, cxr8, cxi8)


def _kprep_kernel(k_ref, km_ref, ka_ref):
    rows = km_ref.shape[0]
    lane = lax.broadcasted_iota(jnp.int32, (BLK, D), 1) % LANE
    even = lane < MOBA_HD
    for r in range(rows):
        k = k_ref[r * BLK:(r + 1) * BLK, :]
        km_ref[r:r + 1, :] = jnp.mean(k.astype(F32), axis=0, keepdims=True)
        blk = pl.program_id(0) * rows + r
        ka_ref[0, r * BLK:(r + 1) * BLK, :] = jnp.where(
            even, k, (lane - MOBA_HD == blk).astype(BF16))
        ka_ref[1, r * BLK:(r + 1) * BLK, :] = jnp.where(
            even, (lane == blk).astype(BF16), k)


def _kprep(proj, rows):
    s = proj.shape[0]
    nb = s // BLK
    assert nb <= MOBA_HD
    return pl.pallas_call(
        _kprep_kernel,
        grid=(nb // rows,),
        in_specs=[pl.BlockSpec((rows * BLK, D), lambda i: (i, C_K))],
        out_specs=[pl.BlockSpec((rows, D), lambda i: (i, 0)),
                   pl.BlockSpec((2, rows * BLK, D), lambda i: (0, i, 0))],
        out_shape=[jax.ShapeDtypeStruct((nb, D), F32),
                   jax.ShapeDtypeStruct((2, s, D), BF16)],
        compiler_params=_cparams(1),
        name="kprep",
    )(proj)


def _moba_kernel(q_ref, ka_ref, vt_ref, km_ref, o_ref, s_ref):
    i = pl.program_id(1)
    nb = km_ref.shape[0]
    kg = min(MOBA_KG, nb)
    kt = kg * BLK
    q = q_ref[...].astype(F32)
    lane = lax.broadcasted_iota(jnp.int32, (BLK, LANE), 1)
    blk = lax.broadcasted_iota(jnp.int32, (nb, BLK), 0)
    blk_f = blk.astype(F32)
    scale = MOBA_HD ** -0.5 * math.log2(math.e)
    qs, qs_own = [], []
    for h in range(2):
        in_head = (lane >= h * MOBA_HD) & (lane < (h + 1) * MOBA_HD)
        qh = jnp.where(in_head, q, 0.0)
        g = sum(lax.dot_general(t, qh.astype(BF16), NT, preferred_element_type=F32)
                for t in _split3(km_ref[...]))
        g = jnp.where(blk < i, g, -jnp.inf)
        sel = jnp.zeros((nb, BLK), jnp.bool_)
        for _ in range(MOBA_TOPK):
            mx = jnp.max(g, axis=0, keepdims=True)
            idx = jnp.min(jnp.where(g == mx, blk_f, float(nb)), axis=0, keepdims=True)
            hit = blk_f == idx
            sel = sel | (hit & (mx > -jnp.inf))
            g = jnp.where(hit, -jnp.inf, g)
        bias = jnp.where(sel, 0.0, NEG)
        pads = [jnp.full((MOBA_HD - nb, BLK), NEG, F32)] if nb < MOBA_HD else []
        other = jnp.zeros((MOBA_HD, BLK), F32)
        bias_q = jnp.concatenate(([other, bias] + pads) if h == 0 else ([bias] + pads + [other]),
                                 axis=0).T
        qs.append(jnp.where(in_head, qh * scale, bias_q).astype(BF16))
        qs_own.append((qh * scale).astype(BF16))

    krow = lax.broadcasted_iota(jnp.int32, (BLK, BLK), 0)
    qcol = lax.broadcasted_iota(jnp.int32, (BLK, BLK), 1)
    causal = krow <= qcol
    last_g = nb // kg - 1
    ones = lambda n: jnp.ones((16, n), BF16)

    ng = MOBA_TRIP
    last_trip = nb // kg // ng - 1

    def produce(tp, h):
        tcp = jnp.minimum(tp, last_trip)
        kb = ka_ref[h, pl.ds(pl.multiple_of(tcp * ng * kt, ng * kt), ng * kt), :]
        s_t = lax.dot_general(kb, qs[h], NT, preferred_element_type=F32)
        s_ref[h] = s_t
        return jnp.max(s_t, axis=0, keepdims=True)

    def step(tp, carry, width, look_ahead, own=False):
        state, maxima = carry
        start = i * BLK if own else jnp.minimum(tp, last_trip) * (ng * kt)
        vt = vt_ref[:, pl.ds(pl.multiple_of(start, BLK), width)]
        new_state, new_maxima = [], []
        for h in range(2):
            m, acc = state[h]
            mn = jnp.maximum(m, maxima[h])
            p = jnp.exp2(s_ref[h, 0:width, :] - mn).astype(BF16)
            new_maxima.append(produce(tp + 1, h) if look_ahead else maxima[h])
            va = jnp.concatenate([vt[h * MOBA_HD:(h + 1) * MOBA_HD, :], ones(width)], axis=0)
            new_state.append((mn, jnp.exp2(m - mn) * acc + jnp.dot(va, p, preferred_element_type=F32)))
        return tuple(new_state), tuple(new_maxima)

    own_max = []
    for h in range(2):
        kb_own = ka_ref[h, pl.ds(pl.multiple_of(i * BLK, BLK), BLK), :]
        s_own = jnp.where(causal, lax.dot_general(kb_own, qs_own[h], NT, preferred_element_type=F32), NEG)
        s_ref[h, 0:BLK, :] = s_own
        own_max.append(jnp.max(s_own, axis=0, keepdims=True))
    init = (jnp.full((1, BLK), NEG, F32), jnp.zeros((MOBA_HD + 16, BLK), F32))
    state, maxima = step(-1, ((init, init), tuple(own_max)), BLK, True, own=True)

    n_groups = (i + kg - 1) // kg
    carry = lax.fori_loop(0, n_groups // ng, lambda tp, c: step(tp, c, ng * kt, True),
                          (tuple(state), maxima))
    tails = [lambda c: c] + [functools.partial(lambda r, c: step(n_groups // ng, c, r * kt, False), r)
                             for r in range(1, ng)]
    carry, _ = lax.switch(n_groups % ng, tails, carry)
    for h in range(2):
        _, acc = carry[h]
        o_ref[h * MOBA_HD:(h + 1) * MOBA_HD, :] = (
            acc[0:MOBA_HD, :] / acc[MOBA_HD:MOBA_HD + 1, :]).astype(BF16)


def _moba(proj, kaug, vt, kmean):
    s = proj.shape[0]
    nb = s // BLK
    assert nb % (MOBA_TRIP * min(MOBA_KG, nb)) == 0
    per = D // LANE
    return pl.pallas_call(
        _moba_kernel,
        grid=(MOBA_HEADS // 2, nb),
        in_specs=[pl.BlockSpec((BLK, LANE), lambda hp, i: (i, C_Q * per + hp)),
                  pl.BlockSpec((2, s, LANE), lambda hp, i: (0, 0, hp)),
                  pl.BlockSpec((LANE, s), lambda hp, i: (hp, 0)),
                  pl.BlockSpec((nb, LANE), lambda hp, i: (0, hp))],
        out_specs=pl.BlockSpec((LANE, BLK), lambda hp, i: (hp, i)),
        out_shape=jax.ShapeDtypeStruct((D, s), BF16),
        scratch_shapes=[pltpu.VMEM((2, MOBA_TRIP * min(MOBA_KG, nb) * BLK, BLK), F32)],
        compiler_params=_cparams(2),
        name="moba",
    )(proj, kaug, vt, kmean)


def _merge_kernel(final, x_ref, ya_ref, ys_ref, u_ref, sg_ref, att_ref, mg_ref,
                  g0_ref, g1_ref, g2_ref, gb_ref, s5d_ref, glub_ref, fnw_ref,
                  gluw_ref, wa_ref, wb_ref, wc_ref, wo_ref, out_ref):
    mm = lambda a, w_ref: jnp.dot(a.astype(BF16), w_ref[...], preferred_element_type=F32)
    f32 = lambda r: r[...].astype(F32)
    yb = f32(ys_ref) + s5d_ref[...] * f32(u_ref)
    yb = jax.nn.gelu(yb)
    yb = yb * _sigmoid(mm(yb, gluw_ref) + glub_ref[...])
    sg = f32(sg_ref)
    yb = yb * (sg * _sigmoid(sg))
    mg = f32(mg_ref)
    att = f32(att_ref).T * (mg * _sigmoid(mg))
    merged = (_sigmoid(f32(g0_ref) + gb_ref[:, 0:D]) * mm(ya_ref[...], wa_ref)
              + _sigmoid(f32(g1_ref) + gb_ref[:, D:2 * D]) * mm(yb, wb_ref)
              + _sigmoid(f32(g2_ref) + gb_ref[:, 2 * D:3 * D]) * mm(att, wc_ref))
    xn = x_ref[...] + mm(merged, wo_ref)
    if final:
        xn = xn * lax.rsqrt(jnp.mean(xn * xn, axis=-1, keepdims=True) + EPS) * fnw_ref[...]
    out_ref[...] = xn


def _merge(final, x, ya, ys, proj, att_t, gate_b, s5d, glub, fnw, gluw, wa, wb, wc, wo, layer, tm):
    s = x.shape[0]
    rowb = pl.BlockSpec((tm, D), lambda i: (i, 0))
    pcol = lambda c: pl.BlockSpec((tm, D), lambda i, c=c: (i, c))
    wspec = pl.BlockSpec((None, D, D), lambda i: (layer, 0, 0), pipeline_mode=pl.Buffered(1))
    return pl.pallas_call(
        functools.partial(_merge_kernel, final),
        grid=(s // tm,),
        in_specs=[rowb, rowb, rowb, pcol(C_S5U), pcol(C_S5G),
                  pl.BlockSpec((D, tm), lambda i: (0, i)), pcol(C_MG),
                  pcol(C_GL), pcol(C_GL + 1), pcol(C_GL + 2),
                  _const_spec((1, 3 * D)), _const_spec((1, D)), _const_spec((1, D)),
                  _const_spec((1, D)), wspec, wspec, wspec, wspec, wspec],
        out_specs=rowb,
        out_shape=jax.ShapeDtypeStruct((s, D), F32),
        compiler_params=_cparams(1),
        name="merge",
    )(x, ya, ys, proj, proj, att_t, proj, proj, proj, proj,
      gate_b, s5d, glub, fnw, gluw, wa, wb, wc, wo)


def _block_diag_pairs(a):
    g, r, c = a.shape
    a2 = a.reshape(g // 2, 2, r, c)
    out = jnp.einsum("pgrc,gh->pgrhc", a2, jnp.eye(2, dtype=a.dtype))
    return out.reshape(g // 2, 2 * r, 2 * c)


def _lane_scatter():
    half = S5_OCT // 2
    sel = np.zeros((half, 2, S5_OCT), np.float32)
    for pr in range(half):
        for gi in range(2):
            sel[pr, gi, 2 * pr + gi] = 1.0
    eye_t, eye_c = np.eye(S5_L, dtype=np.float32), np.eye(S5_CH, dtype=np.float32)
    sc = np.einsum("pgh,tu,cd->pgtcuhd", sel, eye_t, eye_c)
    return jnp.asarray(sc.reshape(half, 2 * S5_L * S5_CH, S5_L * LANE), dtype=BF16)


def _row_tile(s, want):
    t = min(want, s)
    while s % t:
        t //= 2
    return t


def kernel(x, norm_w, w_in, gate_b, conv_w, conv_b, dt_bias, a_log, ssd_d, ssd_norm_w, w_proj_a,
           lambda_re, lambda_im, log_step, s5_b_re, s5_b_im, s5_c_re, s5_c_im, s5_d, glu_w, glu_b,
           w_proj_b, w_proj_c, w_out, final_norm_w):
    b, s, _ = x.shape
    assert b == 1 and s % BLK == 0 and x.shape[2] == D
    depth = norm_w.shape[0]
    nb = s // BLK
    nc = s // S5_L
    xc = x.reshape(s, D)
    o_dt = 3 * D
    o_s5u = o_dt + SSD_HEADS
    o_q = o_s5u + 2 * D
    o_v = o_q + 2 * D
    o_mg = o_v + D
    expand = (jnp.arange(LANE)[:, None] == (jnp.arange(D)[None, :] // SSD_HEAD_DIM)).astype(F32)
    pad16 = lambda a: jnp.pad(a.astype(F32), (0, LANE - SSD_HEADS)).reshape(1, LANE)
    tile_b = lambda a: jnp.tile(a.transpose(0, 2, 1), (1, S5_L, 1))
    tile_c = lambda a: jnp.tile(a.transpose(0, 2, 1), (1, 1, S5_L))
    scat = _lane_scatter()
    w_nat = jnp.concatenate([w_in[:, :, 0:o_dt], w_in[:, :, o_s5u:o_v], w_in[:, :, o_mg:]],
                            axis=2).astype(BF16)
    w_vt = w_in[:, :, o_v:o_mg]
    w_dt = jnp.pad(w_in[:, :, o_dt:o_s5u], ((0, 0), (0, 0), (0, LANE - SSD_HEADS)))
    merge_w = [a.astype(BF16) for a in (glu_w, w_proj_a, w_proj_b, w_proj_c, w_out)]
    pairs = depth * S5_PAIRS
    ls_full = jnp.repeat(log_step.reshape(-1), S5_STATE)
    s5_rows = [a.reshape(pairs, 1, LANE) for a in (lambda_re, lambda_im, ls_full)]
    s5_cols = [a.reshape(pairs, LANE, 1) for a in (lambda_re, lambda_im, ls_full)]
    s5_mats = ([_block_diag_pairs(tile_b(a.reshape(depth * S5_GROUPS, S5_STATE, S5_CH)))
                for a in (s5_b_re, s5_b_im)]
               + [_block_diag_pairs(tile_c(a.reshape(depth * S5_GROUPS, S5_CH, S5_STATE)))
                  for a in (s5_c_re, s5_c_im)])

    for l in range(depth):
        proj, vt, dt = _project(xc, norm_w[l].reshape(1, D), w_nat, w_vt, w_dt, l,
                                     _row_tile(s, 1024))

        ya = _ssd(proj, dt, conv_w[l], conv_b[l].reshape(1, 2 * D),
                  pad16(dt_bias[l]), pad16(a_log[l]),
                  jnp.repeat(ssd_d[l], SSD_HEAD_DIM).reshape(1, D),
                  ssd_norm_w[l].reshape(1, D), expand)

        bd, bxr8, bxi8, cxr8, cxi8, alr, ali = _s5_prep(*s5_rows, *s5_cols, *s5_mats, scat, l)
        n_oct = S5_GROUPS // S5_OCT
        u8 = (proj[:, C_S5U * D:(C_S5U + 1) * D].reshape(nc, S5_L, n_oct, LANE)
              .transpose(2, 0, 1, 3).reshape(n_oct, nc, S5_L * LANE))
        sr, si = _s5_local(u8, bxr8, bxi8)
        pr, pi = _s5_scan(sr, si, alr.reshape(1, -1), ali.reshape(1, -1), 512)
        y8 = _s5_out(u8, bd, pr, pi, cxr8, cxi8)
        ys = y8.reshape(n_oct, nc, S5_L, LANE).transpose(1, 2, 0, 3).reshape(s, D)

        km, kaug = _kprep(proj, math.gcd(nb, 8))
        att_t = _moba(proj, kaug, vt, km)

        xc = _merge(l == depth - 1, xc, ya, ys, proj, att_t, gate_b[l].reshape(1, 3 * D),
                    s5_d[l].reshape(1, D), glu_b[l].reshape(1, D), final_norm_w.reshape(1, D),
                    *merge_w, l, _row_tile(s, 512))
    return xc.reshape(b, s, D)
```

```python
import functools
import math

import jax
import jax.numpy as jnp
import numpy as np
from jax import lax
from jax.experimental import pallas as pl
from jax.experimental.pallas import tpu as pltpu

F32 = jnp.float32
BF16 = jnp.bfloat16
HI = lax.Precision.HIGHEST

D = 1024
EPS = 1e-6
BLK = 256
SSD_HEADS = 16
SSD_HEAD_DIM = 64
SSD_GROUPS = 4
SSD_STATE = 128
SSD_CONV = 4
S5_GROUPS = 64
S5_CH = 16
S5_STATE = 64
S5_CLIP = 1e-4
S5_L = 16
S5_PAIRS = S5_GROUPS // 2
S5_OCT = 8
MOBA_HEADS = 16
MOBA_HD = 64
MOBA_TOPK = 3
MOBA_KG = 4
MOBA_TRIP = 2
NEG = -1e30
LANE = 128
SUBLANE = 8
VMEM_LIMIT = 56 * 1024 * 1024

C_Z, C_XS, C_BC, C_S5U, C_S5G, C_Q, C_K, C_MG, C_GL = 0, 1, 2, 3, 4, 5, 6, 7, 8
N_NAT = 11 * D

NT = (((1,), (1,)), ((), ()))
TN = (((0,), (0,)), ((), ()))


def _cparams(n_axes):
    return pltpu.CompilerParams(dimension_semantics=("arbitrary",) * n_axes,
                                vmem_limit_bytes=VMEM_LIMIT)


def _const_spec(shape):
    nd = len(shape)
    return pl.BlockSpec(shape, lambda *_: (0,) * nd)


def _proj_kernel(x_ref, nw_ref, w_ref, wvt_ref, wdt_ref,
                 proj_ref, vt_ref, dt_ref, h_ref):
    @pl.when(pl.program_id(1) == 0)
    def _():
        x = x_ref[...]
        h = x * lax.rsqrt(jnp.mean(x * x, axis=-1, keepdims=True) + EPS) * nw_ref[...]
        hb = h.astype(BF16)
        h_ref[...] = hb
        vt = lax.dot_general(wvt_ref[...], hb, NT, preferred_element_type=F32)
        vt_ref[...] = vt.astype(BF16)
        dt_ref[...] = jnp.dot(h, wdt_ref[...], precision=HI, preferred_element_type=F32)

    proj_ref[...] = jnp.dot(h_ref[...], w_ref[...], preferred_element_type=F32).astype(BF16)


def _project(x, norm_w, w_nat, w_vt, w_dt, layer, tm):
    s = x.shape[0]
    grid = (s // tm, N_NAT // D)
    return pl.pallas_call(
        _proj_kernel,
        grid=grid,
        in_specs=[
            pl.BlockSpec((tm, D), lambda i, j: (i, 0)),
            _const_spec((1, D)),
            pl.BlockSpec((None, D, D), lambda i, j: (layer, 0, j)),
            pl.BlockSpec((None, D, D), lambda i, j: (layer, 0, 0)),
            pl.BlockSpec((None, D, LANE), lambda i, j: (layer, 0, 0)),
        ],
        out_specs=[
            pl.BlockSpec((tm, D), lambda i, j: (i, j)),
            pl.BlockSpec((D, tm), lambda i, j: (0, i)),
            pl.BlockSpec((tm, LANE), lambda i, j: (i, 0)),
        ],
        out_shape=[
            jax.ShapeDtypeStruct((s, N_NAT), BF16),
            jax.ShapeDtypeStruct((D, s), BF16),
            jax.ShapeDtypeStruct((s, LANE), F32),
        ],
        scratch_shapes=[pltpu.VMEM((tm, D), BF16)],
        compiler_params=_cparams(2),
        name="proj",
    )(x, norm_w, w_nat, w_vt, w_dt)


def _sigmoid(v):
    return 0.5 * jnp.tanh(0.5 * v) + 0.5


def _split3(v):
    hi = v.astype(BF16)
    r1 = v - hi.astype(F32)
    mid = r1.astype(BF16)
    lo = (r1 - mid.astype(F32)).astype(BF16)
    return hi, mid, lo


def _dot_exact_lhs(a_exact, v):
    ab = a_exact.astype(BF16)
    return sum(jnp.dot(ab, t, preferred_element_type=F32) for t in _split3(v))


def _dot_exact_rhs(v, b_exact):
    bb = b_exact.astype(BF16)
    return sum(jnp.dot(t, bb, preferred_element_type=F32) for t in _split3(v))


def _softplus(v):
    return jnp.maximum(v, 0.0) + jnp.log1p(jnp.exp(-jnp.abs(v)))


def _ssd_kernel(z_ref, xs_ref, bc_ref, dt_ref, cw_ref, cb_ref, dtb_ref,
                al_ref, dsk_ref, nw_ref, e_ref, out_ref, ext_ref, st_ref):
    @pl.when(pl.program_id(0) == 0)
    def _():
        ext_ref[0:8, :] = jnp.zeros((8, 2 * D), F32)
        st_ref[...] = jnp.zeros(st_ref.shape, F32)

    ext_ref[8:8 + BLK, 0:D] = xs_ref[...].astype(F32)
    ext_ref[8:8 + BLK, D:2 * D] = bc_ref[...].astype(F32)
    acc = cb_ref[...] + cw_ref[0:1, :] * ext_ref[5:5 + BLK, :]
    for i in range(1, SSD_CONV):
        acc = acc + cw_ref[i:i + 1, :] * ext_ref[5 + i:5 + i + BLK, :]
    ext_ref[0:8, :] = ext_ref[BLK:BLK + 8, :]
    act = acc * _sigmoid(acc)
    xs = act[:, 0:D]

    dt = _softplus(dt_ref[...] + dtb_ref[...])
    da = dt * (-jnp.exp(al_ref[...]))
    row = lax.broadcasted_iota(jnp.int32, (BLK, BLK), 0)
    col = lax.broadcasted_iota(jnp.int32, (BLK, BLK), 1)
    lower = row >= col
    tri = lower.astype(F32)
    a_cum = _dot_exact_lhs(tri, da)
    a_cumt = a_cum.T[0:SSD_HEADS, :]
    xdt = xs * _dot_exact_rhs(dt, e_ref[...])
    xdt_b = xdt.astype(BF16)

    lane = lax.broadcasted_iota(jnp.int32, (BLK, LANE), 1)
    first = lane < SSD_HEAD_DIM
    rowp = lax.broadcasted_iota(jnp.int32, (LANE, 1), 0) < SSD_HEAD_DIM
    y_parts = []
    for g in range(SSD_GROUPS):
        b_g = act[:, D + g * SSD_STATE:D + (g + 1) * SSD_STATE]
        c_g = act[:, D + SSD_GROUPS * SSD_STATE + g * SSD_STATE:
                  D + SSD_GROUPS * SSD_STATE + (g + 1) * SSD_STATE]
        b_gb = b_g.astype(BF16)
        c_gb = c_g.astype(BF16)
        cbm = lax.dot_general(c_gb, b_gb, NT, preferred_element_type=F32)
        for pp in range(2):
            pair = 2 * g + pp
            h0 = 2 * pair
            x_pair = xdt_b[:, pair * LANE:(pair + 1) * LANE]
            y_pair = jnp.zeros((BLK, LANE), F32)
            for hh in range(2):
                h = h0 + hh
                lm = jnp.where(lower, jnp.exp(a_cum[:, h:h + 1] - a_cumt[h:h + 1, :]), 0.0)
                w = (cbm * lm).astype(BF16)
                xm = jnp.where(first if hh == 0 else jnp.logical_not(first), x_pair,
                               jnp.zeros_like(x_pair))
                y_pair = y_pair + jnp.dot(w, xm, preferred_element_type=F32)
            ac_pair = jnp.where(first, a_cum[:, h0:h0 + 1], a_cum[:, h0 + 1:h0 + 2])
            al_pair = ac_pair[BLK - 1:BLK, :]
            r_pair = st_ref[pair]
            y_off = lax.dot_general(c_gb, r_pair.astype(BF16), NT, preferred_element_type=F32)
            y_pair = y_pair + y_off * jnp.exp(ac_pair)
            xdec = (xdt[:, pair * LANE:(pair + 1) * LANE] * jnp.exp(al_pair - ac_pair)).astype(BF16)
            st_new = lax.dot_general(xdec, b_gb, TN, preferred_element_type=F32)
            al_col = jnp.where(rowp, a_cumt[h0:h0 + 1, BLK - 1:BLK],
                               a_cumt[h0 + 1:h0 + 2, BLK - 1:BLK])
            st_ref[pair] = jnp.exp(al_col) * r_pair + st_new
            y_parts.append(y_pair)
    y = jnp.concatenate(y_parts, axis=1) + dsk_ref[...] * xs
    z = z_ref[...].astype(F32)
    y = y * (z * _sigmoid(z))
    gw = D // SSD_GROUPS
    for g in range(SSD_GROUPS):
        yg = y[:, g * gw:(g + 1) * gw]
        yg = yg * lax.rsqrt(jnp.mean(yg * yg, axis=-1, keepdims=True) + EPS)
        out_ref[:, g * gw:(g + 1) * gw] = (yg * nw_ref[:, g * gw:(g + 1) * gw]).astype(BF16)


def _ssd(proj, dt, cw, cb, dtb, al, dsk, nw, expand):
    s = proj.shape[0]
    row = lambda c: pl.BlockSpec((BLK, D), lambda i, c=c: (i, c))
    return pl.pallas_call(
        _ssd_kernel,
        grid=(s // BLK,),
        in_specs=[
            row(C_Z), row(C_XS), row(C_BC),
            pl.BlockSpec((BLK, LANE), lambda i: (i, 0)),
            _const_spec((SSD_CONV, 2 * D)), _const_spec((1, 2 * D)),
            _const_spec((1, LANE)), _const_spec((1, LANE)),
            _const_spec((1, D)), _const_spec((1, D)), _const_spec((LANE, D)),
        ],
        out_specs=pl.BlockSpec((BLK, D), lambda i: (i, 0)),
        out_shape=jax.ShapeDtypeStruct((s, D), BF16),
        scratch_shapes=[pltpu.VMEM((BLK + 8, 2 * D), F32),
                        pltpu.VMEM((SSD_HEADS // 2, LANE, SSD_STATE), F32)],
        compiler_params=_cparams(1),
        name="ssd",
    )(proj, proj, proj, dt, cw, cb, dtb, al, dsk, nw, expand)


S5_LAGPAD = 3


def _s5_prep_kernel(lrr_ref, lir_ref, lsr_ref, lrc_ref, lic_ref, lsc_ref,
                    bre_ref, bim_ref, cre_ref, cim_ref, scat_ref,
                    bd_ref, bxr_ref, bxi_ref, cxr_ref, cxi_ref, alr_ref, ali_ref):
    n = S5_L * S5_CH
    rows = 2 * n
    cols = 2 * n
    bd_ref[0, :, 0:S5_LAGPAD * LANE] = jnp.zeros((LANE, S5_LAGPAD * LANE), BF16)
    bxr_ref[...] = jnp.zeros(bxr_ref.shape, BF16)
    bxi_ref[...] = jnp.zeros(bxi_ref.shape, BF16)
    lane128 = lax.broadcasted_iota(jnp.int32, (S5_CH, LANE), 1) // S5_CH
    ek = (S5_L - 1 - lax.broadcasted_iota(jnp.int32, (S5_L, 1), 0)).astype(F32)
    et = (lax.broadcasted_iota(jnp.int32, (1, S5_L), 1) + 1).astype(F32)
    row_k = (lax.broadcasted_iota(jnp.int32, (rows, S5_L), 0) % n) // S5_CH
    spread_rows = (row_k == lax.broadcasted_iota(jnp.int32, (rows, S5_L), 1)).astype(F32)
    col_t = (lax.broadcasted_iota(jnp.int32, (S5_L, cols), 1) % n) // S5_CH
    spread_cols = (col_t == lax.broadcasted_iota(jnp.int32, (S5_L, cols), 0)).astype(F32)
    for pr in range(S5_OCT // 2):
        lr = jnp.minimum(lrr_ref[pr], -S5_CLIP)
        li = lir_ref[pr]
        st = jnp.exp(lsr_ref[pr])
        lrs, lis = lr * st, li * st
        mag = jnp.exp(lrs)
        abr, abi = mag * jnp.cos(lis), mag * jnp.sin(lis)
        nr, ni = abr - 1.0, abi
        den = lr * lr + li * li
        cfr = (nr * lr + ni * li) / den
        cfi = (ni * lr - nr * li) / den
        bre, bim = bre_ref[pr], bim_ref[pr]
        bbr = cfr * bre - cfi * bim
        bbi = cfr * bim + cfi * bre
        pm = jnp.exp(ek * lrs)
        pbr = _dot_exact_lhs(spread_rows, pm * jnp.cos(ek * lis))
        pbi = _dot_exact_lhs(spread_rows, pm * jnp.sin(ek * lis))
        bxr = pbr * bbr - pbi * bbi
        bxi = pbr * bbi + pbi * bbr
        alm = jnp.exp(float(S5_L) * lrs)
        alr_ref[pr] = alm * jnp.cos(float(S5_L) * lis)
        ali_ref[pr] = alm * jnp.sin(float(S5_L) * lis)
        lrc = jnp.minimum(lrc_ref[pr], -S5_CLIP)
        stc = jnp.exp(lsc_ref[pr])
        lrsc, lisc = lrc * stc, lic_ref[pr] * stc
        qm = jnp.exp(et * lrsc)
        qr = _dot_exact_rhs(qm * jnp.cos(et * lisc), spread_cols)
        qi = _dot_exact_rhs(qm * jnp.sin(et * lisc), spread_cols)
        cre, cim = cre_ref[pr], cim_ref[pr]
        rsl = slice(pr * LANE, (pr + 1) * LANE)
        cxr_ref[0, rsl, :] = jnp.dot((cre * qr - cim * qi).astype(BF16), scat_ref[pr],
                                     preferred_element_type=F32).astype(BF16)
        cxi_ref[0, rsl, :] = jnp.dot((-(cre * qi + cim * qr)).astype(BF16), scat_ref[pr],
                                     preferred_element_type=F32).astype(BF16)
        bxr_b, bxi_b = bxr.astype(BF16), bxi.astype(BF16)
        for gi in range(2):
            g8 = 2 * pr + gi
            gr, gc = slice(gi * n, (gi + 1) * n), slice(gi * n, gi * n + LANE)
            rw = (jnp.dot(bxr[gr, :], cre[:, gc], precision=HI, preferred_element_type=F32)
                  - jnp.dot(bxi[gr, :], cim[:, gc], precision=HI, preferred_element_type=F32))
            for k in range(S5_L):
                src = slice(gi * n + k * S5_CH, gi * n + (k + 1) * S5_CH)
                dst = slice(k * LANE + g8 * S5_CH, k * LANE + (g8 + 1) * S5_CH)
                bxr_ref[0, dst, rsl] = bxr_b[src, :]
                bxi_ref[0, dst, rsl] = bxi_b[src, :]
            for j in range(S5_L):
                m = S5_L - 1 - j
                blk = rw[m * S5_CH:(m + 1) * S5_CH, :]
                bd_ref[0, g8 * S5_CH:(g8 + 1) * S5_CH,
                       (S5_LAGPAD + j) * LANE:(S5_LAGPAD + j + 1) * LANE] = jnp.where(
                           lane128 == g8, blk, 0.0).astype(BF16)


def _s5_prep(lrr, lir, lsr, lrc, lic, lsc, bre2, bim2, cre2, cim2, scat, layer):
    n = S5_L * S5_CH
    half = S5_OCT // 2
    n_oct = S5_GROUPS // S5_OCT
    w8 = S5_L * LANE
    ow = S5_OCT * S5_STATE
    blk4 = lambda a, b: pl.BlockSpec((half, a, b), lambda i: (layer * n_oct + i, 0, 0))
    out4 = lambda a, b: pl.BlockSpec((half, a, b), lambda i: (i, 0, 0))
    oct3 = lambda a, b: pl.BlockSpec((1, a, b), lambda i: (i, 0, 0))
    return pl.pallas_call(
        _s5_prep_kernel,
        grid=(n_oct,),
        in_specs=[blk4(1, LANE)] * 3 + [blk4(LANE, 1)] * 3
                 + [blk4(2 * n, LANE)] * 2 + [blk4(LANE, 2 * n)] * 2
                 + [_const_spec((half, 2 * n, w8))],
        out_specs=[oct3(LANE, (S5_LAGPAD + S5_L) * LANE),
                   oct3(w8, ow), oct3(w8, ow), oct3(ow, w8), oct3(ow, w8),
                   out4(1, LANE), out4(1, LANE)],
        out_shape=[jax.ShapeDtypeStruct((n_oct, LANE, (S5_LAGPAD + S5_L) * LANE), BF16),
                   jax.ShapeDtypeStruct((n_oct, w8, ow), BF16),
                   jax.ShapeDtypeStruct((n_oct, w8, ow), BF16),
                   jax.ShapeDtypeStruct((n_oct, ow, w8), BF16),
                   jax.ShapeDtypeStruct((n_oct, ow, w8), BF16),
                   jax.ShapeDtypeStruct((S5_PAIRS, 1, LANE), F32),
                   jax.ShapeDtypeStruct((S5_PAIRS, 1, LANE), F32)],
        compiler_params=_cparams(1),
        name="s5_prep",
    )(lrr, lir, lsr, lrc, lic, lsc, bre2, bim2, cre2, cim2, scat)


def _s5_local_kernel(u_ref, bxr_ref, bxi_ref, sr_ref, si_ref):
    sr_ref[...] = jnp.dot(u_ref[0], bxr_ref[0], preferred_element_type=F32)
    si_ref[...] = jnp.dot(u_ref[0], bxi_ref[0], preferred_element_type=F32)


def _s5_local(u8, bxr8, bxi8):
    nc, w = u8.shape[1], u8.shape[2]
    ow = S5_OCT * S5_STATE
    return pl.pallas_call(
        _s5_local_kernel,
        grid=(S5_GROUPS // S5_OCT,),
        in_specs=[pl.BlockSpec((1, nc, w), lambda i: (i, 0, 0)),
                  pl.BlockSpec((1, w, ow), lambda i: (i, 0, 0)),
                  pl.BlockSpec((1, w, ow), lambda i: (i, 0, 0))],
        out_specs=[pl.BlockSpec((nc, ow), lambda i: (0, i))] * 2,
        out_shape=[jax.ShapeDtypeStruct((nc, S5_GROUPS * S5_STATE), F32)] * 2,
        compiler_params=_cparams(1),
        name="s5_local",
    )(u8, bxr8, bxi8)


def _s5_scan_kernel(xr_ref, xi_ref, ar_ref, ai_ref, pr_ref, pi_ref):
    ar, ai = ar_ref[...], ai_ref[...]
    w = ar.shape[1]

    def body(b, carry):
        sr, si = carry
        r0 = pl.multiple_of(b * SUBLANE, SUBLANE)
        xr = xr_ref[pl.ds(r0, SUBLANE), :]
        xi = xi_ref[pl.ds(r0, SUBLANE), :]
        out_r, out_i = [], []
        for j in range(SUBLANE):
            out_r.append(sr)
            out_i.append(si)
            sr, si = ar * sr - ai * si + xr[j:j + 1, :], ar * si + ai * sr + xi[j:j + 1, :]
        pr_ref[pl.ds(r0, SUBLANE), :] = jnp.concatenate(out_r, axis=0)
        pi_ref[pl.ds(r0, SUBLANE), :] = jnp.concatenate(out_i, axis=0)
        return sr, si

    assert xr_ref.shape[0] % SUBLANE == 0
    lax.fori_loop(0, xr_ref.shape[0] // SUBLANE, body,
                  (jnp.zeros((1, w), F32), jnp.zeros((1, w), F32)))


def _s5_scan(xr, xi, ar, ai, tw):
    nc, width = xr.shape
    col = pl.BlockSpec((nc, tw), lambda i: (0, i))
    vec = pl.BlockSpec((1, tw), lambda i: (0, i))
    return pl.pallas_call(
        _s5_scan_kernel,
        grid=(width // tw,),
        in_specs=[col, col, vec, vec],
        out_specs=[col, col],
        out_shape=[jax.ShapeDtypeStruct((nc, width), F32)] * 2,
        compiler_params=_cparams(1),
        name="s5_scan",
    )(xr, xi, ar, ai)


def _s5_out_kernel(u_ref, bd_ref, pr_ref, pi_ref, cxr_ref, cxi_ref, y_ref):
    tt = S5_LAGPAD + 1
    prb, pib = pr_ref[...].astype(BF16), pi_ref[...].astype(BF16)
    for jt in range(S5_L // tt):
        osl = slice(jt * tt * LANE, (jt + 1) * tt * LANE)
        acc = (jnp.dot(prb, cxr_ref[0, :, osl], preferred_element_type=F32)
               + jnp.dot(pib, cxi_ref[0, :, osl], preferred_element_type=F32))
        for k in range(0, (jt + 1) * tt, 2):
            lag0 = jt * tt - k + S5_LAGPAD
            lags = jnp.concatenate([bd_ref[0, :, lag0 * LANE:(lag0 + tt) * LANE],
                                    bd_ref[0, :, (lag0 - 1) * LANE:(lag0 - 1 + tt) * LANE]], axis=0)
            acc = acc + jnp.dot(u_ref[0, :, k * LANE:(k + 2) * LANE], lags,
                                preferred_element_type=F32)
        y_ref[0, :, osl] = acc.astype(BF16)


def _s5_out(u8, bd, pr, pi, cxr8, cxi8):
    n_oct, nc, w = u8.shape
    ow = S5_OCT * S5_STATE
    return pl.pallas_call(
        _s5_out_kernel,
        grid=(n_oct,),
        in_specs=[pl.BlockSpec((1, nc, w), lambda i: (i, 0, 0)),
                  pl.BlockSpec((1, LANE, bd.shape[2]), lambda i: (i, 0, 0)),
                  pl.BlockSpec((nc, ow), lambda i: (0, i)),
                  pl.BlockSpec((nc, ow), lambda i: (0, i)),
                  pl.BlockSpec((1, ow, w), lambda i: (i, 0, 0)),
                  pl.BlockSpec((1, ow, w), lambda i: (i, 0, 0))],
        out_specs=pl.BlockSpec((1, nc, w), lambda i: (i, 0, 0)),
        out_shape=jax.ShapeDtypeStruct((n_oct, nc, w), BF16),
        compiler_params=_cparams(1),
        name="s5_out",
    )(u8, bd, pr, pi, cxr8, cxi8)


def _kprep_kernel(k_ref, km_ref, ka_ref):
    rows = km_ref.shape[0]
    lane = lax.broadcasted_iota(jnp.int32, (BLK, D), 1) % LANE
    even = lane < MOBA_HD
    for r in range(rows):
        k = k_ref[r * BLK:(r + 1) * BLK, :]
        km_ref[r:r + 1, :] = jnp.mean(k.astype(F32), axis=0, keepdims=True)
        blk = pl.program_id(0) * rows + r
        ka_ref[0, r * BLK:(r + 1) * BLK, :] = jnp.where(
            even, k, (lane - MOBA_HD == blk).astype(BF16))
        ka_ref[1, r * BLK:(r + 1) * BLK, :] = jnp.where(
            even, (lane == blk).astype(BF16), k)


def _kprep(proj, rows):
    s = proj.shape[0]
    nb = s // BLK
    assert nb <= MOBA_HD
    return pl.pallas_call(
        _kprep_kernel,
        grid=(nb // rows,),
        in_specs=[pl.BlockSpec((rows * BLK, D), lambda i: (i, C_K))],
        out_specs=[pl.BlockSpec((rows, D), lambda i: (i, 0)),
                   pl.BlockSpec((2, rows * BLK, D), lambda i: (0, i, 0))],
        out_shape=[jax.ShapeDtypeStruct((nb, D), F32),
                   jax.ShapeDtypeStruct((2, s, D), BF16)],
        compiler_params=_cparams(1),
        name="kprep",
    )(proj)


def _moba_kernel(q_ref, ka_ref, vt_ref, km_ref, o_ref, s_ref):
    i = pl.program_id(1)
    nb = km_ref.shape[0]
    kg = min(MOBA_KG, nb)
    kt = kg * BLK
    q = q_ref[...].astype(F32)
    lane = lax.broadcasted_iota(jnp.int32, (BLK, LANE), 1)
    blk = lax.broadcasted_iota(jnp.int32, (nb, BLK), 0)
    blk_f = blk.astype(F32)
    scale = MOBA_HD ** -0.5 * math.log2(math.e)
    qs, qs_own = [], []
    for h in range(2):
        in_head = (lane >= h * MOBA_HD) & (lane < (h + 1) * MOBA_HD)
        qh = jnp.where(in_head, q, 0.0)
        g = sum(lax.dot_general(t, qh.astype(BF16), NT, preferred_element_type=F32)
                for t in _split3(km_ref[...]))
        g = jnp.where(blk < i, g, -jnp.inf)
        sel = jnp.zeros((nb, BLK), jnp.bool_)
        for _ in range(MOBA_TOPK):
            mx = jnp.max(g, axis=0, keepdims=True)
            idx = jnp.min(jnp.where(g == mx, blk_f, float(nb)), axis=0, keepdims=True)
            hit = blk_f == idx
            sel = sel | (hit & (mx > -jnp.inf))
            g = jnp.where(hit, -jnp.inf, g)
        bias = jnp.where(sel, 0.0, NEG)
        pads = [jnp.full((MOBA_HD - nb, BLK), NEG, F32)] if nb < MOBA_HD else []
        other = jnp.zeros((MOBA_HD, BLK), F32)
        bias_q = jnp.concatenate(([other, bias] + pads) if h == 0 else ([bias] + pads + [other]),
                                 axis=0).T
        qs.append(jnp.where(in_head, qh * scale, bias_q).astype(BF16))
        qs_own.append((qh * scale).astype(BF16))

    krow = lax.broadcasted_iota(jnp.int32, (BLK, BLK), 0)
    qcol = lax.broadcasted_iota(jnp.int32, (BLK, BLK), 1)
    causal = krow <= qcol
    last_g = nb // kg - 1
    ones = lambda n: jnp.ones((16, n), BF16)

    ng = MOBA_TRIP
    last_trip = nb // kg // ng - 1

    def produce(tp, h):
        tcp = jnp.minimum(tp, last_trip)
        kb = ka_ref[h, pl.ds(pl.multiple_of(tcp * ng * kt, ng * kt), ng * kt), :]
        s_t = lax.dot_general(kb, qs[h], NT, preferred_element_type=F32)
        s_ref[h] = s_t
        return jnp.max(s_t, axis=0, keepdims=True)

    def step(tp, carry, width, look_ahead, own=False):
        state, maxima = carry
        start = i * BLK if own else jnp.minimum(tp, last_trip) * (ng * kt)
        vt = vt_ref[:, pl.ds(pl.multiple_of(start, BLK), width)]
        new_state, new_maxima = [], []
        for h in range(2):
            m, acc = state[h]
            mn = jnp.maximum(m, maxima[h])
            p = jnp.exp2(s_ref[h, 0:width, :] - mn).astype(BF16)
            new_maxima.append(produce(tp + 1, h) if look_ahead else maxima[h])
            va = jnp.concatenate([vt[h * MOBA_HD:(h + 1) * MOBA_HD, :], ones(width)], axis=0)
            new_state.append((mn, jnp.exp2(m - mn) * acc + jnp.dot(va, p, preferred_element_type=F32)))
        return tuple(new_state), tuple(new_maxima)

    own_max = []
    for h in range(2):
        kb_own = ka_ref[h, pl.ds(pl.multiple_of(i * BLK, BLK), BLK), :]
        s_own = jnp.where(causal, lax.dot_general(kb_own, qs_own[h], NT, preferred_element_type=F32), NEG)
        s_ref[h, 0:BLK, :] = s_own
        own_max.append(jnp.max(s_own, axis=0, keepdims=True))
    init = (jnp.full((1, BLK), NEG, F32), jnp.zeros((MOBA_HD + 16, BLK), F32))
    state, maxima = step(-1, ((init, init), tuple(own_max)), BLK, True, own=True)

    n_groups = (i + kg - 1) // kg
    carry = lax.fori_loop(0, n_groups // ng, lambda tp, c: step(tp, c, ng * kt, True),
                          (tuple(state), maxima))
    tails = [lambda c: c] + [functools.partial(lambda r, c: step(n_groups // ng, c, r * kt, False), r)
                             for r in range(1, ng)]
    carry, _ = lax.switch(n_groups % ng, tails, carry)
    for h in range(2):
        _, acc = carry[h]
        o_ref[h * MOBA_HD:(h + 1) * MOBA_HD, :] = (
            acc[0:MOBA_HD, :] / acc[MOBA_HD:MOBA_HD + 1, :]).astype(BF16)


def _moba(proj, kaug, vt, kmean):
    s = proj.shape[0]
    nb = s // BLK
    assert nb % (MOBA_TRIP * min(MOBA_KG, nb)) == 0
    per = D // LANE
    return pl.pallas_call(
        _moba_kernel,
        grid=(MOBA_HEADS // 2, nb),
        in_specs=[pl.BlockSpec((BLK, LANE), lambda hp, i: (i, C_Q * per + hp)),
                  pl.BlockSpec((2, s, LANE), lambda hp, i: (0, 0, hp)),
                  pl.BlockSpec((LANE, s), lambda hp, i: (hp, 0)),
                  pl.BlockSpec((nb, LANE), lambda hp, i: (0, hp))],
        out_specs=pl.BlockSpec((LANE, BLK), lambda hp, i: (hp, i)),
        out_shape=jax.ShapeDtypeStruct((D, s), BF16),
        scratch_shapes=[pltpu.VMEM((2, MOBA_TRIP * min(MOBA_KG, nb) * BLK, BLK), F32)],
        compiler_params=_cparams(2),
        name="moba",
    )(proj, kaug, vt, kmean)


def _merge_kernel(final, x_ref, ya_ref, ys_ref, u_ref, sg_ref, att_ref, mg_ref,
                  g0_ref, g1_ref, g2_ref, gb_ref, s5d_ref, glub_ref, fnw_ref,
                  gluw_ref, wa_ref, wb_ref, wc_ref, wo_ref, out_ref):
    mm = lambda a, w_ref: jnp.dot(a.astype(BF16), w_ref[...], preferred_element_type=F32)
    sub = min(BLK, x_ref.shape[0])

    def rows_body(r, carry):
        rows = pl.ds(pl.multiple_of(r * sub, sub), sub)
        f32 = lambda ref: ref[rows, :].astype(F32)
        yb = f32(ys_ref) + s5d_ref[...] * f32(u_ref)
        yb = jax.nn.gelu(yb)
        yb = yb * _sigmoid(mm(yb, gluw_ref) + glub_ref[...])
        sg = f32(sg_ref)
        yb = yb * (sg * _sigmoid(sg))
        mg = f32(mg_ref)
        att = att_ref[:, rows].astype(F32).T * (mg * _sigmoid(mg))
        merged = (_sigmoid(f32(g0_ref) + gb_ref[:, 0:D]) * mm(ya_ref[rows, :], wa_ref)
                  + _sigmoid(f32(g1_ref) + gb_ref[:, D:2 * D]) * mm(yb, wb_ref)
                  + _sigmoid(f32(g2_ref) + gb_ref[:, 2 * D:3 * D]) * mm(att, wc_ref))
        xn = x_ref[rows, :] + mm(merged, wo_ref)
        if final:
            xn = xn * lax.rsqrt(jnp.mean(xn * xn, axis=-1, keepdims=True) + EPS) * fnw_ref[...]
        out_ref[rows, :] = xn
        return carry

    lax.fori_loop(0, x_ref.shape[0] // sub, rows_body, 0)


def _merge(final, x, ya, ys, proj, att_t, gate_b, s5d, glub, fnw, gluw, wa, wb, wc, wo, layer, tm):
    s = x.shape[0]
    rowb = pl.BlockSpec((tm, D), lambda i: (i, 0))
    pcol = lambda c: pl.BlockSpec((tm, D), lambda i, c=c: (i, c))
    wspec = pl.BlockSpec((None, D, D), lambda i: (layer, 0, 0), pipeline_mode=pl.Buffered(1))
    return pl.pallas_call(
        functools.partial(_merge_kernel, final),
        grid=(s // tm,),
        in_specs=[rowb, rowb, rowb, pcol(C_S5U), pcol(C_S5G),
                  pl.BlockSpec((D, tm), lambda i: (0, i)), pcol(C_MG),
                  pcol(C_GL), pcol(C_GL + 1), pcol(C_GL + 2),
                  _const_spec((1, 3 * D)), _const_spec((1, D)), _const_spec((1, D)),
                  _const_spec((1, D)), wspec, wspec, wspec, wspec, wspec],
        out_specs=rowb,
        out_shape=jax.ShapeDtypeStruct((s, D), F32),
        compiler_params=_cparams(1),
        name="merge",
    )(x, ya, ys, proj, proj, att_t, proj, proj, proj, proj,
      gate_b, s5d, glub, fnw, gluw, wa, wb, wc, wo)


def _block_diag_pairs(a):
    g, r, c = a.shape
    a2 = a.reshape(g // 2, 2, r, c)
    out = jnp.einsum("pgrc,gh->pgrhc", a2, jnp.eye(2, dtype=a.dtype))
    return out.reshape(g // 2, 2 * r, 2 * c)


def _lane_scatter():
    half = S5_OCT // 2
    sel = np.zeros((half, 2, S5_OCT), np.float32)
    for pr in range(half):
        for gi in range(2):
            sel[pr, gi, 2 * pr + gi] = 1.0
    eye_t, eye_c = np.eye(S5_L, dtype=np.float32), np.eye(S5_CH, dtype=np.float32)
    sc = np.einsum("pgh,tu,cd->pgtcuhd", sel, eye_t, eye_c)
    return jnp.asarray(sc.reshape(half, 2 * S5_L * S5_CH, S5_L * LANE), dtype=BF16)


def _row_tile(s, want):
    t = min(want, s)
    while s % t:
        t //= 2
    return t


def kernel(x, norm_w, w_in, gate_b, conv_w, conv_b, dt_bias, a_log, ssd_d, ssd_norm_w, w_proj_a,
           lambda_re, lambda_im, log_step, s5_b_re, s5_b_im, s5_c_re, s5_c_im, s5_d, glu_w, glu_b,
           w_proj_b, w_proj_c, w_out, final_norm_w):
    b, s, _ = x.shape
    assert b == 1 and s % BLK == 0 and x.shape[2] == D
    depth = norm_w.shape[0]
    nb = s // BLK
    nc = s // S5_L
    xc = x.reshape(s, D)
    o_dt = 3 * D
    o_s5u = o_dt + SSD_HEADS
    o_q = o_s5u + 2 * D
    o_v = o_q + 2 * D
    o_mg = o_v + D
    expand = (jnp.arange(LANE)[:, None] == (jnp.arange(D)[None, :] // SSD_HEAD_DIM)).astype(F32)
    pad16 = lambda a: jnp.pad(a.astype(F32), (0, LANE - SSD_HEADS)).reshape(1, LANE)
    tile_b = lambda a: jnp.tile(a.transpose(0, 2, 1), (1, S5_L, 1))
    tile_c = lambda a: jnp.tile(a.transpose(0, 2, 1), (1, 1, S5_L))
    scat = _lane_scatter()
    w_nat = jnp.concatenate([w_in[:, :, 0:o_dt], w_in[:, :, o_s5u:o_v], w_in[:, :, o_mg:]],
                            axis=2).astype(BF16)
    w_vt = jnp.swapaxes(w_in[:, :, o_v:o_mg], 1, 2).astype(BF16)
    w_dt = jnp.pad(w_in[:, :, o_dt:o_s5u], ((0, 0), (0, 0), (0, LANE - SSD_HEADS)))
    merge_w = [a.astype(BF16) for a in (glu_w, w_proj_a, w_proj_b, w_proj_c, w_out)]
    pairs = depth * S5_PAIRS
    ls_full = jnp.repeat(log_step.reshape(-1), S5_STATE)
    s5_rows = [a.reshape(pairs, 1, LANE) for a in (lambda_re, lambda_im, ls_full)]
    s5_cols = [a.reshape(pairs, LANE, 1) for a in (lambda_re, lambda_im, ls_full)]
    s5_mats = ([_block_diag_pairs(tile_b(a.reshape(depth * S5_GROUPS, S5_STATE, S5_CH)))
                for a in (s5_b_re, s5_b_im)]
               + [_block_diag_pairs(tile_c(a.reshape(depth * S5_GROUPS, S5_CH, S5_STATE)))
                  for a in (s5_c_re, s5_c_im)])

    for l in range(depth):
        proj, vt, dt = _project(xc, norm_w[l].reshape(1, D), w_nat, w_vt, w_dt, l,
                                     _row_tile(s, 1024))

        ya = _ssd(proj, dt, conv_w[l], conv_b[l].reshape(1, 2 * D),
                  pad16(dt_bias[l]), pad16(a_log[l]),
                  jnp.repeat(ssd_d[l], SSD_HEAD_DIM).reshape(1, D),
                  ssd_norm_w[l].reshape(1, D), expand)

        bd, bxr8, bxi8, cxr8, cxi8, alr, ali = _s5_prep(*s5_rows, *s5_cols, *s5_mats, scat, l)
        n_oct = S5_GROUPS // S5_OCT
        u8 = (proj[:, C_S5U * D:(C_S5U + 1) * D].reshape(nc, S5_L, n_oct, LANE)
              .transpose(2, 0, 1, 3).reshape(n_oct, nc, S5_L * LANE))
        sr, si = _s5_local(u8, bxr8, bxi8)
        pr, pi = _s5_scan(sr, si, alr.reshape(1, -1), ali.reshape(1, -1), 512)
        y8 = _s5_out(u8, bd, pr, pi, cxr8, cxi8)
        ys = y8.reshape(n_oct, nc, S5_L, LANE).transpose(1, 2, 0, 3).reshape(s, D)

        km, kaug = _kprep(proj, math.gcd(nb, 8))
        att_t = _moba(proj, kaug, vt, km)

        xc = _merge(l == depth - 1, xc, ya, ys, proj, att_t, gate_b[l].reshape(1, 3 * D),
                    s5_d[l].reshape(1, D), glu_b[l].reshape(1, D), final_norm_w.reshape(1, D),
                    *merge_w, l, _row_tile(s, 512))
    return xc.reshape(b, s, D)
```
